```python
import math
import jax, jax.numpy as jnp
from jax import lax
import numpy as np

D_MODEL = 2048
BATCH = 4
SEQ = 2048
DEPTH = 1
DEC_BATCH = 32
DEC_SEQ = 1
PAST_LEN = 8192
PAGE_SIZE = 128

N_HEADS = 8
HEAD_DIM = 64
V_DIM = 2 * HEAD_DIM
ATTN_QK = N_HEADS * 2 * HEAD_DIM
ATTN_WIDTH = N_HEADS * V_DIM
D_CONV = 1024
CONV_WIDTH = 3
D_FF = 5632
ROPE_THETA = 500000.0
ROT_DIM = HEAD_DIM // 4
Q_BLOCK = 128
LN_EPS = 1e-5
NEG_INF = -1e30
DEEPNORM_ALPHA = (2.0 * DEPTH) ** 0.25
DEEPNORM_BETA = (8.0 * DEPTH) ** -0.25
SPLIT_SIZES = (ATTN_QK, ATTN_QK, ATTN_WIDTH, D_CONV, D_CONV, D_CONV, D_MODEL, D_MODEL)
SPLIT_OFFSETS = [int(o) for o in np.cumsum(SPLIT_SIZES)[:-1]]
D_IN = int(sum(SPLIT_SIZES))

kernel_name = 'hybrid_diffattn_shortconv_macaron_deepnorm_step'


def layer_norm(x, g, b):
    xf = x.astype(jnp.float32)
    mu = jnp.mean(xf, axis=-1, keepdims=True)
    var = jnp.mean(jnp.square(xf - mu), axis=-1, keepdims=True)
    return ((xf - mu) * lax.rsqrt(var + LN_EPS)).astype(x.dtype) * g + b


def head_rmsnorm(o, g):
    of = o.astype(jnp.float32)
    return (of * lax.rsqrt(jnp.mean(of * of, axis=-1, keepdims=True) + LN_EPS)).astype(o.dtype) * g


def swiglu(x, w_gate, w_up, w_down):
    return (jax.nn.silu(x @ w_gate) * (x @ w_up)) @ w_down


def rope(x, pos):
    half = ROT_DIM // 2
    inv_freq = jnp.power(ROPE_THETA, -jnp.arange(0, ROT_DIM, 2, dtype=jnp.float32) / ROT_DIM)
    ang = pos.astype(jnp.float32)[:, None] * inv_freq[None, :]
    cos = jnp.cos(ang)[None, :, None, None, :].astype(x.dtype)
    sin = jnp.sin(ang)[None, :, None, None, :].astype(x.dtype)
    x1 = x[..., :half]
    x2 = x[..., half:ROT_DIM]
    return jnp.concatenate([x1 * cos - x2 * sin, x2 * cos + x1 * sin, x[..., ROT_DIM:]], axis=-1)


def diff_weights(s, lam):
    p = jax.nn.softmax(s, axis=-1)
    return p[:, :, 0] - lam * p[:, :, 1]


def prompt_attention(q, k, v, pos, lam):
    B, T = q.shape[0], q.shape[1]
    nb = T // Q_BLOCK
    scale = HEAD_DIM ** -0.5
    qb = q.reshape(B, nb, Q_BLOCK, N_HEADS, 2, HEAD_DIM).transpose(1, 0, 2, 3, 4, 5)
    pb = pos.reshape(nb, Q_BLOCK)

    def one_block(args):
        q_blk, p_blk = args
        s = jnp.einsum('bqhcd,bkhcd->bhcqk', q_blk, k).astype(jnp.float32) * scale
        mask = pos[None, :] <= p_blk[:, None]
        s = jnp.where(mask, s, NEG_INF)
        a = diff_weights(s, lam)
        return jnp.einsum('bhqk,bkhe->bqhe', a.astype(v.dtype), v)

    o = lax.map(one_block, (qb, pb))
    return o.transpose(1, 0, 2, 3, 4).reshape(B, T, N_HEADS, V_DIM)


def sample_attention(q, k_new, v_new, k_past, v_past, qpos, lam):
    P = k_past.shape[1]
    scale = HEAD_DIM ** -0.5
    s_past = jnp.einsum('bqhcd,bkhcd->bhcqk', q, k_past).astype(jnp.float32) * scale
    s_new = jnp.einsum('bqhcd,bkhcd->bhcqk', q, k_new).astype(jnp.float32) * scale
    s_new = jnp.where(qpos[None, :] <= qpos[:, None], s_new, NEG_INF)
    a = diff_weights(jnp.concatenate([s_past, s_new], axis=-1), lam).astype(v_new.dtype)
    return (jnp.einsum('bhqk,bkhe->bqhe', a[..., :P], v_past)
            + jnp.einsum('bhqk,bkhe->bqhe', a[..., P:], v_new))


def short_conv(u, buf, w):
    T = u.shape[1]
    up = jnp.concatenate([buf, u], axis=1)
    y = w[0] * up[:, 0:T] + w[1] * up[:, 1:T + 1] + w[2] * up[:, 2:T + 2]
    return y, up[:, T:]


def decoder_layer(x, pos, conv_buf, k_past, v_past, lambda_init, ln_g, ln_b,
                  ffn1_w_gate, ffn1_w_up, ffn1_w_down, w_in, conv_w,
                  lambda_q1, lambda_k1, lambda_q2, lambda_k2, subln_g,
                  w_attn_out, w_conv_out, w_o, ffn2_w_gate, ffn2_w_up, ffn2_w_down):
    B, T, _ = x.shape
    x = layer_norm(DEEPNORM_ALPHA * x + 0.5 * swiglu(x, ffn1_w_gate, ffn1_w_up, ffn1_w_down), ln_g[0], ln_b[0])
    z = x @ w_in
    q, k, v, c_b, c_c, c_h, g_attn, g_conv = jnp.split(z, SPLIT_OFFSETS, axis=-1)
    q = rope(q.reshape(B, T, N_HEADS, 2, HEAD_DIM), pos)
    k = rope(k.reshape(B, T, N_HEADS, 2, HEAD_DIM), pos)
    v = v.reshape(B, T, N_HEADS, V_DIM)
    lam = (jnp.exp(jnp.sum(lambda_q1.astype(jnp.float32) * lambda_k1.astype(jnp.float32)))
           - jnp.exp(jnp.sum(lambda_q2.astype(jnp.float32) * lambda_k2.astype(jnp.float32)))
           + lambda_init)
    if k_past is None:
        o = prompt_attention(q, k, v, pos, lam)
    else:
        o = sample_attention(q, k, v, k_past, v_past, pos, lam)
    o = head_rmsnorm(o, subln_g) * (1.0 - lambda_init)
    y_attn = o.reshape(B, T, ATTN_WIDTH) @ w_attn_out
    conv_out, new_buf = short_conv(c_c * c_h, conv_buf, conv_w)
    y_conv = (c_b * conv_out) @ w_conv_out
    merged = jax.nn.sigmoid(g_attn) * y_attn + jax.nn.sigmoid(g_conv) * y_conv
    x = layer_norm(DEEPNORM_ALPHA * x + merged @ w_o, ln_g[1], ln_b[1])
    x = layer_norm(DEEPNORM_ALPHA * x + 0.5 * swiglu(x, ffn2_w_gate, ffn2_w_up, ffn2_w_down), ln_g[2], ln_b[2])
    return x, k.reshape(B, T, N_HEADS, V_DIM), v, new_buf


def setup_inputs(seed: int = 0) -> dict:
    key = jax.random.key(seed)
    ks = jax.random.split(key, 24)
    n_pages = PAST_LEN // PAGE_SIZE
    n_used = DEC_BATCH * n_pages
    n_pool = n_used + (n_used + 3) // 4

    def nrm(k, shape, scale):
        return jax.random.normal(k, shape, jnp.float32) * scale

    return {
        'x_prompt': nrm(ks[0], (BATCH, SEQ, D_MODEL), 1.0),
        'x_sample': nrm(ks[1], (DEC_BATCH, DEC_SEQ, D_MODEL), 1.0),
        'cache_k': nrm(ks[2], (DEPTH, n_pool, PAGE_SIZE, N_HEADS, V_DIM), 1.0),
        'cache_v': nrm(ks[3], (DEPTH, n_pool, PAGE_SIZE, N_HEADS, V_DIM), 1.0),
        'state_conv': nrm(ks[4], (DEPTH, DEC_BATCH, CONV_WIDTH - 1, D_CONV), 1.0),
        'page_table': jax.random.permutation(ks[5], n_pool)[:n_used].reshape(DEC_BATCH, n_pages).astype(jnp.int32),
        'ln_g': 1.0 + nrm(ks[6], (DEPTH, 3, D_MODEL), 0.02),
        'ln_b': nrm(ks[7], (DEPTH, 3, D_MODEL), 0.02),
        'ffn1_w_gate': nrm(ks[8], (DEPTH, D_MODEL, D_FF), D_MODEL ** -0.5),
        'ffn1_w_up': nrm(ks[9], (DEPTH, D_MODEL, D_FF), D_MODEL ** -0.5),
        'ffn1_w_down': nrm(ks[10], (DEPTH, D_FF, D_MODEL), D_FF ** -0.5 * DEEPNORM_BETA),
        'w_in': nrm(ks[11], (DEPTH, D_MODEL, D_IN), D_MODEL ** -0.5),
        'conv_w': nrm(ks[12], (DEPTH, CONV_WIDTH, D_CONV), CONV_WIDTH ** -0.5),
        'lambda_q1': nrm(ks[13], (DEPTH, HEAD_DIM), 0.1),
        'lambda_k1': nrm(ks[14], (DEPTH, HEAD_DIM), 0.1),
        'lambda_q2': nrm(ks[15], (DEPTH, HEAD_DIM), 0.1),
        'lambda_k2': nrm(ks[16], (DEPTH, HEAD_DIM), 0.1),
        'subln_g': 1.0 + nrm(ks[17], (DEPTH, V_DIM), 0.02),
        'w_attn_out': nrm(ks[18], (DEPTH, ATTN_WIDTH, D_MODEL), ATTN_WIDTH ** -0.5 * DEEPNORM_BETA),
        'w_conv_out': nrm(ks[19], (DEPTH, D_CONV, D_MODEL), D_CONV ** -0.5 * DEEPNORM_BETA),
        'w_o': nrm(ks[20], (DEPTH, D_MODEL, D_MODEL), D_MODEL ** -0.5 * DEEPNORM_BETA),
        'ffn2_w_gate': nrm(ks[21], (DEPTH, D_MODEL, D_FF), D_MODEL ** -0.5),
        'ffn2_w_up': nrm(ks[22], (DEPTH, D_MODEL, D_FF), D_MODEL ** -0.5),
        'ffn2_w_down': nrm(ks[23], (DEPTH, D_FF, D_MODEL), D_FF ** -0.5 * DEEPNORM_BETA),
    }


def reference(x_prompt, x_sample, cache_k, cache_v, state_conv, page_table,
              ln_g, ln_b, ffn1_w_gate, ffn1_w_up, ffn1_w_down, w_in, conv_w,
              lambda_q1, lambda_k1, lambda_q2, lambda_k2, subln_g,
              w_attn_out, w_conv_out, w_o, ffn2_w_gate, ffn2_w_up, ffn2_w_down):
    B, T = x_prompt.shape[0], x_prompt.shape[1]
    DB, DT = x_sample.shape[0], x_sample.shape[1]
    past_len = page_table.shape[1] * PAGE_SIZE
    pos_p = jnp.arange(T, dtype=jnp.int32)
    pos_s = past_len + jnp.arange(DT, dtype=jnp.int32)
    y_p, y_s = x_prompt, x_sample
    kp_list, vp_list, cp_list, ks_list, vs_list, cs_list = [], [], [], [], [], []
    for layer in range(DEPTH):
        lambda_init = 0.8 - 0.6 * math.exp(-0.3 * layer)
        lp = (lambda_init, ln_g[layer], ln_b[layer], ffn1_w_gate[layer], ffn1_w_up[layer], ffn1_w_down[layer],
              w_in[layer], conv_w[layer], lambda_q1[layer], lambda_k1[layer], lambda_q2[layer], lambda_k2[layer],
              subln_g[layer], w_attn_out[layer], w_conv_out[layer], w_o[layer],
              ffn2_w_gate[layer], ffn2_w_up[layer], ffn2_w_down[layer])
        buf0 = jnp.zeros((B, CONV_WIDTH - 1, D_CONV), x_prompt.dtype)
        y_p, k_p, v_p, c_p = decoder_layer(y_p, pos_p, buf0, None, None, *lp)
        k_past = cache_k[layer, page_table].reshape(DB, past_len, N_HEADS, 2, HEAD_DIM)
        v_past = cache_v[layer, page_table].reshape(DB, past_len, N_HEADS, V_DIM)
        y_s, k_s, v_s, c_s = decoder_layer(y_s, pos_s, state_conv[layer], k_past, v_past, *lp)
        kp_list.append(k_p); vp_list.append(v_p); cp_list.append(c_p)
        ks_list.append(k_s); vs_list.append(v_s); cs_list.append(c_s)
    k_prompt = jnp.stack(kp_list)
    v_prompt = jnp.stack(vp_list)
    conv_prompt = jnp.stack(cp_list)
    k_sample = jnp.stack(ks_list)
    v_sample = jnp.stack(vs_list)
    conv_sample = jnp.stack(cs_list)
    return (y_prompt_out := y_p, y_s, k_prompt, v_prompt, conv_prompt, k_sample, v_sample, conv_sample)
```

```python
import functools
import math

import jax
import jax.numpy as jnp
from jax import lax
from jax.experimental import pallas as pl
from jax.experimental.pallas import tpu as pltpu

ROPE_THETA = 500000.0
LN_EPS = 1e-5
NEG_INF = -1e30
LANES = 128
SUBLANES = 8
VMEM_LIMIT = 56 * 1024 * 1024
BF16 = jnp.bfloat16
F32 = jnp.float32


def _layer_norm(y, g, b):
    mu = jnp.mean(y, axis=-1, keepdims=True)
    d = y - mu
    var = jnp.mean(d * d, axis=-1, keepdims=True)
    return d * lax.rsqrt(var + LN_EPS) * g + b


def _ffn_ln_kernel(x_ref, wg_ref, wu_ref, wd_ref, g_ref, b_ref, o_ref, xb_ref, *, alpha):
    f = pl.program_id(1)
    nf = pl.num_programs(1)

    @pl.when(f == 0)
    def _():
        xb_ref[...] = x_ref[...].astype(BF16)

    xb = xb_ref[...]
    hg = jnp.dot(xb, wg_ref[...], preferred_element_type=F32)
    hu = jnp.dot(xb, wu_ref[...], preferred_element_type=F32)
    h = (hg * jax.nn.sigmoid(hg) * hu).astype(BF16)
    part = jnp.dot(h, wd_ref[...], preferred_element_type=F32)

    @pl.when(f == 0)
    def _():
        o_ref[...] = part

    @pl.when(f > 0)
    def _():
        o_ref[...] += part

    @pl.when(f == nf - 1)
    def _():
        y = alpha * x_ref[...] + 0.5 * o_ref[...]
        o_ref[...] = _layer_norm(y, g_ref[...], b_ref[...])


def _ffn_ln(x, wg, wu, wd, g, b, *, alpha, tm, tf):
    m, d = x.shape
    dff = wg.shape[1]
    return pl.pallas_call(
        functools.partial(_ffn_ln_kernel, alpha=alpha),
        out_shape=jax.ShapeDtypeStruct((m, d), F32),
        grid=(m // tm, dff // tf),
        in_specs=[
            pl.BlockSpec((tm, d), lambda i, f: (i, 0)),
            pl.BlockSpec((d, tf), lambda i, f: (0, f)),
            pl.BlockSpec((d, tf), lambda i, f: (0, f)),
            pl.BlockSpec((tf, d), lambda i, f: (f, 0)),
            pl.BlockSpec((1, d), lambda i, f: (0, 0)),
            pl.BlockSpec((1, d), lambda i, f: (0, 0)),
        ],
        out_specs=pl.BlockSpec((tm, d), lambda i, f: (i, 0)),
        scratch_shapes=[pltpu.VMEM((tm, d), BF16)],
        compiler_params=pltpu.CompilerParams(
            dimension_semantics=("parallel", "arbitrary"), vmem_limit_bytes=VMEM_LIMIT),
        name="ffn_ln",
    )(x, wg, wu, wd, g, b)


def _rope_tile(z, cos, sin_lo, sin_hi, rot_half):
    up = pltpu.roll(z, LANES - rot_half, 1)
    down = pltpu.roll(z, rot_half, 1)
    return z * cos + up * sin_lo + down * sin_hi


def _inproj_kernel(x_ref, w_ref, cos_ref, slo_ref, shi_ref,
                   q_ref, k_ref, v_ref, kb_ref, vb_ref, r_ref, xb_ref,
                   *, q_scale, rot_half, n_heads):
    j = pl.program_id(1)

    @pl.when(j == 0)
    def _():
        xb_ref[...] = x_ref[...].astype(BF16)

    z = jnp.dot(xb_ref[...], w_ref[...], preferred_element_type=F32)

    @pl.when(j == 0)
    def _():
        for h in range(n_heads):
            sl = slice(h * LANES, (h + 1) * LANES)
            r = _rope_tile(z[:, sl], cos_ref[...], slo_ref[...], shi_ref[...], rot_half)
            q_ref[:, sl] = (r * q_scale).astype(BF16)

    @pl.when(j == 1)
    def _():
        for h in range(n_heads):
            sl = slice(h * LANES, (h + 1) * LANES)
            r = _rope_tile(z[:, sl], cos_ref[...], slo_ref[...], shi_ref[...], rot_half)
            k_ref[:, sl] = r
            kb_ref[:, sl] = r.astype(BF16)

    @pl.when(j == 2)
    def _():
        v_ref[...] = z
        vb_ref[...] = z.astype(BF16)

    @pl.when(j >= 3)
    def _():
        r_ref[...] = z


def _inproj(x, w, cos, slo, shi, *, tm, tn, q_scale, rot_half, n_heads):
    m, d = x.shape
    n = w.shape[1]
    nj = n // tn
    nrest = n - 3 * tn
    tab_blocks = cos.shape[0] // tm
    tab_spec = pl.BlockSpec((tm, LANES), lambda i, j: (i % tab_blocks, 0))
    row_spec = pl.BlockSpec((tm, tn), lambda i, j: (i, 0))
    return pl.pallas_call(
        functools.partial(_inproj_kernel, q_scale=q_scale, rot_half=rot_half, n_heads=n_heads),
        out_shape=(
            jax.ShapeDtypeStruct((m, tn), BF16),
            jax.ShapeDtypeStruct((m, tn), F32),
            jax.ShapeDtypeStruct((m, tn), F32),
            jax.ShapeDtypeStruct((m, tn), BF16),
            jax.ShapeDtypeStruct((m, tn), BF16),
            jax.ShapeDtypeStruct((m, nrest), F32),
        ),
        grid=(m // tm, nj),
        in_specs=[
            pl.BlockSpec((tm, d), lambda i, j: (i, 0)),
            pl.BlockSpec((d, tn), lambda i, j: (0, j)),
            tab_spec, tab_spec, tab_spec,
        ],
        out_specs=(row_spec, row_spec, row_spec, row_spec, row_spec,
                   pl.BlockSpec((tm, tn), lambda i, j: (i, jnp.maximum(j - 3, 0)))),
        scratch_shapes=[pltpu.VMEM((tm, d), BF16)],
        compiler_params=pltpu.CompilerParams(
            dimension_semantics=("parallel", "arbitrary"), vmem_limit_bytes=VMEM_LIMIT),
        name="inproj",
    )(x, w, cos, slo, shi)


def _lambda_value(lq1_ref, lk1_ref, lq2_ref, lk2_ref, lam_init):
    s1 = jnp.sum(lq1_ref[...] * lk1_ref[...], axis=-1, keepdims=True)
    s2 = jnp.sum(lq2_ref[...] * lk2_ref[...], axis=-1, keepdims=True)
    return jnp.exp(s1) - jnp.exp(s2) + lam_init


def _head_rmsnorm(o, g, lam_init):
    ms = jnp.mean(o * o, axis=-1, keepdims=True)
    return o * lax.rsqrt(ms + LN_EPS) * g * (1.0 - lam_init)


def _prompt_attn_kernel(q_ref, k_ref, v_ref, lq1_ref, lk1_ref, lq2_ref, lk2_ref, g_ref,
                        o_ref, m_ref, l_ref, acc_ref, *, lam_init, tq, head_dim):
    qi = pl.program_id(2)
    q = q_ref[...]
    lane = lax.broadcasted_iota(jnp.int32, q.shape, 1)
    zero = jnp.zeros_like(q)
    qs = jnp.concatenate([jnp.where(lane < head_dim, q, zero),
                          jnp.where(lane >= head_dim, q, zero)], axis=0)

    m_ref[...] = jnp.full(m_ref.shape, NEG_INF, F32)
    l_ref[...] = jnp.zeros(l_ref.shape, F32)
    acc_ref[...] = jnp.zeros(acc_ref.shape, F32)

    def block(j, masked):
        start = pl.multiple_of(j * tq, tq)
        kb = k_ref[pl.ds(start, tq), :]
        vb = v_ref[pl.ds(start, tq), :]
        s = lax.dot_general(qs, kb, (((1,), (1,)), ((), ())), preferred_element_type=F32)
        if masked:
            row = lax.broadcasted_iota(jnp.int32, (tq, tq), 0)
            col = lax.broadcasted_iota(jnp.int32, (tq, tq), 1)
            keep = jnp.concatenate([col <= row, col <= row], axis=0)
            s = jnp.where(keep, s, NEG_INF)
        m_old = m_ref[...]
        m_new = jnp.maximum(m_old, jnp.max(s, axis=-1, keepdims=True))
        alpha = jnp.exp(m_old - m_new)
        p = jnp.exp(s - m_new)
        l_ref[...] = alpha * l_ref[...] + jnp.sum(p, axis=-1, keepdims=True)
        acc_ref[...] = alpha * acc_ref[...] + jnp.dot(p.astype(BF16), vb, preferred_element_type=F32)
        m_ref[...] = m_new

    def body(j, carry):
        block(j, False)
        return carry

    lax.fori_loop(0, qi, body, 0)
    block(qi, True)

    o = acc_ref[...] / l_ref[...]
    lam = _lambda_value(lq1_ref, lk1_ref, lq2_ref, lk2_ref, lam_init)
    o = o[:tq] - lam * o[tq:]
    o_ref[...] = _head_rmsnorm(o, g_ref[...], lam_init).astype(BF16)


def _prompt_attn(q, kb, vb, lam_vecs, subln_g, *, batch, seq, n_heads, lam_init, tq):
    m, width = q.shape
    nq = seq // tq
    vd = width // n_heads
    vec = pl.BlockSpec((1, lam_vecs[0].shape[1]), lambda b, h, i: (0, 0))
    return pl.pallas_call(
        functools.partial(_prompt_attn_kernel, lam_init=lam_init, tq=tq, head_dim=vd // 2),
        out_shape=jax.ShapeDtypeStruct((m, width), BF16),
        grid=(batch, n_heads, nq),
        in_specs=[
            pl.BlockSpec((tq, vd), lambda b, h, i: (b * nq + i, h)),
            pl.BlockSpec((seq, vd), lambda b, h, i: (b, h)),
            pl.BlockSpec((seq, vd), lambda b, h, i: (b, h)),
            vec, vec, vec, vec,
            pl.BlockSpec((1, vd), lambda b, h, i: (0, 0)),
        ],
        out_specs=pl.BlockSpec((tq, vd), lambda b, h, i: (b * nq + i, h)),
        scratch_shapes=[pltpu.VMEM((2 * tq, 1), F32), pltpu.VMEM((2 * tq, 1), F32),
                        pltpu.VMEM((2 * tq, vd), F32)],
        compiler_params=pltpu.CompilerParams(
            dimension_semantics=("parallel", "parallel", "arbitrary"), vmem_limit_bytes=VMEM_LIMIT),
        name="prompt_attn",
    )(q, kb, vb, *lam_vecs, subln_g)


def _class_reduce(x, op, n_classes):
    shift = n_classes
    while shift < x.shape[-1]:
        x = op(x, pltpu.roll(x, shift, 1))
        shift *= 2
    return x


def _sample_attn_kernel(pt_ref, q_ref, kn_ref, vn_ref, lq1_ref, lk1_ref, lq2_ref, lk2_ref, g_ref,
                        *rest, lam_init, pages_per_step, n_pages):
    pp = pages_per_step
    k_refs = rest[:pp]
    v_refs = rest[pp:2 * pp]
    o_ref = rest[2 * pp]
    s_scr, a_scr, qt_scr, anew_scr, acc_scr = rest[2 * pp + 1:]
    g = pl.program_id(1)
    ng = n_pages // pp
    page, n_heads, vd = k_refs[0].shape
    rows = page * n_heads
    hd = vd // 2
    nt = (((1,), (1,)), ((), ()))

    def pad_rows(x):
        return jnp.concatenate([x, jnp.zeros((LANES - n_heads, vd), x.dtype)], axis=0)

    @pl.when(g == 0)
    def _():
        q = q_ref[...]
        qq = jnp.concatenate([q, q], axis=0)
        row = lax.broadcasted_iota(jnp.int32, qq.shape, 0)
        lane = lax.broadcasted_iota(jnp.int32, qq.shape, 1)
        qt_scr[...] = jnp.where(row // n_heads == lane // hd, qq, jnp.zeros_like(qq))

    @pl.when(g < ng)
    def _():
        for p in range(pp):
            kb = k_refs[p][...].reshape(rows, vd).astype(BF16)
            s_scr[g * pp + p] = lax.dot_general(qt_scr[...], kb, nt, preferred_element_type=F32)

    @pl.when(g == ng)
    def _():
        lane1 = lax.broadcasted_iota(jnp.int32, (2 * n_heads, LANES), 1)
        s_new = lax.dot_general(qt_scr[...], pad_rows(kn_ref[...]).astype(BF16), nt,
                                preferred_element_type=F32)
        s_new = jnp.where(lane1 < n_heads, s_new, NEG_INF)

        def with_new(x, x_new, op):
            return jnp.concatenate([op(x[:, :LANES], x_new), x[:, LANES:]], axis=1)

        m = lax.fori_loop(0, n_pages, lambda i, m: jnp.maximum(m, s_scr[i]),
                          jnp.full((2 * n_heads, rows), NEG_INF, F32))
        m = _class_reduce(with_new(m, s_new, jnp.maximum), jnp.maximum, n_heads)

        def sum_body(i, l):
            p = jnp.exp(s_scr[i] - m)
            s_scr[i] = p
            return l + p
        l = lax.fori_loop(0, n_pages, sum_body, jnp.zeros((2 * n_heads, rows), F32))
        p_new = jnp.exp(s_new - m[:, :LANES])
        l = _class_reduce(with_new(l, p_new, jnp.add), jnp.add, n_heads)

        lam = _lambda_value(lq1_ref, lk1_ref, lq2_ref, lk2_ref, lam_init)
        row = lax.broadcasted_iota(jnp.int32, (n_heads, rows), 0)
        lane = lax.broadcasted_iota(jnp.int32, (n_heads, rows), 1)
        own = row == lane % n_heads

        def a_body(i, c):
            pn = s_scr[i] / l
            a_scr[i] = jnp.where(own, pn[:n_heads] - lam * pn[n_heads:], 0.0)
            return c
        lax.fori_loop(0, n_pages, a_body, 0)
        pn_new = p_new / l[:, :LANES]
        anew_scr[...] = jnp.where(own[:, :LANES], pn_new[:n_heads] - lam * pn_new[n_heads:], 0.0)
        acc_scr[...] = jnp.zeros(acc_scr.shape, F32)

    @pl.when(g >= ng)
    def _():
        acc = acc_scr[...]
        for p in range(pp):
            vb = v_refs[p][...].reshape(rows, vd).astype(BF16)
            acc = acc + jnp.dot(a_scr[(g - ng) * pp + p].astype(BF16), vb, preferred_element_type=F32)
        acc_scr[...] = acc

    @pl.when(g == 2 * ng - 1)
    def _():
        o = acc_scr[...] + jnp.dot(anew_scr[...].astype(BF16), pad_rows(vn_ref[...]).astype(BF16),
                                   preferred_element_type=F32)
        o_ref[...] = _head_rmsnorm(o, g_ref[...], lam_init).astype(o_ref.dtype)


def _sample_attn(page_table, q, k_new, v_new, cache_k, cache_v, lam_vecs, subln_g,
                 *, lam_init, pages_per_step):
    nb = q.shape[0]
    n_pages = page_table.shape[1]
    pp = pages_per_step
    ng = n_pages // pp
    _, page, n_heads, vd = cache_k.shape
    rows = page * n_heads
    pt_flat = page_table.reshape(-1)

    head_spec = pl.BlockSpec((None, n_heads, vd), lambda b, g, pt: (b, 0, 0))
    vec = pl.BlockSpec((1, lam_vecs[0].shape[1]), lambda b, g, pt: (0, 0))

    def k_spec(p):
        return pl.BlockSpec((None, page, n_heads, vd),
                            lambda b, g, pt: (pt[b * n_pages + jnp.minimum(g, ng - 1) * pp + p], 0, 0, 0))

    def v_spec(p):
        return pl.BlockSpec((None, page, n_heads, vd),
                            lambda b, g, pt: (pt[b * n_pages + jnp.maximum(g - ng, 0) * pp + p], 0, 0, 0))

    grid_spec = pltpu.PrefetchScalarGridSpec(
        num_scalar_prefetch=1,
        grid=(nb, 2 * ng),
        in_specs=[head_spec, head_spec, head_spec, vec, vec, vec, vec,
                  pl.BlockSpec((1, vd), lambda b, g, pt: (0, 0))]
                 + [k_spec(p) for p in range(pp)] + [v_spec(p) for p in range(pp)],
        out_specs=head_spec,
        scratch_shapes=[
            pltpu.VMEM((n_pages, 2 * n_heads, rows), F32),
            pltpu.VMEM((n_pages, n_heads, rows), F32),
            pltpu.VMEM((2 * n_heads, vd), BF16),
            pltpu.VMEM((n_heads, LANES), F32),
            pltpu.VMEM((n_heads, vd), F32),
        ],
    )
    return pl.pallas_call(
        functools.partial(_sample_attn_kernel, lam_init=lam_init, pages_per_step=pp, n_pages=n_pages),
        out_shape=jax.ShapeDtypeStruct((nb, n_heads, vd), BF16),
        grid_spec=grid_spec,
        compiler_params=pltpu.CompilerParams(
            dimension_semantics=("parallel", "arbitrary"), vmem_limit_bytes=VMEM_LIMIT),
        name="sample_attn",
    )(pt_flat, q, k_new, v_new, *lam_vecs, subln_g, *([cache_k] * pp), *([cache_v] * pp))


def _mixer_tail(o, cb, conv, gate_refs, x, wa_ref, wc_ref, wo_ref, g_ref, b_ref, alpha):
    ga0_ref, ga1_ref, gc0_ref, gc1_ref = gate_refs
    ga = jnp.concatenate([ga0_ref[...], ga1_ref[...]], axis=1)
    gc = jnp.concatenate([gc0_ref[...], gc1_ref[...]], axis=1)
    y_attn = jnp.dot(o, wa_ref[...], preferred_element_type=F32)
    y_conv = jnp.dot((cb * conv).astype(BF16), wc_ref[...], preferred_element_type=F32)
    merged = jax.nn.sigmoid(ga) * y_attn + jax.nn.sigmoid(gc) * y_conv
    z = jnp.dot(merged.astype(BF16), wo_ref[...], preferred_element_type=F32)
    return _layer_norm(alpha * x + z, g_ref[...], b_ref[...])


def _merge_prompt_kernel(o_ref, cb_ref, cc_ref, ch_ref, pc_ref, ph_ref,
                         ga0_ref, ga1_ref, gc0_ref, gc1_ref, x_ref,
                         cw_ref, wa_ref, wc_ref, wo_ref, g_ref, b_ref,
                         out_ref, tail_ref, u_scr, *, alpha, tiles_per_seq):
    i = pl.program_id(0)
    tm = cc_ref.shape[0]
    u = cc_ref[...] * ch_ref[...]
    prev = pc_ref[...] * ph_ref[...]
    prev = jnp.where(i % tiles_per_seq == 0, jnp.zeros_like(prev), prev)
    u_scr[0:SUBLANES, :] = prev
    u_scr[SUBLANES:SUBLANES + tm, :] = u
    u1 = u_scr[SUBLANES - 1:SUBLANES - 1 + tm, :]
    u2 = u_scr[SUBLANES - 2:SUBLANES - 2 + tm, :]
    conv = cw_ref[0:1, :] * u2 + cw_ref[1:2, :] * u1 + cw_ref[2:3, :] * u
    tail_ref[...] = u[tm - SUBLANES:tm, :]
    out_ref[...] = _mixer_tail(o_ref[...], cb_ref[...], conv, (ga0_ref, ga1_ref, gc0_ref, gc1_ref),
                               x_ref[...], wa_ref, wc_ref, wo_ref, g_ref, b_ref, alpha)


def _merge_sample_kernel(o_ref, cb_ref, cc_ref, ch_ref, s0_ref, s1_ref,
                         ga0_ref, ga1_ref, gc0_ref, gc1_ref, x_ref,
                         cw_ref, wa_ref, wc_ref, wo_ref, g_ref, b_ref,
                         out_ref, u_ref, *, alpha):
    u = cc_ref[...] * ch_ref[...]
    conv = cw_ref[0:1, :] * s0_ref[...] + cw_ref[1:2, :] * s1_ref[...] + cw_ref[2:3, :] * u
    u_ref[...] = u
    out_ref[...] = _mixer_tail(o_ref[...], cb_ref[...], conv, (ga0_ref, ga1_ref, gc0_ref, gc1_ref),
                               x_ref[...], wa_ref, wc_ref, wo_ref, g_ref, b_ref, alpha)


def _const_spec(shape):
    return pl.BlockSpec(shape, lambda i: (0,) * len(shape), pipeline_mode=pl.Buffered(1))


def _merge_prompt(o, rest, x, conv_w, wa, wc, wo, g, b, *, alpha, tm, seq):
    m, d = x.shape
    dc = o.shape[1]
    rows8 = tm // SUBLANES
    in_specs = [
        pl.BlockSpec((tm, dc), lambda i: (i, 0)),
        pl.BlockSpec((tm, dc), lambda i: (i, 0)),
        pl.BlockSpec((tm, dc), lambda i: (i, 1)),
        pl.BlockSpec((tm, dc), lambda i: (i, 2)),
        pl.BlockSpec((SUBLANES, dc), lambda i: (jnp.maximum(i * rows8 - 1, 0), 1)),
        pl.BlockSpec((SUBLANES, dc), lambda i: (jnp.maximum(i * rows8 - 1, 0), 2)),
        pl.BlockSpec((tm, dc), lambda i: (i, 3)),
        pl.BlockSpec((tm, dc), lambda i: (i, 4)),
        pl.BlockSpec((tm, dc), lambda i: (i, 5)),
        pl.BlockSpec((tm, dc), lambda i: (i, 6)),
        pl.BlockSpec((tm, d), lambda i: (i, 0)),
        _const_spec(conv_w.shape), _const_spec(wa.shape), _const_spec(wc.shape), _const_spec(wo.shape),
        _const_spec(g.shape), _const_spec(b.shape),
    ]
    return pl.pallas_call(
        functools.partial(_merge_prompt_kernel, alpha=alpha, tiles_per_seq=seq // tm),
        out_shape=(jax.ShapeDtypeStruct((m, d), F32),
                   jax.ShapeDtypeStruct((m // tm * SUBLANES, dc), F32)),
        grid=(m // tm,),
        in_specs=in_specs,
        out_specs=(pl.BlockSpec((tm, d), lambda i: (i, 0)),
                   pl.BlockSpec((SUBLANES, dc), lambda i: (i, 0))),
        scratch_shapes=[pltpu.VMEM((tm + SUBLANES, dc), F32)],
        compiler_params=pltpu.CompilerParams(
            dimension_semantics=("parallel",), vmem_limit_bytes=VMEM_LIMIT),
        name="merge_prompt",
    )(o, rest, rest, rest, rest, rest, rest, rest, rest, rest, x, conv_w, wa, wc, wo, g, b)


def _merge_sample(o, rest, s0, s1, x, conv_w, wa, wc, wo, g, b, *, alpha):
    m, d = x.shape
    dc = o.shape[1]
    in_specs = [
        pl.BlockSpec((m, dc), lambda i: (0, 0)),
        pl.BlockSpec((m, dc), lambda i: (0, 0)),
        pl.BlockSpec((m, dc), lambda i: (0, 1)),
        pl.BlockSpec((m, dc), lambda i: (0, 2)),
        pl.BlockSpec((m, dc), lambda i: (0, 0)),
        pl.BlockSpec((m, dc), lambda i: (0, 0)),
        pl.BlockSpec((m, dc), lambda i: (0, 3)),
        pl.BlockSpec((m, dc), lambda i: (0, 4)),
        pl.BlockSpec((m, dc), lambda i: (0, 5)),
        pl.BlockSpec((m, dc), lambda i: (0, 6)),
        pl.BlockSpec((m, d), lambda i: (0, 0)),
        _const_spec(conv_w.shape), _const_spec(wa.shape), _const_spec(wc.shape), _const_spec(wo.shape),
        _const_spec(g.shape), _const_spec(b.shape),
    ]
    return pl.pallas_call(
        functools.partial(_merge_sample_kernel, alpha=alpha),
        out_shape=(jax.ShapeDtypeStruct((m, d), F32), jax.ShapeDtypeStruct((m, dc), F32)),
        grid=(1,),
        in_specs=in_specs,
        out_specs=(pl.BlockSpec((m, d), lambda i: (0, 0)), pl.BlockSpec((m, dc), lambda i: (0, 0))),
        compiler_params=pltpu.CompilerParams(
            dimension_semantics=("arbitrary",), vmem_limit_bytes=VMEM_LIMIT),
        name="merge_sample",
    )(o, rest, rest, rest, s0, s1, rest, rest, rest, rest, x, conv_w, wa, wc, wo, g, b)


def _rope_tables(pos, head_dim, rot_dim):
    half = rot_dim // 2
    inv_freq = jnp.power(ROPE_THETA, -jnp.arange(0, rot_dim, 2, dtype=F32) / rot_dim)
    ang = pos.astype(F32)[:, None] * inv_freq[None, :]
    cos, sin = jnp.cos(ang), jnp.sin(ang)
    n = pos.shape[0]
    ones = jnp.ones((n, head_dim - rot_dim), F32)
    zeros_h = jnp.zeros((n, half), F32)
    zeros_r = jnp.zeros((n, head_dim - rot_dim), F32)
    c = jnp.concatenate([cos, cos, ones], axis=1)
    lo = jnp.concatenate([-sin, zeros_h, zeros_r], axis=1)
    hi = jnp.concatenate([zeros_h, sin, zeros_r], axis=1)
    two = lambda t: jnp.concatenate([t, t], axis=1)
    return two(c), two(lo), two(hi)


def kernel(x_prompt, x_sample, cache_k, cache_v, state_conv, page_table, ln_g, ln_b, ffn1_w_gate, ffn1_w_up, ffn1_w_down, w_in, conv_w, lambda_q1, lambda_k1, lambda_q2, lambda_k2, subln_g, w_attn_out, w_conv_out, w_o, ffn2_w_gate, ffn2_w_up, ffn2_w_down):
    batch, seq, d_model = x_prompt.shape
    dec_batch, dec_seq, _ = x_sample.shape
    assert dec_seq == 1
    depth = ln_g.shape[0]
    page_size, n_heads, v_dim = cache_k.shape[2:]
    head_dim = v_dim // 2
    rot_dim = head_dim // 4
    width = n_heads * v_dim
    d_conv = conv_w.shape[2]
    assert width == d_conv and v_dim == LANES and n_heads == SUBLANES
    past_len = page_table.shape[1] * page_size
    alpha = (2.0 * depth) ** 0.25
    q_scale = head_dim ** -0.5

    tabs_p = _rope_tables(jnp.arange(seq, dtype=jnp.int32), head_dim, rot_dim)
    tabs_s = _rope_tables(jnp.full((dec_batch,), past_len, jnp.int32), head_dim, rot_dim)

    y_p = x_prompt.reshape(batch * seq, d_model)
    y_s = x_sample.reshape(dec_batch, d_model)
    outs = [[] for _ in range(6)]
    for layer in range(depth):
        lam_init = 0.8 - 0.6 * math.exp(-0.3 * layer)
        bf = lambda w: w[layer].astype(BF16)
        wg1, wu1, wd1 = bf(ffn1_w_gate), bf(ffn1_w_up), bf(ffn1_w_down)
        wg2, wu2, wd2 = bf(ffn2_w_gate), bf(ffn2_w_up), bf(ffn2_w_down)
        win, wa, wc, wo = bf(w_in), bf(w_attn_out), bf(w_conv_out), bf(w_o)
        g = [ln_g[layer, i][None, :] for i in range(3)]
        b = [ln_b[layer, i][None, :] for i in range(3)]
        lam_vecs = [v[layer][None, :] for v in (lambda_q1, lambda_k1, lambda_q2, lambda_k2)]
        sg = subln_g[layer][None, :]
        cw = conv_w[layer]

        x1 = _ffn_ln(y_p, wg1, wu1, wd1, g[0], b[0], alpha=alpha, tm=512, tf=512)
        q, k, v, kb, vb, rest = _inproj(x1, win, *tabs_p, tm=512, tn=width, q_scale=q_scale,
                                        rot_half=rot_dim // 2, n_heads=n_heads)
        o = _prompt_attn(q, kb, vb, lam_vecs, sg, batch=batch, seq=seq, n_heads=n_heads,
                         lam_init=lam_init, tq=256)
        x2, tails = _merge_prompt(o, rest, x1, cw, wa, wc, wo, g[1], b[1], alpha=alpha, tm=256, seq=seq)
        y_p = _ffn_ln(x2, wg2, wu2, wd2, g[2], b[2], alpha=alpha, tm=512, tf=512)
        tails = tails.reshape(batch, seq // 256, SUBLANES, d_conv)
        outs[0].append(k.reshape(batch, seq, n_heads, v_dim))
        outs[1].append(v.reshape(batch, seq, n_heads, v_dim))
        outs[2].append(tails[:, -1, SUBLANES - 2:, :])

        s1 = _ffn_ln(y_s, wg1, wu1, wd1, g[0], b[0], alpha=alpha, tm=dec_batch, tf=512)
        qs, ks, vs, _, _, rest_s = _inproj(s1, win, *tabs_s, tm=dec_batch, tn=width, q_scale=q_scale,
                                           rot_half=rot_dim // 2, n_heads=n_heads)
        heads = lambda t: t.reshape(dec_batch, n_heads, v_dim)
        os_ = _sample_attn(page_table, heads(qs), heads(ks), heads(vs), cache_k[layer], cache_v[layer],
                           lam_vecs, sg, lam_init=lam_init, pages_per_step=8)
        st = state_conv[layer]
        s2, u_s = _merge_sample(os_.reshape(dec_batch, width), rest_s, st[:, 0], st[:, 1], s1,
                                cw, wa, wc, wo, g[1], b[1], alpha=alpha)
        y_s = _ffn_ln(s2, wg2, wu2, wd2, g[2], b[2], alpha=alpha, tm=dec_batch, tf=512)
        outs[3].append(ks.reshape(dec_batch, 1, n_heads, v_dim))
        outs[4].append(vs.reshape(dec_batch, 1, n_heads, v_dim))
        outs[5].append(jnp.stack([st[:, 1], u_s], axis=1))

    return (y_p.reshape(batch, seq, d_model), y_s.reshape(dec_batch, 1, d_model),
            jnp.stack(outs[0]), jnp.stack(outs[1]), jnp.stack(outs[2]),
            jnp.stack(outs[3]), jnp.stack(outs[4]), jnp.stack(outs[5]))
```

```python
import functools
import math

import jax
import jax.numpy as jnp
from jax import lax
from jax.experimental import pallas as pl
from jax.experimental.pallas import tpu as pltpu

ROPE_THETA = 500000.0
LN_EPS = 1e-5
NEG_INF = -1e30
LANES = 128
SUBLANES = 8
VMEM_LIMIT = 56 * 1024 * 1024
BF16 = jnp.bfloat16
F32 = jnp.float32


def _layer_norm(y, g, b):
    mu = jnp.mean(y, axis=-1, keepdims=True)
    d = y - mu
    var = jnp.mean(d * d, axis=-1, keepdims=True)
    return d * lax.rsqrt(var + LN_EPS) * g + b


def _ffn_ln_kernel(x_ref, wg_ref, wu_ref, wd_ref, g_ref, b_ref, o_ref, xb_ref, *, alpha):
    f = pl.program_id(1)
    nf = pl.num_programs(1)

    @pl.when(f == 0)
    def _():
        xb_ref[...] = x_ref[...].astype(BF16)
        o_ref[...] = jnp.zeros(o_ref.shape, F32)

    xb = xb_ref[...]
    hg = jnp.dot(xb, wg_ref[...], preferred_element_type=F32)
    hu = jnp.dot(xb, wu_ref[...], preferred_element_type=F32)
    h = (hg * jax.nn.sigmoid(hg) * hu).astype(BF16)
    o_ref[...] += jnp.dot(h, wd_ref[...], preferred_element_type=F32)

    @pl.when(f == nf - 1)
    def _():
        y = alpha * x_ref[...] + 0.5 * o_ref[...]
        o_ref[...] = _layer_norm(y, g_ref[...], b_ref[...])


def _ffn_ln(x, wg, wu, wd, g, b, *, alpha, tm, tf):
    m, d = x.shape
    dff = wg.shape[1]
    return pl.pallas_call(
        functools.partial(_ffn_ln_kernel, alpha=alpha),
        out_shape=jax.ShapeDtypeStruct((m, d), F32),
        grid=(m // tm, dff // tf),
        in_specs=[
            pl.BlockSpec((tm, d), lambda i, f: (i, 0)),
            pl.BlockSpec((d, tf), lambda i, f: (0, f)),
            pl.BlockSpec((d, tf), lambda i, f: (0, f)),
            pl.BlockSpec((tf, d), lambda i, f: (f, 0)),
            pl.BlockSpec((1, d), lambda i, f: (0, 0)),
            pl.BlockSpec((1, d), lambda i, f: (0, 0)),
        ],
        out_specs=pl.BlockSpec((tm, d), lambda i, f: (i, 0)),
        scratch_shapes=[pltpu.VMEM((tm, d), BF16)],
        compiler_params=pltpu.CompilerParams(
            dimension_semantics=("parallel", "arbitrary"), vmem_limit_bytes=VMEM_LIMIT),
        name="ffn_ln",
    )(x, wg, wu, wd, g, b)


def _rope_tile(z, cos, sin_lo, sin_hi, rot_half):
    up = pltpu.roll(z, LANES - rot_half, 1)
    down = pltpu.roll(z, rot_half, 1)
    return z * cos + up * sin_lo + down * sin_hi


def _inproj_kernel(x_ref, w_ref, cos_ref, slo_ref, shi_ref,
                   q_ref, k_ref, v_ref, kb_ref, vb_ref, r_ref, xb_ref,
                   *, q_scale, rot_half, n_heads):
    j = pl.program_id(1)

    @pl.when(j == 0)
    def _():
        xb_ref[...] = x_ref[...].astype(BF16)

    r_ref[...] = jnp.dot(xb_ref[...], w_ref[...], preferred_element_type=F32)

    @pl.when(j == 0)
    def _():
        for h in range(n_heads):
            sl = slice(h * LANES, (h + 1) * LANES)
            r = _rope_tile(r_ref[:, sl], cos_ref[...], slo_ref[...], shi_ref[...], rot_half)
            q_ref[:, sl] = (r * q_scale).astype(BF16)

    @pl.when(j == 1)
    def _():
        for h in range(n_heads):
            sl = slice(h * LANES, (h + 1) * LANES)
            r = _rope_tile(r_ref[:, sl], cos_ref[...], slo_ref[...], shi_ref[...], rot_half)
            k_ref[:, sl] = r
            kb_ref[:, sl] = r.astype(BF16)

    @pl.when(j == 2)
    def _():
        v_ref[...] = r_ref[...]
        vb_ref[...] = r_ref[...].astype(BF16)


def _inproj(x, w, cos, slo, shi, *, tm, tn, q_scale, rot_half, n_heads):
    m, d = x.shape
    n = w.shape[1]
    nj = n // tn
    nrest = n - 3 * tn
    tab_blocks = cos.shape[0] // tm
    tab_spec = pl.BlockSpec((tm, LANES), lambda i, j: (i % tab_blocks, 0))
    row_spec = pl.BlockSpec((tm, tn), lambda i, j: (i, 0))
    return pl.pallas_call(
        functools.partial(_inproj_kernel, q_scale=q_scale, rot_half=rot_half, n_heads=n_heads),
        out_shape=(
            jax.ShapeDtypeStruct((m, tn), BF16),
            jax.ShapeDtypeStruct((m, tn), F32),
            jax.ShapeDtypeStruct((m, tn), F32),
            jax.ShapeDtypeStruct((m, tn), BF16),
            jax.ShapeDtypeStruct((m, tn), BF16),
            jax.ShapeDtypeStruct((m, nrest), F32),
        ),
        grid=(m // tm, nj),
        in_specs=[
            pl.BlockSpec((tm, d), lambda i, j: (i, 0)),
            pl.BlockSpec((d, tn), lambda i, j: (0, j)),
            tab_spec, tab_spec, tab_spec,
        ],
        out_specs=(row_spec, row_spec, row_spec, row_spec, row_spec,
                   pl.BlockSpec((tm, tn), lambda i, j: (i, jnp.maximum(j - 3, 0)))),
        scratch_shapes=[pltpu.VMEM((tm, d), BF16)],
        compiler_params=pltpu.CompilerParams(
            dimension_semantics=("parallel", "arbitrary"), vmem_limit_bytes=VMEM_LIMIT),
        name="inproj",
    )(x, w, cos, slo, shi)


def _lambda_value(lq1_ref, lk1_ref, lq2_ref, lk2_ref, lam_init):
    s1 = jnp.sum(lq1_ref[...] * lk1_ref[...], axis=-1, keepdims=True)
    s2 = jnp.sum(lq2_ref[...] * lk2_ref[...], axis=-1, keepdims=True)
    return jnp.exp(s1) - jnp.exp(s2) + lam_init


def _head_rmsnorm(o, g, lam_init):
    ms = jnp.mean(o * o, axis=-1, keepdims=True)
    return o * lax.rsqrt(ms + LN_EPS) * g * (1.0 - lam_init)


def _prompt_attn_kernel(q_ref, k_ref, v_ref, lq1_ref, lk1_ref, lq2_ref, lk2_ref, g_ref,
                        o_ref, *, lam_init, tq, head_dim, nq):
    qi = pl.program_id(2)
    q = q_ref[...]
    lane = lax.broadcasted_iota(jnp.int32, q.shape, 1)
    zero = jnp.zeros_like(q)
    qs = jnp.concatenate([jnp.where(lane < head_dim, q, zero),
                          jnp.where(lane >= head_dim, q, zero)], axis=0)
    row = lax.broadcasted_iota(jnp.int32, (tq, tq), 0)
    col = lax.broadcasted_iota(jnp.int32, (tq, tq), 1)
    keep = jnp.concatenate([col <= row, col <= row], axis=0)
    lam = _lambda_value(lq1_ref, lk1_ref, lq2_ref, lk2_ref, lam_init)

    for i in range(nq):
        @pl.when(qi == i)
        def _(i=i):
            ext = (i + 1) * tq
            s = lax.dot_general(qs, k_ref[0:ext, :], (((1,), (1,)), ((), ())), preferred_element_type=F32)
            diag = jnp.where(keep, s[:, ext - tq:], NEG_INF)
            s = diag if i == 0 else jnp.concatenate([s[:, :ext - tq], diag], axis=1)
            m = jnp.max(s, axis=-1, keepdims=True)
            p = jnp.exp(s - m)
            l = jnp.sum(p, axis=-1, keepdims=True)
            o = jnp.dot(p.astype(BF16), v_ref[0:ext, :], preferred_element_type=F32) / l
            o = o[:tq] - lam * o[tq:]
            o_ref[...] = _head_rmsnorm(o, g_ref[...], lam_init).astype(BF16)


def _prompt_attn(q, kb, vb, lam_vecs, subln_g, *, batch, seq, n_heads, lam_init, tq):
    m, width = q.shape
    nq = seq // tq
    vd = width // n_heads
    vec = pl.BlockSpec((1, lam_vecs[0].shape[1]), lambda b, h, i: (0, 0))
    return pl.pallas_call(
        functools.partial(_prompt_attn_kernel, lam_init=lam_init, tq=tq, head_dim=vd // 2, nq=nq),
        out_shape=jax.ShapeDtypeStruct((m, width), BF16),
        grid=(batch, n_heads, nq),
        in_specs=[
            pl.BlockSpec((tq, vd), lambda b, h, i: (b * nq + i, h)),
            pl.BlockSpec((seq, vd), lambda b, h, i: (b, h)),
            pl.BlockSpec((seq, vd), lambda b, h, i: (b, h)),
            vec, vec, vec, vec,
            pl.BlockSpec((1, vd), lambda b, h, i: (0, 0)),
        ],
        out_specs=pl.BlockSpec((tq, vd), lambda b, h, i: (b * nq + i, h)),
        compiler_params=pltpu.CompilerParams(
            dimension_semantics=("parallel", "parallel", "arbitrary"), vmem_limit_bytes=VMEM_LIMIT),
        name="prompt_attn",
    )(q, kb, vb, *lam_vecs, subln_g)


def _class_reduce(x, op, n_classes):
    shift = n_classes
    while shift < x.shape[-1]:
        x = op(x, pltpu.roll(x, shift, 1))
        shift *= 2
    return x


def _sample_attn_kernel(pt_ref, q_ref, kn_ref, vn_ref, lq1_ref, lk1_ref, lq2_ref, lk2_ref, g_ref,
                        *rest, lam_init, pages_per_step, n_pages):
    pp = pages_per_step
    k_refs = rest[:pp]
    v_refs = rest[pp:2 * pp]
    o_ref = rest[2 * pp]
    s_scr, a_scr, qt_scr, anew_scr, acc_scr = rest[2 * pp + 1:]
    g = pl.program_id(1)
    ng = n_pages // pp
    page, n_heads, vd = k_refs[0].shape
    rows = page * n_heads
    hd = vd // 2
    nt = (((1,), (1,)), ((), ()))

    def pad_rows(x):
        return jnp.concatenate([x, jnp.zeros((LANES - n_heads, vd), x.dtype)], axis=0)

    @pl.when(g == 0)
    def _():
        q = q_ref[...]
        qq = jnp.concatenate([q, q], axis=0)
        row = lax.broadcasted_iota(jnp.int32, qq.shape, 0)
        lane = lax.broadcasted_iota(jnp.int32, qq.shape, 1)
        qt_scr[...] = jnp.where(row // n_heads == lane // hd, qq, jnp.zeros_like(qq))

    @pl.when(g < ng)
    def _():
        for p in range(pp):
            kb = k_refs[p][...].reshape(rows, vd).astype(BF16)
            s_scr[g * pp + p] = lax.dot_general(qt_scr[...], kb, nt, preferred_element_type=F32)

    @pl.when(g == ng)
    def _():
        lane1 = lax.broadcasted_iota(jnp.int32, (2 * n_heads, LANES), 1)
        s_new = lax.dot_general(qt_scr[...], pad_rows(kn_ref[...]).astype(BF16), nt,
                                preferred_element_type=F32)
        s_new = jnp.where(lane1 < n_heads, s_new, NEG_INF)

        def with_new(x, x_new, op):
            return jnp.concatenate([op(x[:, :LANES], x_new), x[:, LANES:]], axis=1)

        m = lax.fori_loop(0, n_pages, lambda i, m: jnp.maximum(m, s_scr[i]),
                          jnp.full((2 * n_heads, rows), NEG_INF, F32))
        m = _class_reduce(with_new(m, s_new, jnp.maximum), jnp.maximum, n_heads)

        def sum_body(i, l):
            p = jnp.exp(s_scr[i] - m)
            s_scr[i] = p
            return l + p
        l = lax.fori_loop(0, n_pages, sum_body, jnp.zeros((2 * n_heads, rows), F32))
        p_new = jnp.exp(s_new - m[:, :LANES])
        l = _class_reduce(with_new(l, p_new, jnp.add), jnp.add, n_heads)

        lam = _lambda_value(lq1_ref, lk1_ref, lq2_ref, lk2_ref, lam_init)
        row = lax.broadcasted_iota(jnp.int32, (n_heads, rows), 0)
        lane = lax.broadcasted_iota(jnp.int32, (n_heads, rows), 1)
        own = row == lane % n_heads

        def a_body(i, c):
            pn = s_scr[i] / l
            a_scr[i] = jnp.where(own, pn[:n_heads] - lam * pn[n_heads:], 0.0)
            return c
        lax.fori_loop(0, n_pages, a_body, 0)
        pn_new = p_new / l[:, :LANES]
        anew_scr[...] = jnp.where(own[:, :LANES], pn_new[:n_heads] - lam * pn_new[n_heads:], 0.0)
        acc_scr[...] = jnp.zeros(acc_scr.shape, F32)

    @pl.when(g >= ng)
    def _():
        acc = acc_scr[...]
        for p in range(pp):
            vb = v_refs[p][...].reshape(rows, vd).astype(BF16)
            acc = acc + jnp.dot(a_scr[(g - ng) * pp + p].astype(BF16), vb, preferred_element_type=F32)
        acc_scr[...] = acc

    @pl.when(g == 2 * ng - 1)
    def _():
        o = acc_scr[...] + jnp.dot(anew_scr[...].astype(BF16), pad_rows(vn_ref[...]).astype(BF16),
                                   preferred_element_type=F32)
        o_ref[...] = _head_rmsnorm(o, g_ref[...], lam_init).astype(o_ref.dtype)


def _sample_attn(page_table, q, k_new, v_new, cache_k, cache_v, lam_vecs, subln_g,
                 *, lam_init, pages_per_step):
    nb = q.shape[0]
    n_pages = page_table.shape[1]
    pp = pages_per_step
    ng = n_pages // pp
    _, page, n_heads, vd = cache_k.shape
    rows = page * n_heads
    pt_flat = page_table.reshape(-1)

    head_spec = pl.BlockSpec((None, n_heads, vd), lambda b, g, pt: (b, 0, 0))
    vec = pl.BlockSpec((1, lam_vecs[0].shape[1]), lambda b, g, pt: (0, 0))

    def k_spec(p):
        return pl.BlockSpec((None, page, n_heads, vd),
                            lambda b, g, pt: (pt[b * n_pages + jnp.minimum(g, ng - 1) * pp + p], 0, 0, 0))

    def v_spec(p):
        return pl.BlockSpec((None, page, n_heads, vd),
                            lambda b, g, pt: (pt[b * n_pages + jnp.maximum(g - ng, 0) * pp + p], 0, 0, 0))

    grid_spec = pltpu.PrefetchScalarGridSpec(
        num_scalar_prefetch=1,
        grid=(nb, 2 * ng),
        in_specs=[head_spec, head_spec, head_spec, vec, vec, vec, vec,
                  pl.BlockSpec((1, vd), lambda b, g, pt: (0, 0))]
                 + [k_spec(p) for p in range(pp)] + [v_spec(p) for p in range(pp)],
        out_specs=head_spec,
        scratch_shapes=[
            pltpu.VMEM((n_pages, 2 * n_heads, rows), F32),
            pltpu.VMEM((n_pages, n_heads, rows), F32),
            pltpu.VMEM((2 * n_heads, vd), BF16),
            pltpu.VMEM((n_heads, LANES), F32),
            pltpu.VMEM((n_heads, vd), F32),
        ],
    )
    return pl.pallas_call(
        functools.partial(_sample_attn_kernel, lam_init=lam_init, pages_per_step=pp, n_pages=n_pages),
        out_shape=jax.ShapeDtypeStruct((nb, n_heads, vd), BF16),
        grid_spec=grid_spec,
        compiler_params=pltpu.CompilerParams(
            dimension_semantics=("parallel", "arbitrary"), vmem_limit_bytes=VMEM_LIMIT),
        name="sample_attn",
    )(pt_flat, q, k_new, v_new, *lam_vecs, subln_g, *([cache_k] * pp), *([cache_v] * pp))


def _mixer_tail(o, cb, conv, gate_refs, x, wa_ref, wc_ref, wo_ref, g_ref, b_ref, alpha):
    ga0_ref, ga1_ref, gc0_ref, gc1_ref = gate_refs
    ga = jnp.concatenate([ga0_ref[...], ga1_ref[...]], axis=1)
    gc = jnp.concatenate([gc0_ref[...], gc1_ref[...]], axis=1)
    y_attn = jnp.dot(o, wa_ref[...], preferred_element_type=F32)
    y_conv = jnp.dot((cb * conv).astype(BF16), wc_ref[...], preferred_element_type=F32)
    merged = jax.nn.sigmoid(ga) * y_attn + jax.nn.sigmoid(gc) * y_conv
    z = jnp.dot(merged.astype(BF16), wo_ref[...], preferred_element_type=F32)
    return _layer_norm(alpha * x + z, g_ref[...], b_ref[...])


def _merge_prompt_kernel(o_ref, cb_ref, cc_ref, ch_ref, pc_ref, ph_ref,
                         ga0_ref, ga1_ref, gc0_ref, gc1_ref, x_ref,
                         cw_ref, wa_ref, wc_ref, wo_ref, g_ref, b_ref,
                         out_ref, tail_ref, u_scr, *, alpha, tiles_per_seq):
    i = pl.program_id(0)
    tm = cc_ref.shape[0]
    u = cc_ref[...] * ch_ref[...]
    prev = pc_ref[...] * ph_ref[...]
    prev = jnp.where(i % tiles_per_seq == 0, jnp.zeros_like(prev), prev)
    u_scr[0:SUBLANES, :] = prev
    u_scr[SUBLANES:SUBLANES + tm, :] = u
    u1 = u_scr[SUBLANES - 1:SUBLANES - 1 + tm, :]
    u2 = u_scr[SUBLANES - 2:SUBLANES - 2 + tm, :]
    conv = cw_ref[0:1, :] * u2 + cw_ref[1:2, :] * u1 + cw_ref[2:3, :] * u
    tail_ref[...] = u[tm - SUBLANES:tm, :]
    out_ref[...] = _mixer_tail(o_ref[...], cb_ref[...], conv, (ga0_ref, ga1_ref, gc0_ref, gc1_ref),
                               x_ref[...], wa_ref, wc_ref, wo_ref, g_ref, b_ref, alpha)


def _merge_sample_kernel(o_ref, cb_ref, cc_ref, ch_ref, s0_ref, s1_ref,
                         ga0_ref, ga1_ref, gc0_ref, gc1_ref, x_ref,
                         cw_ref, wa_ref, wc_ref, wo_ref, g_ref, b_ref,
                         out_ref, u_ref, *, alpha):
    u = cc_ref[...] * ch_ref[...]
    conv = cw_ref[0:1, :] * s0_ref[...] + cw_ref[1:2, :] * s1_ref[...] + cw_ref[2:3, :] * u
    u_ref[...] = u
    out_ref[...] = _mixer_tail(o_ref[...], cb_ref[...], conv, (ga0_ref, ga1_ref, gc0_ref, gc1_ref),
                               x_ref[...], wa_ref, wc_ref, wo_ref, g_ref, b_ref, alpha)


def _const_spec(shape):
    return pl.BlockSpec(shape, lambda i: (0,) * len(shape), pipeline_mode=pl.Buffered(1))


def _merge_prompt(o, rest, x, conv_w, wa, wc, wo, g, b, *, alpha, tm, seq):
    m, d = x.shape
    dc = o.shape[1]
    rows8 = tm // SUBLANES
    in_specs = [
        pl.BlockSpec((tm, dc), lambda i: (i, 0)),
        pl.BlockSpec((tm, dc), lambda i: (i, 0)),
        pl.BlockSpec((tm, dc), lambda i: (i, 1)),
        pl.BlockSpec((tm, dc), lambda i: (i, 2)),
        pl.BlockSpec((SUBLANES, dc), lambda i: (jnp.maximum(i * rows8 - 1, 0), 1)),
        pl.BlockSpec((SUBLANES, dc), lambda i: (jnp.maximum(i * rows8 - 1, 0), 2)),
        pl.BlockSpec((tm, dc), lambda i: (i, 3)),
        pl.BlockSpec((tm, dc), lambda i: (i, 4)),
        pl.BlockSpec((tm, dc), lambda i: (i, 5)),
        pl.BlockSpec((tm, dc), lambda i: (i, 6)),
        pl.BlockSpec((tm, d), lambda i: (i, 0)),
        _const_spec(conv_w.shape), _const_spec(wa.shape), _const_spec(wc.shape), _const_spec(wo.shape),
        _const_spec(g.shape), _const_spec(b.shape),
    ]
    return pl.pallas_call(
        functools.partial(_merge_prompt_kernel, alpha=alpha, tiles_per_seq=seq // tm),
        out_shape=(jax.ShapeDtypeStruct((m, d), F32),
                   jax.ShapeDtypeStruct((m // tm * SUBLANES, dc), F32)),
        grid=(m // tm,),
        in_specs=in_specs,
        out_specs=(pl.BlockSpec((tm, d), lambda i: (i, 0)),
                   pl.BlockSpec((SUBLANES, dc), lambda i: (i, 0))),
        scratch_shapes=[pltpu.VMEM((tm + SUBLANES, dc), F32)],
        compiler_params=pltpu.CompilerParams(
            dimension_semantics=("parallel",), vmem_limit_bytes=VMEM_LIMIT),
        name="merge_prompt",
    )(o, rest, rest, rest, rest, rest, rest, rest, rest, rest, x, conv_w, wa, wc, wo, g, b)


def _merge_sample(o, rest, s0, s1, x, conv_w, wa, wc, wo, g, b, *, alpha):
    m, d = x.shape
    dc = o.shape[1]
    in_specs = [
        pl.BlockSpec((m, dc), lambda i: (0, 0)),
        pl.BlockSpec((m, dc), lambda i: (0, 0)),
        pl.BlockSpec((m, dc), lambda i: (0, 1)),
        pl.BlockSpec((m, dc), lambda i: (0, 2)),
        pl.BlockSpec((m, dc), lambda i: (0, 0)),
        pl.BlockSpec((m, dc), lambda i: (0, 0)),
        pl.BlockSpec((m, dc), lambda i: (0, 3)),
        pl.BlockSpec((m, dc), lambda i: (0, 4)),
        pl.BlockSpec((m, dc), lambda i: (0, 5)),
        pl.BlockSpec((m, dc), lambda i: (0, 6)),
        pl.BlockSpec((m, d), lambda i: (0, 0)),
        _const_spec(conv_w.shape), _const_spec(wa.shape), _const_spec(wc.shape), _const_spec(wo.shape),
        _const_spec(g.shape), _const_spec(b.shape),
    ]
    return pl.pallas_call(
        functools.partial(_merge_sample_kernel, alpha=alpha),
        out_shape=(jax.ShapeDtypeStruct((m, d), F32), jax.ShapeDtypeStruct((m, dc), F32)),
        grid=(1,),
        in_specs=in_specs,
        out_specs=(pl.BlockSpec((m, d), lambda i: (0, 0)), pl.BlockSpec((m, dc), lambda i: (0, 0))),
        compiler_params=pltpu.CompilerParams(
            dimension_semantics=("arbitrary",), vmem_limit_bytes=VMEM_LIMIT),
        name="merge_sample",
    )(o, rest, rest, rest, s0, s1, rest, rest, rest, rest, x, conv_w, wa, wc, wo, g, b)


def _rope_tables(pos, head_dim, rot_dim):
    half = rot_dim // 2
    inv_freq = jnp.power(ROPE_THETA, -jnp.arange(0, rot_dim, 2, dtype=F32) / rot_dim)
    ang = pos.astype(F32)[:, None] * inv_freq[None, :]
    cos, sin = jnp.cos(ang), jnp.sin(ang)
    n = pos.shape[0]
    ones = jnp.ones((n, head_dim - rot_dim), F32)
    zeros_h = jnp.zeros((n, half), F32)
    zeros_r = jnp.zeros((n, head_dim - rot_dim), F32)
    c = jnp.concatenate([cos, cos, ones], axis=1)
    lo = jnp.concatenate([-sin, zeros_h, zeros_r], axis=1)
    hi = jnp.concatenate([zeros_h, sin, zeros_r], axis=1)
    two = lambda t: jnp.concatenate([t, t], axis=1)
    return two(c), two(lo), two(hi)


def kernel(x_prompt, x_sample, cache_k, cache_v, state_conv, page_table, ln_g, ln_b, ffn1_w_gate, ffn1_w_up, ffn1_w_down, w_in, conv_w, lambda_q1, lambda_k1, lambda_q2, lambda_k2, subln_g, w_attn_out, w_conv_out, w_o, ffn2_w_gate, ffn2_w_up, ffn2_w_down):
    batch, seq, d_model = x_prompt.shape
    dec_batch, dec_seq, _ = x_sample.shape
    assert dec_seq == 1
    depth = ln_g.shape[0]
    page_size, n_heads, v_dim = cache_k.shape[2:]
    head_dim = v_dim // 2
    rot_dim = head_dim // 4
    width = n_heads * v_dim
    d_conv = conv_w.shape[2]
    assert width == d_conv and v_dim == LANES and n_heads == SUBLANES
    past_len = page_table.shape[1] * page_size
    alpha = (2.0 * depth) ** 0.25
    q_scale = head_dim ** -0.5

    tabs_p = _rope_tables(jnp.arange(seq, dtype=jnp.int32), head_dim, rot_dim)
    tabs_s = _rope_tables(jnp.full((dec_batch,), past_len, jnp.int32), head_dim, rot_dim)

    y_p = x_prompt.reshape(batch * seq, d_model)
    y_s = x_sample.reshape(dec_batch, d_model)
    outs = [[] for _ in range(6)]
    for layer in range(depth):
        lam_init = 0.8 - 0.6 * math.exp(-0.3 * layer)
        bf = lambda w: w[layer].astype(BF16)
        wg1, wu1, wd1 = bf(ffn1_w_gate), bf(ffn1_w_up), bf(ffn1_w_down)
        wg2, wu2, wd2 = bf(ffn2_w_gate), bf(ffn2_w_up), bf(ffn2_w_down)
        win, wa, wc, wo = bf(w_in), bf(w_attn_out), bf(w_conv_out), bf(w_o)
        g = [ln_g[layer, i][None, :] for i in range(3)]
        b = [ln_b[layer, i][None, :] for i in range(3)]
        lam_vecs = [v[layer][None, :] for v in (lambda_q1, lambda_k1, lambda_q2, lambda_k2)]
        sg = subln_g[layer][None, :]
        cw = conv_w[layer]

        x1 = _ffn_ln(y_p, wg1, wu1, wd1, g[0], b[0], alpha=alpha, tm=512, tf=512)
        q, k, v, kb, vb, rest = _inproj(x1, win, *tabs_p, tm=512, tn=width, q_scale=q_scale,
                                        rot_half=rot_dim // 2, n_heads=n_heads)
        o = _prompt_attn(q, kb, vb, lam_vecs, sg, batch=batch, seq=seq, n_heads=n_heads,
                         lam_init=lam_init, tq=256)
        x2, tails = _merge_prompt(o, rest, x1, cw, wa, wc, wo, g[1], b[1], alpha=alpha, tm=256, seq=seq)
        y_p = _ffn_ln(x2, wg2, wu2, wd2, g[2], b[2], alpha=alpha, tm=512, tf=512)
        tails = tails.reshape(batch, seq // 256, SUBLANES, d_conv)
        outs[0].append(k.reshape(batch, seq, n_heads, v_dim))
        outs[1].append(v.reshape(batch, seq, n_heads, v_dim))
        outs[2].append(tails[:, -1, SUBLANES - 2:, :])

        s1 = _ffn_ln(y_s, wg1, wu1, wd1, g[0], b[0], alpha=alpha, tm=dec_batch, tf=512)
        qs, ks, vs, _, _, rest_s = _inproj(s1, win, *tabs_s, tm=dec_batch, tn=width, q_scale=q_scale,
                                           rot_half=rot_dim // 2, n_heads=n_heads)
        heads = lambda t: t.reshape(dec_batch, n_heads, v_dim)
        os_ = _sample_attn(page_table, heads(qs), heads(ks), heads(vs), cache_k[layer], cache_v[layer],
                           lam_vecs, sg, lam_init=lam_init, pages_per_step=16)
        st = state_conv[layer]
        s2, u_s = _merge_sample(os_.reshape(dec_batch, width), rest_s, st[:, 0], st[:, 1], s1,
                                cw, wa, wc, wo, g[1], b[1], alpha=alpha)
        y_s = _ffn_ln(s2, wg2, wu2, wd2, g[2], b[2], alpha=alpha, tm=dec_batch, tf=512)
        outs[3].append(ks.reshape(dec_batch, 1, n_heads, v_dim))
        outs[4].append(vs.reshape(dec_batch, 1, n_heads, v_dim))
        outs[5].append(jnp.stack([st[:, 1], u_s], axis=1))

    return (y_p.reshape(batch, seq, d_model), y_s.reshape(dec_batch, 1, d_model),
            jnp.stack(outs[0]), jnp.stack(outs[1]), jnp.stack(outs[2]),
            jnp.stack(outs[3]), jnp.stack(outs[4]), jnp.stack(outs[5]))
```

```python
import functools
import math

import jax
import jax.numpy as jnp
from jax import lax
from jax.experimental import pallas as pl
from jax.experimental.pallas import tpu as pltpu

ROPE_THETA = 500000.0
LN_EPS = 1e-5
NEG_INF = -1e30
LANES = 128
SUBLANES = 8
VMEM_LIMIT = 56 * 1024 * 1024
LN_ROWS = 256
BF16 = jnp.bfloat16
F32 = jnp.float32


def _layer_norm(y, g, b):
    mu = jnp.mean(y, axis=-1, keepdims=True)
    d = y - mu
    var = jnp.mean(d * d, axis=-1, keepdims=True)
    return d * lax.rsqrt(var + LN_EPS) * g + b


def _ffn_ln_kernel(x_ref, wg_ref, wu_ref, wd_ref, g_ref, b_ref, o_ref, xb_ref, *, alpha):
    f = pl.program_id(1)
    nf = pl.num_programs(1)

    @pl.when(f == 0)
    def _():
        xb_ref[...] = x_ref[...].astype(BF16)
        o_ref[...] = jnp.zeros(o_ref.shape, F32)

    xb = xb_ref[...]
    hg = jnp.dot(xb, wg_ref[...].astype(BF16), preferred_element_type=F32)
    hu = jnp.dot(xb, wu_ref[...].astype(BF16), preferred_element_type=F32)
    h = (hg * jax.nn.sigmoid(hg) * hu).astype(BF16)
    o_ref[...] += jnp.dot(h, wd_ref[...].astype(BF16), preferred_element_type=F32)

    @pl.when(f == nf - 1)
    def _():
        rows = min(x_ref.shape[0], LN_ROWS)
        for r in range(x_ref.shape[0] // rows):
            sl = slice(r * rows, (r + 1) * rows)
            y = alpha * x_ref[sl, :] + 0.5 * o_ref[sl, :]
            o_ref[sl, :] = _layer_norm(y, g_ref[...], b_ref[...])


def _ffn_ln(x, wg, wu, wd, g, b, *, alpha, tm, tf):
    m, d = x.shape
    dff = wg.shape[1]
    return pl.pallas_call(
        functools.partial(_ffn_ln_kernel, alpha=alpha),
        out_shape=jax.ShapeDtypeStruct((m, d), F32),
        grid=(m // tm, dff // tf),
        in_specs=[
            pl.BlockSpec((tm, d), lambda i, f: (i, 0)),
            pl.BlockSpec((d, tf), lambda i, f: (0, f)),
            pl.BlockSpec((d, tf), lambda i, f: (0, f)),
            pl.BlockSpec((tf, d), lambda i, f: (f, 0)),
            pl.BlockSpec((1, d), lambda i, f: (0, 0)),
            pl.BlockSpec((1, d), lambda i, f: (0, 0)),
        ],
        out_specs=pl.BlockSpec((tm, d), lambda i, f: (i, 0)),
        scratch_shapes=[pltpu.VMEM((tm, d), BF16)],
        compiler_params=pltpu.CompilerParams(
            dimension_semantics=("parallel", "arbitrary"), vmem_limit_bytes=VMEM_LIMIT),
        name="ffn_ln",
    )(x, wg, wu, wd, g, b)


N_QKV_TILES = 3
N_CONV_TILES = 3
def _rope_tile(z, cos, sin_lo, sin_hi, rot_half):
    up = pltpu.roll(z, LANES - rot_half, 1)
    down = pltpu.roll(z, rot_half, 1)
    return z * cos + up * sin_lo + down * sin_hi


def _inproj_kernel(x_ref, w_ref, cos_ref, slo_ref, shi_ref,
                   q_ref, k_ref, v_ref, kb_ref, vb_ref, r_ref, gate_ref, xb_ref,
                   *, q_scale, rot_half, n_heads):
    j = pl.program_id(1)

    @pl.when(j == 0)
    def _():
        xb_ref[...] = x_ref[...].astype(BF16)

    @pl.when(j < N_QKV_TILES + N_CONV_TILES)
    def _():
        r_ref[...] = jnp.dot(xb_ref[...], w_ref[...], preferred_element_type=F32)

    @pl.when(j >= N_QKV_TILES + N_CONV_TILES)
    def _():
        gate_ref[...] = jnp.dot(xb_ref[...], w_ref[...], preferred_element_type=F32).astype(BF16)

    @pl.when(j == 0)
    def _():
        for h in range(n_heads):
            sl = slice(h * LANES, (h + 1) * LANES)
            r = _rope_tile(r_ref[:, sl], cos_ref[...], slo_ref[...], shi_ref[...], rot_half)
            q_ref[:, sl] = (r * q_scale).astype(BF16)

    @pl.when(j == 1)
    def _():
        for h in range(n_heads):
            sl = slice(h * LANES, (h + 1) * LANES)
            r = _rope_tile(r_ref[:, sl], cos_ref[...], slo_ref[...], shi_ref[...], rot_half)
            k_ref[:, sl] = r
            kb_ref[:, sl] = r.astype(BF16)

    @pl.when(j == 2)
    def _():
        v_ref[...] = r_ref[...]
        vb_ref[...] = r_ref[...].astype(BF16)


def _inproj(x, w, cos, slo, shi, *, tm, tn, q_scale, rot_half, n_heads):
    m, d = x.shape
    n = w.shape[1]
    nj = n // tn
    n_first = N_QKV_TILES + N_CONV_TILES
    n_gate_tiles = nj - n_first
    tab_blocks = cos.shape[0] // tm
    tab_spec = pl.BlockSpec((tm, LANES), lambda i, j: (i % tab_blocks, 0))
    row_spec = pl.BlockSpec((tm, tn), lambda i, j: (i, 0))
    return pl.pallas_call(
        functools.partial(_inproj_kernel, q_scale=q_scale, rot_half=rot_half, n_heads=n_heads),
        out_shape=(
            jax.ShapeDtypeStruct((m, tn), BF16),
            jax.ShapeDtypeStruct((m, tn), F32),
            jax.ShapeDtypeStruct((m, tn), F32),
            jax.ShapeDtypeStruct((m, tn), BF16),
            jax.ShapeDtypeStruct((m, tn), BF16),
            jax.ShapeDtypeStruct((m, N_CONV_TILES * tn), F32),
            jax.ShapeDtypeStruct((m, n_gate_tiles * tn), BF16),
        ),
        grid=(m // tm, nj),
        in_specs=[
            pl.BlockSpec((tm, d), lambda i, j: (i, 0)),
            pl.BlockSpec((d, tn), lambda i, j: (0, j)),
            tab_spec, tab_spec, tab_spec,
        ],
        out_specs=(row_spec, row_spec, row_spec, row_spec, row_spec,
                   pl.BlockSpec((tm, tn), lambda i, j: (i, jnp.clip(j - N_QKV_TILES, 0, N_CONV_TILES - 1))),
                   pl.BlockSpec((tm, tn), lambda i, j: (i, jnp.maximum(j - n_first, 0)))),
        scratch_shapes=[pltpu.VMEM((tm, d), BF16)],
        compiler_params=pltpu.CompilerParams(
            dimension_semantics=("parallel", "arbitrary"), vmem_limit_bytes=VMEM_LIMIT),
        name="inproj",
    )(x, w, cos, slo, shi)


def _lambda_value(lq1_ref, lk1_ref, lq2_ref, lk2_ref, lam_init):
    s1 = jnp.sum(lq1_ref[...] * lk1_ref[...], axis=-1, keepdims=True)
    s2 = jnp.sum(lq2_ref[...] * lk2_ref[...], axis=-1, keepdims=True)
    return jnp.exp(s1) - jnp.exp(s2) + lam_init


def _head_rmsnorm(o, g, lam_init):
    ms = jnp.mean(o * o, axis=-1, keepdims=True)
    return o * lax.rsqrt(ms + LN_EPS) * g * (1.0 - lam_init)


def _reduce_keys(x, reduce_fn, combine_fn, groups=8):
    n, w = x.shape
    x3 = x.reshape(n // SUBLANES, SUBLANES, w)
    per = x3.shape[0] // groups
    parts = [reduce_fn(x3[g * per:(g + 1) * per], axis=0) for g in range(groups)]
    while len(parts) > 1:
        parts = [combine_fn(parts[i], parts[i + 1]) for i in range(0, len(parts), 2)]
    return reduce_fn(parts[0], axis=0, keepdims=True)


def _prompt_attn_kernel(q_ref, k_ref, v_ref, lq1_ref, lk1_ref, lq2_ref, lk2_ref, g_ref,
                        o_ref, vt_ref, *, lam_init, tq, head_dim, nq, row_chunks):
    qi = pl.program_id(2)
    rc = tq // row_chunks
    seq = k_ref.shape[0]

    @pl.when(qi == 0)
    def _():
        for c in range(seq // tq):
            cols = slice(c * tq, (c + 1) * tq)
            vt_ref[:, cols] = v_ref[cols, :].astype(F32).T.astype(BF16)

    q = q_ref[...]
    lane = lax.broadcasted_iota(jnp.int32, q.shape, 1)
    zero = jnp.zeros_like(q)
    q_sub = (jnp.where(lane < head_dim, q, zero), jnp.where(lane >= head_dim, q, zero))
    key = lax.broadcasted_iota(jnp.int32, (rc, rc), 0)
    qry = lax.broadcasted_iota(jnp.int32, (rc, rc), 1)
    keep = jnp.concatenate([key <= qry, key <= qry], axis=1)
    lam = _lambda_value(lq1_ref, lk1_ref, lq2_ref, lk2_ref, lam_init)

    def softmax_av(qc, ext):
        s = lax.dot_general(k_ref[0:ext, :], qc, (((1,), (1,)), ((), ())), preferred_element_type=F32)
        diag = jnp.where(keep, s[ext - rc:, :], NEG_INF)
        s = diag if ext == rc else jnp.concatenate([s[:ext - rc, :], diag], axis=0)
        m = _reduce_keys(s, jnp.max, jnp.maximum)
        p = jnp.exp(s - m)
        l = _reduce_keys(p, jnp.sum, jnp.add)
        return jnp.dot(vt_ref[:, 0:ext], p.astype(BF16), preferred_element_type=F32) / l

    for i in range(nq):
        @pl.when(qi == i)
        def _(i=i):
            for r in range(row_chunks):
                rows = slice(r * rc, (r + 1) * rc)
                ext = i * tq + (r + 1) * rc
                o = softmax_av(jnp.concatenate([q_sub[0][rows], q_sub[1][rows]], axis=0), ext)
                o = o[:, :rc] - lam * o[:, rc:]
                o_ref[rows, :] = _head_rmsnorm(o.T, g_ref[...], lam_init).astype(BF16)


def _prompt_attn(q, kb, vb, lam_vecs, subln_g, *, batch, seq, n_heads, lam_init, tq):
    m, width = q.shape
    nq = seq // tq
    vd = width // n_heads
    vec = pl.BlockSpec((1, lam_vecs[0].shape[1]), lambda b, h, i: (0, 0))
    return pl.pallas_call(
        functools.partial(_prompt_attn_kernel, lam_init=lam_init, tq=tq, head_dim=vd // 2, nq=nq,
                          row_chunks=2),
        out_shape=jax.ShapeDtypeStruct((m, width), BF16),
        grid=(batch, n_heads, nq),
        in_specs=[
            pl.BlockSpec((tq, vd), lambda b, h, i: (b * nq + i, h)),
            pl.BlockSpec((seq, vd), lambda b, h, i: (b, h)),
            pl.BlockSpec((seq, vd), lambda b, h, i: (b, h)),
            vec, vec, vec, vec,
            pl.BlockSpec((1, vd), lambda b, h, i: (0, 0)),
        ],
        out_specs=pl.BlockSpec((tq, vd), lambda b, h, i: (b * nq + i, h)),
        scratch_shapes=[pltpu.VMEM((vd, seq), BF16)],
        compiler_params=pltpu.CompilerParams(
            dimension_semantics=("parallel", "parallel", "arbitrary"), vmem_limit_bytes=VMEM_LIMIT),
        name="prompt_attn",
    )(q, kb, vb, *lam_vecs, subln_g)


def _class_reduce(x, op, n_classes):
    shift = n_classes
    while shift < x.shape[-1]:
        x = op(x, pltpu.roll(x, shift, 1))
        shift *= 2
    return x


def _sample_attn_kernel(pt_ref, q_ref, kn_ref, vn_ref, lq1_ref, lk1_ref, lq2_ref, lk2_ref, g_ref,
                        *rest, lam_init, pages_per_step, n_pages):
    pp = pages_per_step
    k_refs = rest[:pp]
    v_refs = rest[pp:2 * pp]
    o_ref = rest[2 * pp]
    s_scr, a_scr, qt_scr, anew_scr, acc_scr = rest[2 * pp + 1:]
    g = pl.program_id(1)
    ng = n_pages // pp
    page, n_heads, vd = k_refs[0].shape
    rows = page * n_heads
    hd = vd // 2
    nt = (((1,), (1,)), ((), ()))

    def pad_rows(x):
        return jnp.concatenate([x, jnp.zeros((LANES - n_heads, vd), x.dtype)], axis=0)

    @pl.when(g == 0)
    def _():
        q = q_ref[...]
        qq = jnp.concatenate([q, q], axis=0)
        row = lax.broadcasted_iota(jnp.int32, qq.shape, 0)
        lane = lax.broadcasted_iota(jnp.int32, qq.shape, 1)
        qt_scr[...] = jnp.where(row // n_heads == lane // hd, qq, jnp.zeros_like(qq))

    @pl.when(g < ng)
    def _():
        for p in range(pp):
            kb = k_refs[p][...].reshape(rows, vd).astype(BF16)
            s_scr[g * pp + p] = lax.dot_general(qt_scr[...], kb, nt, preferred_element_type=F32)

    @pl.when(g == ng)
    def _():
        lane1 = lax.broadcasted_iota(jnp.int32, (2 * n_heads, LANES), 1)
        s_new = lax.dot_general(qt_scr[...], pad_rows(kn_ref[...]).astype(BF16), nt,
                                preferred_element_type=F32)
        s_new = jnp.where(lane1 < n_heads, s_new, NEG_INF)

        def with_new(x, x_new, op):
            return jnp.concatenate([op(x[:, :LANES], x_new), x[:, LANES:]], axis=1)

        m = lax.fori_loop(0, n_pages, lambda i, m: jnp.maximum(m, s_scr[i]),
                          jnp.full((2 * n_heads, rows), NEG_INF, F32))
        m = _class_reduce(with_new(m, s_new, jnp.maximum), jnp.maximum, n_heads)

        def sum_body(i, l):
            p = jnp.exp(s_scr[i] - m)
            s_scr[i] = p
            return l + p
        l = lax.fori_loop(0, n_pages, sum_body, jnp.zeros((2 * n_heads, rows), F32))
        p_new = jnp.exp(s_new - m[:, :LANES])
        l = _class_reduce(with_new(l, p_new, jnp.add), jnp.add, n_heads)

        lam = _lambda_value(lq1_ref, lk1_ref, lq2_ref, lk2_ref, lam_init)
        row = lax.broadcasted_iota(jnp.int32, (n_heads, rows), 0)
        lane = lax.broadcasted_iota(jnp.int32, (n_heads, rows), 1)
        own = row == lane % n_heads

        def a_body(i, c):
            pn = s_scr[i] / l
            a_scr[i] = jnp.where(own, pn[:n_heads] - lam * pn[n_heads:], 0.0)
            return c
        lax.fori_loop(0, n_pages, a_body, 0)
        pn_new = p_new / l[:, :LANES]
        anew_scr[...] = jnp.where(own[:, :LANES], pn_new[:n_heads] - lam * pn_new[n_heads:], 0.0)
        acc_scr[...] = jnp.zeros(acc_scr.shape, F32)

    @pl.when(g >= ng)
    def _():
        acc = acc_scr[...]
        for p in range(pp):
            vb = v_refs[p][...].reshape(rows, vd).astype(BF16)
            acc = acc + jnp.dot(a_scr[(g - ng) * pp + p].astype(BF16), vb, preferred_element_type=F32)
        acc_scr[...] = acc

    @pl.when(g == 2 * ng - 1)
    def _():
        o = acc_scr[...] + jnp.dot(anew_scr[...].astype(BF16), pad_rows(vn_ref[...]).astype(BF16),
                                   preferred_element_type=F32)
        o_ref[...] = _head_rmsnorm(o, g_ref[...], lam_init).astype(o_ref.dtype)


def _sample_attn(page_table, q, k_new, v_new, cache_k, cache_v, lam_vecs, subln_g,
                 *, lam_init, pages_per_step):
    nb = q.shape[0]
    n_pages = page_table.shape[1]
    pp = pages_per_step
    ng = n_pages // pp
    _, page, n_heads, vd = cache_k.shape
    rows = page * n_heads
    pt_flat = page_table.reshape(-1)

    head_spec = pl.BlockSpec((None, n_heads, vd), lambda b, g, pt: (b, 0, 0))
    vec = pl.BlockSpec((1, lam_vecs[0].shape[1]), lambda b, g, pt: (0, 0))

    def k_spec(p):
        return pl.BlockSpec((None, page, n_heads, vd),
                            lambda b, g, pt: (pt[b * n_pages + jnp.minimum(g, ng - 1) * pp + p], 0, 0, 0))

    def v_spec(p):
        return pl.BlockSpec((None, page, n_heads, vd),
                            lambda b, g, pt: (pt[b * n_pages + jnp.maximum(g - ng, 0) * pp + p], 0, 0, 0))

    grid_spec = pltpu.PrefetchScalarGridSpec(
        num_scalar_prefetch=1,
        grid=(nb, 2 * ng),
        in_specs=[head_spec, head_spec, head_spec, vec, vec, vec, vec,
                  pl.BlockSpec((1, vd), lambda b, g, pt: (0, 0))]
                 + [k_spec(p) for p in range(pp)] + [v_spec(p) for p in range(pp)],
        out_specs=head_spec,
        scratch_shapes=[
            pltpu.VMEM((n_pages, 2 * n_heads, rows), F32),
            pltpu.VMEM((n_pages, n_heads, rows), F32),
            pltpu.VMEM((2 * n_heads, vd), BF16),
            pltpu.VMEM((n_heads, LANES), F32),
            pltpu.VMEM((n_heads, vd), F32),
        ],
    )
    return pl.pallas_call(
        functools.partial(_sample_attn_kernel, lam_init=lam_init, pages_per_step=pp, n_pages=n_pages),
        out_shape=jax.ShapeDtypeStruct((nb, n_heads, vd), BF16),
        grid_spec=grid_spec,
        compiler_params=pltpu.CompilerParams(
            dimension_semantics=("parallel", "arbitrary"), vmem_limit_bytes=VMEM_LIMIT),
        name="sample_attn",
    )(pt_flat, q, k_new, v_new, *lam_vecs, subln_g, *([cache_k] * pp), *([cache_v] * pp))


def _mixer_tail(o, cb, conv, ga, gc, x, wa_ref, wc_ref, wo_ref, g_ref, b_ref, alpha):
    ga = ga.astype(F32)
    gc = gc.astype(F32)
    y_attn = jnp.dot(o, wa_ref[...], preferred_element_type=F32)
    y_conv = jnp.dot((cb * conv).astype(BF16), wc_ref[...], preferred_element_type=F32)
    merged = jax.nn.sigmoid(ga) * y_attn + jax.nn.sigmoid(gc) * y_conv
    z = jnp.dot(merged.astype(BF16), wo_ref[...], preferred_element_type=F32)
    return _layer_norm(alpha * x + z, g_ref[...], b_ref[...])


def _merge_prompt_kernel(o_ref, cb_ref, cc_ref, ch_ref, pc_ref, ph_ref, ga_ref, gc_ref, x_ref,
                         cw_ref, wa_ref, wc_ref, wo_ref, g_ref, b_ref,
                         out_ref, tail_ref, u_scr, *, alpha, tiles_per_seq):
    i = pl.program_id(0)
    tm = cc_ref.shape[0]
    u = cc_ref[...] * ch_ref[...]
    prev = pc_ref[...] * ph_ref[...]
    prev = jnp.where(i % tiles_per_seq == 0, jnp.zeros_like(prev), prev)
    u_scr[0:SUBLANES, :] = prev
    u_scr[SUBLANES:SUBLANES + tm, :] = u
    u1 = u_scr[SUBLANES - 1:SUBLANES - 1 + tm, :]
    u2 = u_scr[SUBLANES - 2:SUBLANES - 2 + tm, :]
    conv = cw_ref[0:1, :] * u2 + cw_ref[1:2, :] * u1 + cw_ref[2:3, :] * u
    tail_ref[...] = u[tm - SUBLANES:tm, :]
    out_ref[...] = _mixer_tail(o_ref[...], cb_ref[...], conv, ga_ref[...], gc_ref[...],
                               x_ref[...], wa_ref, wc_ref, wo_ref, g_ref, b_ref, alpha)


def _merge_sample_kernel(o_ref, cb_ref, cc_ref, ch_ref, s0_ref, s1_ref, ga_ref, gc_ref, x_ref,
                         cw_ref, wa_ref, wc_ref, wo_ref, g_ref, b_ref,
                         out_ref, u_ref, *, alpha):
    u = cc_ref[...] * ch_ref[...]
    conv = cw_ref[0:1, :] * s0_ref[...] + cw_ref[1:2, :] * s1_ref[...] + cw_ref[2:3, :] * u
    u_ref[...] = u
    out_ref[...] = _mixer_tail(o_ref[...], cb_ref[...], conv, ga_ref[...], gc_ref[...],
                               x_ref[...], wa_ref, wc_ref, wo_ref, g_ref, b_ref, alpha)


def _const_spec(shape):
    return pl.BlockSpec(shape, lambda i: (0,) * len(shape), pipeline_mode=pl.Buffered(1))


def _merge_prompt(o, conv_in, gates, x, conv_w, wa, wc, wo, g, b, *, alpha, tm, seq):
    m, d = x.shape
    dc = o.shape[1]
    rows8 = tm // SUBLANES
    in_specs = [
        pl.BlockSpec((tm, dc), lambda i: (i, 0)),
        pl.BlockSpec((tm, dc), lambda i: (i, 0)),
        pl.BlockSpec((tm, dc), lambda i: (i, 1)),
        pl.BlockSpec((tm, dc), lambda i: (i, 2)),
        pl.BlockSpec((SUBLANES, dc), lambda i: (jnp.maximum(i * rows8 - 1, 0), 1)),
        pl.BlockSpec((SUBLANES, dc), lambda i: (jnp.maximum(i * rows8 - 1, 0), 2)),
        pl.BlockSpec((tm, d), lambda i: (i, 0)),
        pl.BlockSpec((tm, d), lambda i: (i, 1)),
        pl.BlockSpec((tm, d), lambda i: (i, 0)),
        _const_spec(conv_w.shape), _const_spec(wa.shape), _const_spec(wc.shape), _const_spec(wo.shape),
        _const_spec(g.shape), _const_spec(b.shape),
    ]
    return pl.pallas_call(
        functools.partial(_merge_prompt_kernel, alpha=alpha, tiles_per_seq=seq // tm),
        out_shape=(jax.ShapeDtypeStruct((m, d), F32),
                   jax.ShapeDtypeStruct((m // tm * SUBLANES, dc), F32)),
        grid=(m // tm,),
        in_specs=in_specs,
        out_specs=(pl.BlockSpec((tm, d), lambda i: (i, 0)),
                   pl.BlockSpec((SUBLANES, dc), lambda i: (i, 0))),
        scratch_shapes=[pltpu.VMEM((tm + SUBLANES, dc), F32)],
        compiler_params=pltpu.CompilerParams(
            dimension_semantics=("parallel",), vmem_limit_bytes=VMEM_LIMIT),
        name="merge_prompt",
    )(o, conv_in, conv_in, conv_in, conv_in, conv_in, gates, gates, x, conv_w, wa, wc, wo, g, b)


def _merge_sample(o, conv_in, gates, s0, s1, x, conv_w, wa, wc, wo, g, b, *, alpha):
    m, d = x.shape
    dc = o.shape[1]
    in_specs = [
        pl.BlockSpec((m, dc), lambda i: (0, 0)),
        pl.BlockSpec((m, dc), lambda i: (0, 0)),
        pl.BlockSpec((m, dc), lambda i: (0, 1)),
        pl.BlockSpec((m, dc), lambda i: (0, 2)),
        pl.BlockSpec((m, dc), lambda i: (0, 0)),
        pl.BlockSpec((m, dc), lambda i: (0, 0)),
        pl.BlockSpec((m, d), lambda i: (0, 0)),
        pl.BlockSpec((m, d), lambda i: (0, 1)),
        pl.BlockSpec((m, d), lambda i: (0, 0)),
        _const_spec(conv_w.shape), _const_spec(wa.shape), _const_spec(wc.shape), _const_spec(wo.shape),
        _const_spec(g.shape), _const_spec(b.shape),
    ]
    return pl.pallas_call(
        functools.partial(_merge_sample_kernel, alpha=alpha),
        out_shape=(jax.ShapeDtypeStruct((m, d), F32), jax.ShapeDtypeStruct((m, dc), F32)),
        grid=(1,),
        in_specs=in_specs,
        out_specs=(pl.BlockSpec((m, d), lambda i: (0, 0)), pl.BlockSpec((m, dc), lambda i: (0, 0))),
        compiler_params=pltpu.CompilerParams(
            dimension_semantics=("arbitrary",), vmem_limit_bytes=VMEM_LIMIT),
        name="merge_sample",
    )(o, conv_in, conv_in, conv_in, s0, s1, gates, gates, x, conv_w, wa, wc, wo, g, b)


def _rope_tables(pos, head_dim, rot_dim):
    half = rot_dim // 2
    inv_freq = jnp.power(ROPE_THETA, -jnp.arange(0, rot_dim, 2, dtype=F32) / rot_dim)
    ang = pos.astype(F32)[:, None] * inv_freq[None, :]
    cos, sin = jnp.cos(ang), jnp.sin(ang)
    n = pos.shape[0]
    ones = jnp.ones((n, head_dim - rot_dim), F32)
    zeros_h = jnp.zeros((n, half), F32)
    zeros_r = jnp.zeros((n, head_dim - rot_dim), F32)
    c = jnp.concatenate([cos, cos, ones], axis=1)
    lo = jnp.concatenate([-sin, zeros_h, zeros_r], axis=1)
    hi = jnp.concatenate([zeros_h, sin, zeros_r], axis=1)
    two = lambda t: jnp.concatenate([t, t], axis=1)
    return two(c), two(lo), two(hi)


def kernel(x_prompt, x_sample, cache_k, cache_v, state_conv, page_table, ln_g, ln_b, ffn1_w_gate, ffn1_w_up, ffn1_w_down, w_in, conv_w, lambda_q1, lambda_k1, lambda_q2, lambda_k2, subln_g, w_attn_out, w_conv_out, w_o, ffn2_w_gate, ffn2_w_up, ffn2_w_down):
    batch, seq, d_model = x_prompt.shape
    dec_batch, dec_seq, _ = x_sample.shape
    assert dec_seq == 1
    depth = ln_g.shape[0]
    page_size, n_heads, v_dim = cache_k.shape[2:]
    head_dim = v_dim // 2
    rot_dim = head_dim // 4
    width = n_heads * v_dim
    d_conv = conv_w.shape[2]
    assert width == d_conv and v_dim == LANES and n_heads == SUBLANES
    past_len = page_table.shape[1] * page_size
    alpha = (2.0 * depth) ** 0.25
    q_scale = head_dim ** -0.5

    tabs_p = _rope_tables(jnp.arange(seq, dtype=jnp.int32), head_dim, rot_dim)
    tabs_s = _rope_tables(jnp.full((dec_batch,), past_len, jnp.int32), head_dim, rot_dim)

    y_p = x_prompt.reshape(batch * seq, d_model)
    y_s = x_sample.reshape(dec_batch, d_model)
    outs = [[] for _ in range(6)]
    for layer in range(depth):
        lam_init = 0.8 - 0.6 * math.exp(-0.3 * layer)
        bf = lambda w: w[layer].astype(BF16)
        wg1, wu1, wd1 = ffn1_w_gate[layer], ffn1_w_up[layer], ffn1_w_down[layer]
        wg2, wu2, wd2 = ffn2_w_gate[layer], ffn2_w_up[layer], ffn2_w_down[layer]
        win, wa, wc, wo = bf(w_in), bf(w_attn_out), bf(w_conv_out), bf(w_o)
        g = [ln_g[layer, i][None, :] for i in range(3)]
        b = [ln_b[layer, i][None, :] for i in range(3)]
        lam_vecs = [v[layer][None, :] for v in (lambda_q1, lambda_k1, lambda_q2, lambda_k2)]
        sg = subln_g[layer][None, :]
        cw = conv_w[layer]

        x1 = _ffn_ln(y_p, wg1, wu1, wd1, g[0], b[0], alpha=alpha, tm=1024, tf=256)
        q, k, v, kb, vb, conv_in, gates = _inproj(x1, win, *tabs_p, tm=512, tn=width, q_scale=q_scale,
                                                  rot_half=rot_dim // 2, n_heads=n_heads)
        o = _prompt_attn(q, kb, vb, lam_vecs, sg, batch=batch, seq=seq, n_heads=n_heads,
                         lam_init=lam_init, tq=512)
        x2, tails = _merge_prompt(o, conv_in, gates, x1, cw, wa, wc, wo, g[1], b[1],
                                  alpha=alpha, tm=256, seq=seq)
        y_p = _ffn_ln(x2, wg2, wu2, wd2, g[2], b[2], alpha=alpha, tm=1024, tf=256)
        tails = tails.reshape(batch, seq // 256, SUBLANES, d_conv)
        outs[0].append(k.reshape(batch, seq, n_heads, v_dim))
        outs[1].append(v.reshape(batch, seq, n_heads, v_dim))
        outs[2].append(tails[:, -1, SUBLANES - 2:, :])

        s1 = _ffn_ln(y_s, wg1, wu1, wd1, g[0], b[0], alpha=alpha, tm=dec_batch, tf=512)
        qs, ks, vs, _, _, conv_in_s, gates_s = _inproj(s1, win, *tabs_s, tm=dec_batch, tn=width,
                                                       q_scale=q_scale, rot_half=rot_dim // 2, n_heads=n_heads)
        heads = lambda t: t.reshape(dec_batch, n_heads, v_dim)
        os_ = _sample_attn(page_table, heads(qs), heads(ks), heads(vs), cache_k[layer], cache_v[layer],
                           lam_vecs, sg, lam_init=lam_init, pages_per_step=16)
        st = state_conv[layer]
        s2, u_s = _merge_sample(os_.reshape(dec_batch, width), conv_in_s, gates_s, st[:, 0], st[:, 1], s1,
                                cw, wa, wc, wo, g[1], b[1], alpha=alpha)
        y_s = _ffn_ln(s2, wg2, wu2, wd2, g[2], b[2], alpha=alpha, tm=dec_batch, tf=512)
        outs[3].append(ks.reshape(dec_batch, 1, n_heads, v_dim))
        outs[4].append(vs.reshape(dec_batch, 1, n_heads, v_dim))
        outs[5].append(jnp.stack([st[:, 1], u_s], axis=1))

    return (y_p.reshape(batch, seq, d_model), y_s.reshape(dec_batch, 1, d_model),
            jnp.stack(outs[0]), jnp.stack(outs[1]), jnp.stack(outs[2]),
            jnp.stack(outs[3]), jnp.stack(outs[4]), jnp.stack(outs[5]))
```

```python
import functools
import math

import jax
import jax.numpy as jnp
from jax import lax
from jax.experimental import pallas as pl
from jax.experimental.pallas import tpu as pltpu

ROPE_THETA = 500000.0
LN_EPS = 1e-5
NEG_INF = -1e30
LANES = 128
SUBLANES = 8
VMEM_LIMIT = 56 * 1024 * 1024
LN_ROWS = 256
SOFTMAX_UNROLL = 4
BF16 = jnp.bfloat16
F32 = jnp.float32


def _layer_norm(y, g, b):
    mu = jnp.mean(y, axis=-1, keepdims=True)
    d = y - mu
    var = jnp.mean(d * d, axis=-1, keepdims=True)
    return d * lax.rsqrt(var + LN_EPS) * g + b


def _ffn_ln_kernel(x_ref, wg_ref, wu_ref, wd_ref, g_ref, b_ref, o_ref, xb_ref, *, alpha):
    f = pl.program_id(1)
    nf = pl.num_programs(1)

    @pl.when(f == 0)
    def _():
        xb_ref[...] = x_ref[...].astype(BF16)
        o_ref[...] = jnp.zeros(o_ref.shape, F32)

    xb = xb_ref[...]
    hg = jnp.dot(xb, wg_ref[...].astype(BF16), preferred_element_type=F32)
    hu = jnp.dot(xb, wu_ref[...].astype(BF16), preferred_element_type=F32)
    h = (hg * jax.nn.sigmoid(hg) * hu).astype(BF16)
    o_ref[...] += jnp.dot(h, wd_ref[...].astype(BF16), preferred_element_type=F32)

    @pl.when(f == nf - 1)
    def _():
        rows = min(x_ref.shape[0], LN_ROWS)
        for r in range(x_ref.shape[0] // rows):
            sl = slice(r * rows, (r + 1) * rows)
            y = alpha * x_ref[sl, :] + 0.5 * o_ref[sl, :]
            o_ref[sl, :] = _layer_norm(y, g_ref[...], b_ref[...])


def _ffn_ln(x, wg, wu, wd, g, b, *, alpha, tm, tf):
    m, d = x.shape
    dff = wg.shape[1]
    return pl.pallas_call(
        functools.partial(_ffn_ln_kernel, alpha=alpha),
        out_shape=jax.ShapeDtypeStruct((m, d), F32),
        grid=(m // tm, dff // tf),
        in_specs=[
            pl.BlockSpec((tm, d), lambda i, f: (i, 0)),
            pl.BlockSpec((d, tf), lambda i, f: (0, f)),
            pl.BlockSpec((d, tf), lambda i, f: (0, f)),
            pl.BlockSpec((tf, d), lambda i, f: (f, 0)),
            pl.BlockSpec((1, d), lambda i, f: (0, 0)),
            pl.BlockSpec((1, d), lambda i, f: (0, 0)),
        ],
        out_specs=pl.BlockSpec((tm, d), lambda i, f: (i, 0)),
        scratch_shapes=[pltpu.VMEM((tm, d), BF16)],
        compiler_params=pltpu.CompilerParams(
            dimension_semantics=("parallel", "arbitrary"), vmem_limit_bytes=VMEM_LIMIT),
        name="ffn_ln",
    )(x, wg, wu, wd, g, b)


N_QKV_TILES = 3
N_CONV_TILES = 3
def _rope_tile(z, cos, sin_lo, sin_hi, rot_half):
    up = pltpu.roll(z, LANES - rot_half, 1)
    down = pltpu.roll(z, rot_half, 1)
    return z * cos + up * sin_lo + down * sin_hi


def _inproj_kernel(x_ref, w_ref, cos_ref, slo_ref, shi_ref,
                   q_ref, k_ref, v_ref, kb_ref, vb_ref, r_ref, gate_ref, xb_ref,
                   *, q_scale, rot_half, n_heads):
    j = pl.program_id(1)

    @pl.when(j == 0)
    def _():
        xb_ref[...] = x_ref[...].astype(BF16)

    @pl.when(j < N_QKV_TILES + N_CONV_TILES)
    def _():
        r_ref[...] = jnp.dot(xb_ref[...], w_ref[...], preferred_element_type=F32)

    @pl.when(j >= N_QKV_TILES + N_CONV_TILES)
    def _():
        gate_ref[...] = jnp.dot(xb_ref[...], w_ref[...], preferred_element_type=F32).astype(BF16)

    @pl.when(j == 0)
    def _():
        for h in range(n_heads):
            sl = slice(h * LANES, (h + 1) * LANES)
            r = _rope_tile(r_ref[:, sl], cos_ref[...], slo_ref[...], shi_ref[...], rot_half)
            q_ref[:, sl] = (r * q_scale).astype(BF16)

    @pl.when(j == 1)
    def _():
        for h in range(n_heads):
            sl = slice(h * LANES, (h + 1) * LANES)
            r = _rope_tile(r_ref[:, sl], cos_ref[...], slo_ref[...], shi_ref[...], rot_half)
            k_ref[:, sl] = r
            kb_ref[:, sl] = r.astype(BF16)

    @pl.when(j == 2)
    def _():
        v_ref[...] = r_ref[...]
        vb_ref[...] = r_ref[...].astype(BF16)


def _inproj(x, w, cos, slo, shi, *, tm, tn, q_scale, rot_half, n_heads):
    m, d = x.shape
    n = w.shape[1]
    nj = n // tn
    n_first = N_QKV_TILES + N_CONV_TILES
    n_gate_tiles = nj - n_first
    tab_blocks = cos.shape[0] // tm
    tab_spec = pl.BlockSpec((tm, LANES), lambda i, j: (i % tab_blocks, 0))
    row_spec = pl.BlockSpec((tm, tn), lambda i, j: (i, 0))
    return pl.pallas_call(
        functools.partial(_inproj_kernel, q_scale=q_scale, rot_half=rot_half, n_heads=n_heads),
        out_shape=(
            jax.ShapeDtypeStruct((m, tn), BF16),
            jax.ShapeDtypeStruct((m, tn), F32),
            jax.ShapeDtypeStruct((m, tn), F32),
            jax.ShapeDtypeStruct((m, tn), BF16),
            jax.ShapeDtypeStruct((m, tn), BF16),
            jax.ShapeDtypeStruct((m, N_CONV_TILES * tn), F32),
            jax.ShapeDtypeStruct((m, n_gate_tiles * tn), BF16),
        ),
        grid=(m // tm, nj),
        in_specs=[
            pl.BlockSpec((tm, d), lambda i, j: (i, 0)),
            pl.BlockSpec((d, tn), lambda i, j: (0, j)),
            tab_spec, tab_spec, tab_spec,
        ],
        out_specs=(row_spec, row_spec, row_spec, row_spec, row_spec,
                   pl.BlockSpec((tm, tn), lambda i, j: (i, jnp.clip(j - N_QKV_TILES, 0, N_CONV_TILES - 1))),
                   pl.BlockSpec((tm, tn), lambda i, j: (i, jnp.maximum(j - n_first, 0)))),
        scratch_shapes=[pltpu.VMEM((tm, d), BF16)],
        compiler_params=pltpu.CompilerParams(
            dimension_semantics=("parallel", "arbitrary"), vmem_limit_bytes=VMEM_LIMIT),
        name="inproj",
    )(x, w, cos, slo, shi)


def _lambda_value(lq1_ref, lk1_ref, lq2_ref, lk2_ref, lam_init):
    s1 = jnp.sum(lq1_ref[...] * lk1_ref[...], axis=-1, keepdims=True)
    s2 = jnp.sum(lq2_ref[...] * lk2_ref[...], axis=-1, keepdims=True)
    return jnp.exp(s1) - jnp.exp(s2) + lam_init


def _head_rmsnorm(o, g, lam_init):
    ms = jnp.mean(o * o, axis=-1, keepdims=True)
    return o * lax.rsqrt(ms + LN_EPS) * g * (1.0 - lam_init)


def _reduce_keys(x, reduce_fn, combine_fn, groups=8):
    n, w = x.shape
    x3 = x.reshape(n // SUBLANES, SUBLANES, w)
    per = x3.shape[0] // groups
    parts = [reduce_fn(x3[g * per:(g + 1) * per], axis=0) for g in range(groups)]
    while len(parts) > 1:
        parts = [combine_fn(parts[i], parts[i + 1]) for i in range(0, len(parts), 2)]
    return reduce_fn(parts[0], axis=0, keepdims=True)


def _prompt_attn_unit(qi, q_ref, k_ref, v_ref, lam, g_ref, o_ref, vt_ref,
                      *, lam_init, nq, row_chunks):
    tq, vd = q_ref.shape
    head_dim = vd // 2
    rc = tq // row_chunks
    seq = k_ref.shape[0]

    @pl.when(qi == 0)
    def _():
        for c in range(seq // tq):
            cols = slice(c * tq, (c + 1) * tq)
            vt_ref[:, cols] = v_ref[cols, :].astype(F32).T.astype(BF16)

    q = q_ref[...]
    lane = lax.broadcasted_iota(jnp.int32, q.shape, 1)
    zero = jnp.zeros_like(q)
    q_sub = (jnp.where(lane < head_dim, q, zero), jnp.where(lane >= head_dim, q, zero))
    key = lax.broadcasted_iota(jnp.int32, (rc, rc), 0)
    qry = lax.broadcasted_iota(jnp.int32, (rc, rc), 1)
    keep = jnp.concatenate([key <= qry, key <= qry], axis=1)

    def softmax_av(qc, ext):
        s = lax.dot_general(k_ref[0:ext, :], qc, (((1,), (1,)), ((), ())), preferred_element_type=F32)
        diag = jnp.where(keep, s[ext - rc:, :], NEG_INF)
        s = diag if ext == rc else jnp.concatenate([s[:ext - rc, :], diag], axis=0)
        m = _reduce_keys(s, jnp.max, jnp.maximum)
        p = jnp.exp(s - m)
        l = _reduce_keys(p, jnp.sum, jnp.add)
        return jnp.dot(vt_ref[:, 0:ext], p.astype(BF16), preferred_element_type=F32) / l

    for i in range(nq):
        @pl.when(qi == i)
        def _(i=i):
            for r in range(row_chunks):
                rows = slice(r * rc, (r + 1) * rc)
                ext = i * tq + (r + 1) * rc
                o = softmax_av(jnp.concatenate([q_sub[0][rows], q_sub[1][rows]], axis=0), ext)
                o = o[:, :rc] - lam * o[:, rc:]
                o_ref[rows, :] = _head_rmsnorm(o.T, g_ref[...], lam_init).astype(BF16)


def _class_reduce(x, op, n_classes):
    shift = n_classes
    while shift < x.shape[-1]:
        x = op(x, pltpu.roll(x, shift, 1))
        shift *= 2
    return x


def _attn_kernel(pt_ref, q_ref, kn_ref, vn_ref, lq1_ref, lk1_ref, lq2_ref, lk2_ref, g_ref,
                 qp_ref, kp_ref, vp_ref, *rest,
                 lam_init, pages_per_step, n_pages, steps_per_unit, nq, row_chunks):
    pp = pages_per_step
    k_refs = rest[:pp]
    v_refs = rest[pp:2 * pp]
    o_ref, op_ref = rest[2 * pp:2 * pp + 2]
    s_scr, a_scr, qt_scr, anew_scr, acc_scr, vt_scr = rest[2 * pp + 2:]
    g = pl.program_id(1)
    ng = n_pages // pp
    step = pl.program_id(0) * (2 * ng) + g

    @pl.when(step % steps_per_unit == 0)
    def _():
        lam = _lambda_value(lq1_ref, lk1_ref, lq2_ref, lk2_ref, lam_init)
        _prompt_attn_unit((step // steps_per_unit) % nq, qp_ref, kp_ref, vp_ref, lam, g_ref, op_ref, vt_scr,
                          lam_init=lam_init, nq=nq, row_chunks=row_chunks)

    page, n_heads, vd = k_refs[0].shape
    rows = page * n_heads
    hd = vd // 2
    nt = (((1,), (1,)), ((), ()))

    def pad_rows(x):
        return jnp.concatenate([x, jnp.zeros((LANES - n_heads, vd), x.dtype)], axis=0)

    @pl.when(g == 0)
    def _():
        q = q_ref[...]
        qq = jnp.concatenate([q, q], axis=0)
        row = lax.broadcasted_iota(jnp.int32, qq.shape, 0)
        lane = lax.broadcasted_iota(jnp.int32, qq.shape, 1)
        qt_scr[...] = jnp.where(row // n_heads == lane // hd, qq, jnp.zeros_like(qq))

    @pl.when(g < ng)
    def _():
        for p in range(pp):
            kb = k_refs[p][...].reshape(rows, vd).astype(BF16)
            s_scr[g * pp + p] = lax.dot_general(qt_scr[...], kb, nt, preferred_element_type=F32)

    @pl.when(g == ng)
    def _():
        lane1 = lax.broadcasted_iota(jnp.int32, (2 * n_heads, LANES), 1)
        s_new = lax.dot_general(qt_scr[...], pad_rows(kn_ref[...]).astype(BF16), nt,
                                preferred_element_type=F32)
        s_new = jnp.where(lane1 < n_heads, s_new, NEG_INF)

        def with_new(x, x_new, op):
            return jnp.concatenate([op(x[:, :LANES], x_new), x[:, LANES:]], axis=1)

        m = lax.fori_loop(0, n_pages, lambda i, m: jnp.maximum(m, s_scr[i]),
                          jnp.full((2 * n_heads, rows), NEG_INF, F32), unroll=SOFTMAX_UNROLL)
        m = _class_reduce(with_new(m, s_new, jnp.maximum), jnp.maximum, n_heads)

        def sum_body(i, l):
            p = jnp.exp(s_scr[i] - m)
            s_scr[i] = p
            return l + p
        l = lax.fori_loop(0, n_pages, sum_body, jnp.zeros((2 * n_heads, rows), F32), unroll=SOFTMAX_UNROLL)
        p_new = jnp.exp(s_new - m[:, :LANES])
        l = _class_reduce(with_new(l, p_new, jnp.add), jnp.add, n_heads)

        lam = _lambda_value(lq1_ref, lk1_ref, lq2_ref, lk2_ref, lam_init)
        row = lax.broadcasted_iota(jnp.int32, (n_heads, rows), 0)
        lane = lax.broadcasted_iota(jnp.int32, (n_heads, rows), 1)
        own = row == lane % n_heads

        def a_body(i, c):
            pn = s_scr[i] / l
            a_scr[i] = jnp.where(own, pn[:n_heads] - lam * pn[n_heads:], 0.0)
            return c
        lax.fori_loop(0, n_pages, a_body, 0, unroll=SOFTMAX_UNROLL)
        pn_new = p_new / l[:, :LANES]
        anew_scr[...] = jnp.where(own[:, :LANES], pn_new[:n_heads] - lam * pn_new[n_heads:], 0.0)
        acc_scr[...] = jnp.zeros(acc_scr.shape, F32)

    @pl.when(g >= ng)
    def _():
        acc = acc_scr[...]
        for p in range(pp):
            vb = v_refs[p][...].reshape(rows, vd).astype(BF16)
            acc = acc + jnp.dot(a_scr[(g - ng) * pp + p].astype(BF16), vb, preferred_element_type=F32)
        acc_scr[...] = acc

    @pl.when(g == 2 * ng - 1)
    def _():
        o = acc_scr[...] + jnp.dot(anew_scr[...].astype(BF16), pad_rows(vn_ref[...]).astype(BF16),
                                   preferred_element_type=F32)
        o_ref[...] = _head_rmsnorm(o, g_ref[...], lam_init).astype(o_ref.dtype)


def _attention(page_table, q, k_new, v_new, cache_k, cache_v, qp, kp, vp, lam_vecs, subln_g,
               *, batch, seq, lam_init, pages_per_step, tq, row_chunks):
    nb = q.shape[0]
    n_pages = page_table.shape[1]
    pp = pages_per_step
    ng = n_pages // pp
    _, page, n_heads, vd = cache_k.shape
    rows = page * n_heads
    pt_flat = page_table.reshape(-1)
    nq = seq // tq
    n_units = batch * n_heads * nq
    n_steps = nb * 2 * ng
    steps_per_unit = n_steps // n_units
    assert steps_per_unit * n_units == n_steps

    def unit(b, g):
        u = (b * (2 * ng) + g) // steps_per_unit
        return u // (n_heads * nq), (u // nq) % n_heads, u % nq

    def qp_map(b, g, pt):
        pb, h, i = unit(b, g)
        return pb * nq + i, h

    def kvp_map(b, g, pt):
        pb, h, _ = unit(b, g)
        return pb, h

    head_spec = pl.BlockSpec((None, n_heads, vd), lambda b, g, pt: (b, 0, 0))
    vec = pl.BlockSpec((1, lam_vecs[0].shape[1]), lambda b, g, pt: (0, 0))

    def k_spec(p):
        return pl.BlockSpec((None, page, n_heads, vd),
                            lambda b, g, pt: (pt[b * n_pages + jnp.minimum(g, ng - 1) * pp + p], 0, 0, 0))

    def v_spec(p):
        return pl.BlockSpec((None, page, n_heads, vd),
                            lambda b, g, pt: (pt[b * n_pages + jnp.maximum(g - ng, 0) * pp + p], 0, 0, 0))

    grid_spec = pltpu.PrefetchScalarGridSpec(
        num_scalar_prefetch=1,
        grid=(nb, 2 * ng),
        in_specs=[head_spec, head_spec, head_spec, vec, vec, vec, vec,
                  pl.BlockSpec((1, vd), lambda b, g, pt: (0, 0)),
                  pl.BlockSpec((tq, vd), qp_map),
                  pl.BlockSpec((seq, vd), kvp_map),
                  pl.BlockSpec((seq, vd), kvp_map)]
                 + [k_spec(p) for p in range(pp)] + [v_spec(p) for p in range(pp)],
        out_specs=(head_spec, pl.BlockSpec((tq, vd), qp_map)),
        scratch_shapes=[
            pltpu.VMEM((n_pages, 2 * n_heads, rows), F32),
            pltpu.VMEM((n_pages, n_heads, rows), F32),
            pltpu.VMEM((2 * n_heads, vd), BF16),
            pltpu.VMEM((n_heads, LANES), F32),
            pltpu.VMEM((n_heads, vd), F32),
            pltpu.VMEM((vd, seq), BF16),
        ],
    )
    return pl.pallas_call(
        functools.partial(_attn_kernel, lam_init=lam_init, pages_per_step=pp, n_pages=n_pages,
                          steps_per_unit=steps_per_unit, nq=nq, row_chunks=row_chunks),
        out_shape=(jax.ShapeDtypeStruct((nb, n_heads, vd), BF16),
                   jax.ShapeDtypeStruct(qp.shape, BF16)),
        grid_spec=grid_spec,
        compiler_params=pltpu.CompilerParams(
            dimension_semantics=("arbitrary", "arbitrary"), vmem_limit_bytes=VMEM_LIMIT),
        name="attention",
    )(pt_flat, q, k_new, v_new, *lam_vecs, subln_g, qp, kp, vp, *([cache_k] * pp), *([cache_v] * pp))


def _mixer_tail(o, cb, conv, ga, gc, x, wa_ref, wc_ref, wo_ref, g_ref, b_ref, alpha):
    ga = ga.astype(F32)
    gc = gc.astype(F32)
    y_attn = jnp.dot(o, wa_ref[...], preferred_element_type=F32)
    y_conv = jnp.dot((cb * conv).astype(BF16), wc_ref[...], preferred_element_type=F32)
    merged = jax.nn.sigmoid(ga) * y_attn + jax.nn.sigmoid(gc) * y_conv
    z = jnp.dot(merged.astype(BF16), wo_ref[...], preferred_element_type=F32)
    return _layer_norm(alpha * x + z, g_ref[...], b_ref[...])


def _merge_prompt_kernel(o_ref, cb_ref, cc_ref, ch_ref, pc_ref, ph_ref, ga_ref, gc_ref, x_ref,
                         cw_ref, wa_ref, wc_ref, wo_ref, g_ref, b_ref,
                         out_ref, tail_ref, u_scr, *, alpha, tiles_per_seq):
    i = pl.program_id(0)
    tm = cc_ref.shape[0]
    u = cc_ref[...] * ch_ref[...]
    prev = pc_ref[...] * ph_ref[...]
    prev = jnp.where(i % tiles_per_seq == 0, jnp.zeros_like(prev), prev)
    u_scr[0:SUBLANES, :] = prev
    u_scr[SUBLANES:SUBLANES + tm, :] = u
    u1 = u_scr[SUBLANES - 1:SUBLANES - 1 + tm, :]
    u2 = u_scr[SUBLANES - 2:SUBLANES - 2 + tm, :]
    conv = cw_ref[0:1, :] * u2 + cw_ref[1:2, :] * u1 + cw_ref[2:3, :] * u
    tail_ref[...] = u[tm - SUBLANES:tm, :]
    out_ref[...] = _mixer_tail(o_ref[...], cb_ref[...], conv, ga_ref[...], gc_ref[...],
                               x_ref[...], wa_ref, wc_ref, wo_ref, g_ref, b_ref, alpha)


def _merge_sample_kernel(o_ref, cb_ref, cc_ref, ch_ref, s0_ref, s1_ref, ga_ref, gc_ref, x_ref,
                         cw_ref, wa_ref, wc_ref, wo_ref, g_ref, b_ref,
                         out_ref, u_ref, *, alpha):
    u = cc_ref[...] * ch_ref[...]
    conv = cw_ref[0:1, :] * s0_ref[...] + cw_ref[1:2, :] * s1_ref[...] + cw_ref[2:3, :] * u
    u_ref[...] = u
    out_ref[...] = _mixer_tail(o_ref[...], cb_ref[...], conv, ga_ref[...], gc_ref[...],
                               x_ref[...], wa_ref, wc_ref, wo_ref, g_ref, b_ref, alpha)


def _const_spec(shape):
    return pl.BlockSpec(shape, lambda i: (0,) * len(shape), pipeline_mode=pl.Buffered(1))


def _merge_prompt(o, conv_in, gates, x, conv_w, wa, wc, wo, g, b, *, alpha, tm, seq):
    m, d = x.shape
    dc = o.shape[1]
    rows8 = tm // SUBLANES
    in_specs = [
        pl.BlockSpec((tm, dc), lambda i: (i, 0)),
        pl.BlockSpec((tm, dc), lambda i: (i, 0)),
        pl.BlockSpec((tm, dc), lambda i: (i, 1)),
        pl.BlockSpec((tm, dc), lambda i: (i, 2)),
        pl.BlockSpec((SUBLANES, dc), lambda i: (jnp.maximum(i * rows8 - 1, 0), 1)),
        pl.BlockSpec((SUBLANES, dc), lambda i: (jnp.maximum(i * rows8 - 1, 0), 2)),
        pl.BlockSpec((tm, d), lambda i: (i, 0)),
        pl.BlockSpec((tm, d), lambda i: (i, 1)),
        pl.BlockSpec((tm, d), lambda i: (i, 0)),
        _const_spec(conv_w.shape), _const_spec(wa.shape), _const_spec(wc.shape), _const_spec(wo.shape),
        _const_spec(g.shape), _const_spec(b.shape),
    ]
    return pl.pallas_call(
        functools.partial(_merge_prompt_kernel, alpha=alpha, tiles_per_seq=seq // tm),
        out_shape=(jax.ShapeDtypeStruct((m, d), F32),
                   jax.ShapeDtypeStruct((m // tm * SUBLANES, dc), F32)),
        grid=(m // tm,),
        in_specs=in_specs,
        out_specs=(pl.BlockSpec((tm, d), lambda i: (i, 0)),
                   pl.BlockSpec((SUBLANES, dc), lambda i: (i, 0))),
        scratch_shapes=[pltpu.VMEM((tm + SUBLANES, dc), F32)],
        compiler_params=pltpu.CompilerParams(
            dimension_semantics=("parallel",), vmem_limit_bytes=VMEM_LIMIT),
        name="merge_prompt",
    )(o, conv_in, conv_in, conv_in, conv_in, conv_in, gates, gates, x, conv_w, wa, wc, wo, g, b)


def _merge_sample(o, conv_in, gates, s0, s1, x, conv_w, wa, wc, wo, g, b, *, alpha):
    m, d = x.shape
    dc = o.shape[1]
    in_specs = [
        pl.BlockSpec((m, dc), lambda i: (0, 0)),
        pl.BlockSpec((m, dc), lambda i: (0, 0)),
        pl.BlockSpec((m, dc), lambda i: (0, 1)),
        pl.BlockSpec((m, dc), lambda i: (0, 2)),
        pl.BlockSpec((m, dc), lambda i: (0, 0)),
        pl.BlockSpec((m, dc), lambda i: (0, 0)),
        pl.BlockSpec((m, d), lambda i: (0, 0)),
        pl.BlockSpec((m, d), lambda i: (0, 1)),
        pl.BlockSpec((m, d), lambda i: (0, 0)),
        _const_spec(conv_w.shape), _const_spec(wa.shape), _const_spec(wc.shape), _const_spec(wo.shape),
        _const_spec(g.shape), _const_spec(b.shape),
    ]
    return pl.pallas_call(
        functools.partial(_merge_sample_kernel, alpha=alpha),
        out_shape=(jax.ShapeDtypeStruct((m, d), F32), jax.ShapeDtypeStruct((m, dc), F32)),
        grid=(1,),
        in_specs=in_specs,
        out_specs=(pl.BlockSpec((m, d), lambda i: (0, 0)), pl.BlockSpec((m, dc), lambda i: (0, 0))),
        compiler_params=pltpu.CompilerParams(
            dimension_semantics=("arbitrary",), vmem_limit_bytes=VMEM_LIMIT),
        name="merge_sample",
    )(o, conv_in, conv_in, conv_in, s0, s1, gates, gates, x, conv_w, wa, wc, wo, g, b)


def _rope_tables(pos, head_dim, rot_dim):
    half = rot_dim // 2
    inv_freq = jnp.power(ROPE_THETA, -jnp.arange(0, rot_dim, 2, dtype=F32) / rot_dim)
    ang = pos.astype(F32)[:, None] * inv_freq[None, :]
    cos, sin = jnp.cos(ang), jnp.sin(ang)
    n = pos.shape[0]
    ones = jnp.ones((n, head_dim - rot_dim), F32)
    zeros_h = jnp.zeros((n, half), F32)
    zeros_r = jnp.zeros((n, head_dim - rot_dim), F32)
    c = jnp.concatenate([cos, cos, ones], axis=1)
    lo = jnp.concatenate([-sin, zeros_h, zeros_r], axis=1)
    hi = jnp.concatenate([zeros_h, sin, zeros_r], axis=1)
    two = lambda t: jnp.concatenate([t, t], axis=1)
    return two(c), two(lo), two(hi)


def kernel(x_prompt, x_sample, cache_k, cache_v, state_conv, page_table, ln_g, ln_b, ffn1_w_gate, ffn1_w_up, ffn1_w_down, w_in, conv_w, lambda_q1, lambda_k1, lambda_q2, lambda_k2, subln_g, w_attn_out, w_conv_out, w_o, ffn2_w_gate, ffn2_w_up, ffn2_w_down):
    batch, seq, d_model = x_prompt.shape
    dec_batch, dec_seq, _ = x_sample.shape
    assert dec_seq == 1
    depth = ln_g.shape[0]
    page_size, n_heads, v_dim = cache_k.shape[2:]
    head_dim = v_dim // 2
    rot_dim = head_dim // 4
    width = n_heads * v_dim
    d_conv = conv_w.shape[2]
    assert width == d_conv and v_dim == LANES and n_heads == SUBLANES
    past_len = page_table.shape[1] * page_size
    alpha = (2.0 * depth) ** 0.25
    q_scale = head_dim ** -0.5

    tabs_p = _rope_tables(jnp.arange(seq, dtype=jnp.int32), head_dim, rot_dim)
    tabs_s = _rope_tables(jnp.full((dec_batch,), past_len, jnp.int32), head_dim, rot_dim)

    y_p = x_prompt.reshape(batch * seq, d_model)
    y_s = x_sample.reshape(dec_batch, d_model)
    outs = [[] for _ in range(6)]
    for layer in range(depth):
        lam_init = 0.8 - 0.6 * math.exp(-0.3 * layer)
        bf = lambda w: w[layer].astype(BF16)
        wg1, wu1, wd1 = ffn1_w_gate[layer], ffn1_w_up[layer], ffn1_w_down[layer]
        wg2, wu2, wd2 = ffn2_w_gate[layer], ffn2_w_up[layer], ffn2_w_down[layer]
        win, wa, wc, wo = bf(w_in), bf(w_attn_out), bf(w_conv_out), bf(w_o)
        g = [ln_g[layer, i][None, :] for i in range(3)]
        b = [ln_b[layer, i][None, :] for i in range(3)]
        lam_vecs = [v[layer][None, :] for v in (lambda_q1, lambda_k1, lambda_q2, lambda_k2)]
        sg = subln_g[layer][None, :]
        cw = conv_w[layer]

        x1 = _ffn_ln(y_p, wg1, wu1, wd1, g[0], b[0], alpha=alpha, tm=1024, tf=256)
        q, k, v, kb, vb, conv_in, gates = _inproj(x1, win, *tabs_p, tm=512, tn=width, q_scale=q_scale,
                                                  rot_half=rot_dim // 2, n_heads=n_heads)
        s1 = _ffn_ln(y_s, wg1, wu1, wd1, g[0], b[0], alpha=alpha, tm=dec_batch, tf=512)
        qs, ks, vs, _, _, conv_in_s, gates_s = _inproj(s1, win, *tabs_s, tm=dec_batch, tn=width,
                                                       q_scale=q_scale, rot_half=rot_dim // 2, n_heads=n_heads)

        heads = lambda t: t.reshape(dec_batch, n_heads, v_dim)
        os_, o = _attention(page_table, heads(qs), heads(ks), heads(vs), cache_k[layer], cache_v[layer],
                            q, kb, vb, lam_vecs, sg, batch=batch, seq=seq, lam_init=lam_init,
                            pages_per_step=16, tq=512, row_chunks=2)

        x2, tails = _merge_prompt(o, conv_in, gates, x1, cw, wa, wc, wo, g[1], b[1],
                                  alpha=alpha, tm=256, seq=seq)
        y_p = _ffn_ln(x2, wg2, wu2, wd2, g[2], b[2], alpha=alpha, tm=1024, tf=256)
        tails = tails.reshape(batch, seq // 256, SUBLANES, d_conv)
        outs[0].append(k.reshape(batch, seq, n_heads, v_dim))
        outs[1].append(v.reshape(batch, seq, n_heads, v_dim))
        outs[2].append(tails[:, -1, SUBLANES - 2:, :])

        st = state_conv[layer]
        s2, u_s = _merge_sample(os_.reshape(dec_batch, width), conv_in_s, gates_s, st[:, 0], st[:, 1], s1,
                                cw, wa, wc, wo, g[1], b[1], alpha=alpha)
        y_s = _ffn_ln(s2, wg2, wu2, wd2, g[2], b[2], alpha=alpha, tm=dec_batch, tf=512)
        outs[3].append(ks.reshape(dec_batch, 1, n_heads, v_dim))
        outs[4].append(vs.reshape(dec_batch, 1, n_heads, v_dim))
        outs[5].append(jnp.stack([st[:, 1], u_s], axis=1))

    return (y_p.reshape(batch, seq, d_model), y_s.reshape(dec_batch, 1, d_model),
            jnp.stack(outs[0]), jnp.stack(outs[1]), jnp.stack(outs[2]),
            jnp.stack(outs[3]), jnp.stack(outs[4]), jnp.stack(outs[5]))
```

```python
import functools
import math

import jax
import jax.numpy as jnp
from jax import lax
from jax.experimental import pallas as pl
from jax.experimental.pallas import tpu as pltpu

ROPE_THETA = 500000.0
LN_EPS = 1e-5
NEG_INF = -1e30
LANES = 128
SUBLANES = 8
VMEM_LIMIT = 56 * 1024 * 1024
LN_ROWS = 256
SOFTMAX_UNROLL = 4
BF16 = jnp.bfloat16
F32 = jnp.float32


def _layer_norm(y, g, b):
    mu = jnp.mean(y, axis=-1, keepdims=True)
    d = y - mu
    var = jnp.mean(d * d, axis=-1, keepdims=True)
    return d * lax.rsqrt(var + LN_EPS) * g + b


def _ffn_ln_kernel(x_ref, wg_ref, wu_ref, wd_ref, g_ref, b_ref, o_ref, xb_ref, *, alpha):
    f = pl.program_id(1)
    nf = pl.num_programs(1)

    @pl.when(f == 0)
    def _():
        xb_ref[...] = x_ref[...].astype(BF16)
        o_ref[...] = jnp.zeros(o_ref.shape, F32)

    xb = xb_ref[...]
    hg = jnp.dot(xb, wg_ref[...].astype(BF16), preferred_element_type=F32)
    hu = jnp.dot(xb, wu_ref[...].astype(BF16), preferred_element_type=F32)
    h = (hg * jax.nn.sigmoid(hg) * hu).astype(BF16)
    o_ref[...] += jnp.dot(h, wd_ref[...].astype(BF16), preferred_element_type=F32)

    @pl.when(f == nf - 1)
    def _():
        rows = min(x_ref.shape[0], LN_ROWS)
        for r in range(x_ref.shape[0] // rows):
            sl = slice(r * rows, (r + 1) * rows)
            y = alpha * x_ref[sl, :] + 0.5 * o_ref[sl, :]
            o_ref[sl, :] = _layer_norm(y, g_ref[...], b_ref[...])


def _ffn_ln(x, wg, wu, wd, g, b, *, alpha, tm, tf):
    m, d = x.shape
    dff = wg.shape[1]
    return pl.pallas_call(
        functools.partial(_ffn_ln_kernel, alpha=alpha),
        out_shape=jax.ShapeDtypeStruct((m, d), F32),
        grid=(m // tm, dff // tf),
        in_specs=[
            pl.BlockSpec((tm, d), lambda i, f: (i, 0)),
            pl.BlockSpec((d, tf), lambda i, f: (0, f)),
            pl.BlockSpec((d, tf), lambda i, f: (0, f)),
            pl.BlockSpec((tf, d), lambda i, f: (f, 0)),
            pl.BlockSpec((1, d), lambda i, f: (0, 0)),
            pl.BlockSpec((1, d), lambda i, f: (0, 0)),
        ],
        out_specs=pl.BlockSpec((tm, d), lambda i, f: (i, 0)),
        scratch_shapes=[pltpu.VMEM((tm, d), BF16)],
        compiler_params=pltpu.CompilerParams(
            dimension_semantics=("parallel", "arbitrary"), vmem_limit_bytes=VMEM_LIMIT),
        name="ffn_ln",
    )(x, wg, wu, wd, g, b)


N_QKV_TILES = 3
N_CONV_TILES = 3
def _rope_tile(z, cos, sin_lo, sin_hi, rot_half):
    up = pltpu.roll(z, LANES - rot_half, 1)
    down = pltpu.roll(z, rot_half, 1)
    return z * cos + up * sin_lo + down * sin_hi


def _inproj_kernel(x_ref, w_ref, cos_ref, slo_ref, shi_ref,
                   q_ref, k_ref, v_ref, kb_ref, vb_ref, r_ref, gate_ref, xb_ref,
                   *, q_scale, rot_half, n_heads):
    j = pl.program_id(1)

    @pl.when(j == 0)
    def _():
        xb_ref[...] = x_ref[...].astype(BF16)

    @pl.when(j < N_QKV_TILES + N_CONV_TILES)
    def _():
        r_ref[...] = jnp.dot(xb_ref[...], w_ref[...], preferred_element_type=F32)

    @pl.when(j >= N_QKV_TILES + N_CONV_TILES)
    def _():
        gate_ref[...] = jnp.dot(xb_ref[...], w_ref[...], preferred_element_type=F32).astype(BF16)

    @pl.when(j == 0)
    def _():
        for h in range(n_heads):
            sl = slice(h * LANES, (h + 1) * LANES)
            r = _rope_tile(r_ref[:, sl], cos_ref[...], slo_ref[...], shi_ref[...], rot_half)
            q_ref[:, sl] = (r * q_scale).astype(BF16)

    @pl.when(j == 1)
    def _():
        for h in range(n_heads):
            sl = slice(h * LANES, (h + 1) * LANES)
            r = _rope_tile(r_ref[:, sl], cos_ref[...], slo_ref[...], shi_ref[...], rot_half)
            k_ref[:, sl] = r
            kb_ref[:, sl] = r.astype(BF16)

    @pl.when(j == 2)
    def _():
        v_ref[...] = r_ref[...]
        vb_ref[...] = r_ref[...].astype(BF16)


def _inproj(x, w, cos, slo, shi, *, tm, tn, q_scale, rot_half, n_heads):
    m, d = x.shape
    n = w.shape[1]
    nj = n // tn
    n_first = N_QKV_TILES + N_CONV_TILES
    n_gate_tiles = nj - n_first
    tab_blocks = cos.shape[0] // tm
    tab_spec = pl.BlockSpec((tm, LANES), lambda i, j: (i % tab_blocks, 0))
    row_spec = pl.BlockSpec((tm, tn), lambda i, j: (i, 0))
    return pl.pallas_call(
        functools.partial(_inproj_kernel, q_scale=q_scale, rot_half=rot_half, n_heads=n_heads),
        out_shape=(
            jax.ShapeDtypeStruct((m, tn), BF16),
            jax.ShapeDtypeStruct((m, tn), F32),
            jax.ShapeDtypeStruct((m, tn), F32),
            jax.ShapeDtypeStruct((m, tn), BF16),
            jax.ShapeDtypeStruct((m, tn), BF16),
            jax.ShapeDtypeStruct((m, N_CONV_TILES * tn), F32),
            jax.ShapeDtypeStruct((m, n_gate_tiles * tn), BF16),
        ),
        grid=(m // tm, nj),
        in_specs=[
            pl.BlockSpec((tm, d), lambda i, j: (i, 0)),
            pl.BlockSpec((d, tn), lambda i, j: (0, j)),
            tab_spec, tab_spec, tab_spec,
        ],
        out_specs=(row_spec, row_spec, row_spec, row_spec, row_spec,
                   pl.BlockSpec((tm, tn), lambda i, j: (i, jnp.clip(j - N_QKV_TILES, 0, N_CONV_TILES - 1))),
                   pl.BlockSpec((tm, tn), lambda i, j: (i, jnp.maximum(j - n_first, 0)))),
        scratch_shapes=[pltpu.VMEM((tm, d), BF16)],
        compiler_params=pltpu.CompilerParams(
            dimension_semantics=("parallel", "arbitrary"), vmem_limit_bytes=VMEM_LIMIT),
        name="inproj",
    )(x, w, cos, slo, shi)


def _lambda_value(lq1_ref, lk1_ref, lq2_ref, lk2_ref, lam_init):
    s1 = jnp.sum(lq1_ref[...] * lk1_ref[...], axis=-1, keepdims=True)
    s2 = jnp.sum(lq2_ref[...] * lk2_ref[...], axis=-1, keepdims=True)
    return jnp.exp(s1) - jnp.exp(s2) + lam_init


def _head_rmsnorm(o, g, lam_init):
    ms = jnp.mean(o * o, axis=-1, keepdims=True)
    return o * lax.rsqrt(ms + LN_EPS) * g * (1.0 - lam_init)


def _reduce_keys(x, reduce_fn, combine_fn, groups=8):
    n, w = x.shape
    x3 = x.reshape(n // SUBLANES, SUBLANES, w)
    per = x3.shape[0] // groups
    parts = [reduce_fn(x3[g * per:(g + 1) * per], axis=0) for g in range(groups)]
    while len(parts) > 1:
        parts = [combine_fn(parts[i], parts[i + 1]) for i in range(0, len(parts), 2)]
    return reduce_fn(parts[0], axis=0, keepdims=True)


def _prompt_attn_unit(qi, q_ref, k_ref, v_ref, lam, g_ref, o_ref, vt_ref,
                      *, lam_init, nq, row_chunks):
    tq, vd = q_ref.shape
    head_dim = vd // 2
    rc = tq // row_chunks
    seq = k_ref.shape[0]

    @pl.when(qi == 0)
    def _():
        for c in range(seq // tq):
            cols = slice(c * tq, (c + 1) * tq)
            vt_ref[:, cols] = v_ref[cols, :].astype(F32).T.astype(BF16)

    q = q_ref[...]
    lane = lax.broadcasted_iota(jnp.int32, q.shape, 1)
    zero = jnp.zeros_like(q)
    q_sub = (jnp.where(lane < head_dim, q, zero), jnp.where(lane >= head_dim, q, zero))
    key = lax.broadcasted_iota(jnp.int32, (rc, rc), 0)
    qry = lax.broadcasted_iota(jnp.int32, (rc, rc), 1)
    keep = jnp.concatenate([key <= qry, key <= qry], axis=1)

    def softmax_av(qc, ext):
        s = lax.dot_general(k_ref[0:ext, :], qc, (((1,), (1,)), ((), ())), preferred_element_type=F32)
        diag = jnp.where(keep, s[ext - rc:, :], NEG_INF)
        s = diag if ext == rc else jnp.concatenate([s[:ext - rc, :], diag], axis=0)
        m = _reduce_keys(s, jnp.max, jnp.maximum)
        p = jnp.exp(s - m)
        l = _reduce_keys(p, jnp.sum, jnp.add)
        return jnp.dot(vt_ref[:, 0:ext], p.astype(BF16), preferred_element_type=F32) / l

    for i in range(nq):
        @pl.when(qi == i)
        def _(i=i):
            for r in range(row_chunks):
                rows = slice(r * rc, (r + 1) * rc)
                ext = i * tq + (r + 1) * rc
                o = softmax_av(jnp.concatenate([q_sub[0][rows], q_sub[1][rows]], axis=0), ext)
                o = o[:, :rc] - lam * o[:, rc:]
                o_ref[rows, :] = _head_rmsnorm(o.T, g_ref[...], lam_init).astype(BF16)


def _class_reduce(x, op, n_classes):
    shift = n_classes
    while shift < x.shape[-1]:
        x = op(x, pltpu.roll(x, shift, 1))
        shift *= 2
    return x


def _attn_kernel(pt_ref, q_ref, kn_ref, vn_ref, lq1_ref, lk1_ref, lq2_ref, lk2_ref, g_ref,
                 qp_ref, kp_ref, vp_ref, *rest,
                 lam_init, pages_per_step, n_pages, steps_per_unit, nq, row_chunks):
    pp = pages_per_step
    k_refs = rest[:pp]
    v_refs = rest[pp:2 * pp]
    o_ref, op_ref = rest[2 * pp:2 * pp + 2]
    s_scr, a_scr, qt_scr, anew_scr, acc_scr, vt_scr = rest[2 * pp + 2:]
    g = pl.program_id(1)
    ng = n_pages // pp
    step = pl.program_id(0) * (2 * ng) + g

    @pl.when(step % steps_per_unit == 0)
    def _():
        lam = _lambda_value(lq1_ref, lk1_ref, lq2_ref, lk2_ref, lam_init)
        _prompt_attn_unit((step // steps_per_unit) % nq, qp_ref, kp_ref, vp_ref, lam, g_ref, op_ref, vt_scr,
                          lam_init=lam_init, nq=nq, row_chunks=row_chunks)

    page, n_heads, vd = k_refs[0].shape
    rows = page * n_heads
    hd = vd // 2
    nt = (((1,), (1,)), ((), ()))

    def pad_rows(x):
        return jnp.concatenate([x, jnp.zeros((LANES - n_heads, vd), x.dtype)], axis=0)

    @pl.when(g == 0)
    def _():
        q = q_ref[...]
        qq = jnp.concatenate([q, q], axis=0)
        row = lax.broadcasted_iota(jnp.int32, qq.shape, 0)
        lane = lax.broadcasted_iota(jnp.int32, qq.shape, 1)
        qt_scr[...] = jnp.where(row // n_heads == lane // hd, qq, jnp.zeros_like(qq))

    @pl.when(g < ng)
    def _():
        for p in range(pp):
            kb = k_refs[p][...].reshape(rows, vd).astype(BF16)
            s_scr[g * pp + p] = lax.dot_general(qt_scr[...], kb, nt, preferred_element_type=F32)

    @pl.when(g == ng)
    def _():
        lane1 = lax.broadcasted_iota(jnp.int32, (2 * n_heads, LANES), 1)
        s_new = lax.dot_general(qt_scr[...], pad_rows(kn_ref[...]).astype(BF16), nt,
                                preferred_element_type=F32)
        s_new = jnp.where(lane1 < n_heads, s_new, NEG_INF)

        def with_new(x, x_new, op):
            return jnp.concatenate([op(x[:, :LANES], x_new), x[:, LANES:]], axis=1)

        m = lax.fori_loop(0, n_pages, lambda i, m: jnp.maximum(m, s_scr[i]),
                          jnp.full((2 * n_heads, rows), NEG_INF, F32), unroll=SOFTMAX_UNROLL)
        m = _class_reduce(with_new(m, s_new, jnp.maximum), jnp.maximum, n_heads)

        def sum_body(i, l):
            p = jnp.exp(s_scr[i] - m)
            s_scr[i] = p
            return l + p
        l = lax.fori_loop(0, n_pages, sum_body, jnp.zeros((2 * n_heads, rows), F32), unroll=SOFTMAX_UNROLL)
        p_new = jnp.exp(s_new - m[:, :LANES])
        l = _class_reduce(with_new(l, p_new, jnp.add), jnp.add, n_heads)

        lam = _lambda_value(lq1_ref, lk1_ref, lq2_ref, lk2_ref, lam_init)
        row = lax.broadcasted_iota(jnp.int32, (n_heads, rows), 0)
        lane = lax.broadcasted_iota(jnp.int32, (n_heads, rows), 1)
        own = row == lane % n_heads

        def a_body(i, c):
            pn = s_scr[i] / l
            a_scr[i] = jnp.where(own, pn[:n_heads] - lam * pn[n_heads:], 0.0)
            return c
        lax.fori_loop(0, n_pages, a_body, 0, unroll=SOFTMAX_UNROLL)
        pn_new = p_new / l[:, :LANES]
        anew_scr[...] = jnp.where(own[:, :LANES], pn_new[:n_heads] - lam * pn_new[n_heads:], 0.0)
        acc_scr[...] = jnp.zeros(acc_scr.shape, F32)

    @pl.when(g >= ng)
    def _():
        acc = acc_scr[...]
        for p in range(pp):
            vb = v_refs[p][...].reshape(rows, vd).astype(BF16)
            acc = acc + jnp.dot(a_scr[(g - ng) * pp + p].astype(BF16), vb, preferred_element_type=F32)
        acc_scr[...] = acc

    @pl.when(g == 2 * ng - 1)
    def _():
        o = acc_scr[...] + jnp.dot(anew_scr[...].astype(BF16), pad_rows(vn_ref[...]).astype(BF16),
                                   preferred_element_type=F32)
        o_ref[...] = _head_rmsnorm(o, g_ref[...], lam_init).astype(o_ref.dtype)


def _attention(page_table, q, k_new, v_new, cache_k, cache_v, qp, kp, vp, lam_vecs, subln_g,
               *, batch, seq, lam_init, pages_per_step, tq, row_chunks):
    nb = q.shape[0]
    n_pages = page_table.shape[1]
    pp = pages_per_step
    ng = n_pages // pp
    _, page, n_heads, vd = cache_k.shape
    rows = page * n_heads
    pt_flat = page_table.reshape(-1)
    nq = seq // tq
    n_units = batch * n_heads * nq
    n_steps = nb * 2 * ng
    steps_per_unit = n_steps // n_units
    assert steps_per_unit * n_units == n_steps

    def unit(b, g):
        u = (b * (2 * ng) + g) // steps_per_unit
        return u // (n_heads * nq), (u // nq) % n_heads, u % nq

    def qp_map(b, g, pt):
        pb, h, i = unit(b, g)
        return pb * nq + i, h

    def kvp_map(b, g, pt):
        pb, h, _ = unit(b, g)
        return pb, h

    head_spec = pl.BlockSpec((None, n_heads, vd), lambda b, g, pt: (b, 0, 0))
    vec = pl.BlockSpec((1, lam_vecs[0].shape[1]), lambda b, g, pt: (0, 0))

    def k_spec(p):
        return pl.BlockSpec((None, page, n_heads, vd),
                            lambda b, g, pt: (pt[b * n_pages + jnp.minimum(g, ng - 1) * pp + p], 0, 0, 0))

    def v_spec(p):
        return pl.BlockSpec((None, page, n_heads, vd),
                            lambda b, g, pt: (pt[b * n_pages + jnp.maximum(g - ng, 0) * pp + p], 0, 0, 0))

    grid_spec = pltpu.PrefetchScalarGridSpec(
        num_scalar_prefetch=1,
        grid=(nb, 2 * ng),
        in_specs=[head_spec, head_spec, head_spec, vec, vec, vec, vec,
                  pl.BlockSpec((1, vd), lambda b, g, pt: (0, 0)),
                  pl.BlockSpec((tq, vd), qp_map),
                  pl.BlockSpec((seq, vd), kvp_map),
                  pl.BlockSpec((seq, vd), kvp_map)]
                 + [k_spec(p) for p in range(pp)] + [v_spec(p) for p in range(pp)],
        out_specs=(head_spec, pl.BlockSpec((tq, vd), qp_map)),
        scratch_shapes=[
            pltpu.VMEM((n_pages, 2 * n_heads, rows), F32),
            pltpu.VMEM((n_pages, n_heads, rows), F32),
            pltpu.VMEM((2 * n_heads, vd), BF16),
            pltpu.VMEM((n_heads, LANES), F32),
            pltpu.VMEM((n_heads, vd), F32),
            pltpu.VMEM((vd, seq), BF16),
        ],
    )
    return pl.pallas_call(
        functools.partial(_attn_kernel, lam_init=lam_init, pages_per_step=pp, n_pages=n_pages,
                          steps_per_unit=steps_per_unit, nq=nq, row_chunks=row_chunks),
        out_shape=(jax.ShapeDtypeStruct((nb, n_heads, vd), BF16),
                   jax.ShapeDtypeStruct(qp.shape, BF16)),
        grid_spec=grid_spec,
        compiler_params=pltpu.CompilerParams(
            dimension_semantics=("arbitrary", "arbitrary"), vmem_limit_bytes=VMEM_LIMIT),
        name="attention",
    )(pt_flat, q, k_new, v_new, *lam_vecs, subln_g, qp, kp, vp, *([cache_k] * pp), *([cache_v] * pp))


def _mixer_tail(o, cb, conv, ga, gc, x, wa_ref, wc_ref, wo_ref, g_ref, b_ref, alpha):
    ga = ga.astype(F32)
    gc = gc.astype(F32)
    y_attn = jnp.dot(o, wa_ref[...], preferred_element_type=F32)
    y_conv = jnp.dot((cb * conv).astype(BF16), wc_ref[...], preferred_element_type=F32)
    merged = jax.nn.sigmoid(ga) * y_attn + jax.nn.sigmoid(gc) * y_conv
    z = jnp.dot(merged.astype(BF16), wo_ref[...], preferred_element_type=F32)
    return _layer_norm(alpha * x + z, g_ref[...], b_ref[...])


def _merge_prompt_kernel(o_ref, cb_ref, cc_ref, ch_ref, pc_ref, ph_ref, ga_ref, gc_ref, x_ref,
                         cw_ref, wa_ref, wc_ref, wo_ref, g_ref, b_ref,
                         out_ref, tail_ref, u_scr, *, alpha, tiles_per_seq):
    i = pl.program_id(0)
    tm = cc_ref.shape[0]
    u = cc_ref[...] * ch_ref[...]
    prev = pc_ref[...] * ph_ref[...]
    prev = jnp.where(i % tiles_per_seq == 0, jnp.zeros_like(prev), prev)
    u_scr[0:SUBLANES, :] = prev
    u_scr[SUBLANES:SUBLANES + tm, :] = u
    u1 = u_scr[SUBLANES - 1:SUBLANES - 1 + tm, :]
    u2 = u_scr[SUBLANES - 2:SUBLANES - 2 + tm, :]
    conv = cw_ref[0:1, :] * u2 + cw_ref[1:2, :] * u1 + cw_ref[2:3, :] * u
    tail_ref[...] = u[tm - SUBLANES:tm, :]
    out_ref[...] = _mixer_tail(o_ref[...], cb_ref[...], conv, ga_ref[...], gc_ref[...],
                               x_ref[...], wa_ref, wc_ref, wo_ref, g_ref, b_ref, alpha)


def _merge_sample_kernel(o_ref, cb_ref, cc_ref, ch_ref, s0_ref, s1_ref, ga_ref, gc_ref, x_ref,
                         cw_ref, wa_ref, wc_ref, wo_ref, g_ref, b_ref,
                         out_ref, u_ref, *, alpha):
    u = cc_ref[...] * ch_ref[...]
    conv = cw_ref[0:1, :] * s0_ref[...] + cw_ref[1:2, :] * s1_ref[...] + cw_ref[2:3, :] * u
    u_ref[...] = u
    out_ref[...] = _mixer_tail(o_ref[...], cb_ref[...], conv, ga_ref[...], gc_ref[...],
                               x_ref[...], wa_ref, wc_ref, wo_ref, g_ref, b_ref, alpha)


def _const_spec(shape):
    return pl.BlockSpec(shape, lambda i: (0,) * len(shape), pipeline_mode=pl.Buffered(1))


def _merge_prompt(o, conv_in, gates, x, conv_w, wa, wc, wo, g, b, *, alpha, tm, seq):
    m, d = x.shape
    dc = o.shape[1]
    rows8 = tm // SUBLANES
    in_specs = [
        pl.BlockSpec((tm, dc), lambda i: (i, 0)),
        pl.BlockSpec((tm, dc), lambda i: (i, 0)),
        pl.BlockSpec((tm, dc), lambda i: (i, 1)),
        pl.BlockSpec((tm, dc), lambda i: (i, 2)),
        pl.BlockSpec((SUBLANES, dc), lambda i: (jnp.maximum(i * rows8 - 1, 0), 1)),
        pl.BlockSpec((SUBLANES, dc), lambda i: (jnp.maximum(i * rows8 - 1, 0), 2)),
        pl.BlockSpec((tm, d), lambda i: (i, 0)),
        pl.BlockSpec((tm, d), lambda i: (i, 1)),
        pl.BlockSpec((tm, d), lambda i: (i, 0)),
        _const_spec(conv_w.shape), _const_spec(wa.shape), _const_spec(wc.shape), _const_spec(wo.shape),
        _const_spec(g.shape), _const_spec(b.shape),
    ]
    return pl.pallas_call(
        functools.partial(_merge_prompt_kernel, alpha=alpha, tiles_per_seq=seq // tm),
        out_shape=(jax.ShapeDtypeStruct((m, d), F32),
                   jax.ShapeDtypeStruct((m // tm * SUBLANES, dc), F32)),
        grid=(m // tm,),
        in_specs=in_specs,
        out_specs=(pl.BlockSpec((tm, d), lambda i: (i, 0)),
                   pl.BlockSpec((SUBLANES, dc), lambda i: (i, 0))),
        scratch_shapes=[pltpu.VMEM((tm + SUBLANES, dc), F32)],
        compiler_params=pltpu.CompilerParams(
            dimension_semantics=("parallel",), vmem_limit_bytes=VMEM_LIMIT),
        name="merge_prompt",
    )(o, conv_in, conv_in, conv_in, conv_in, conv_in, gates, gates, x, conv_w, wa, wc, wo, g, b)


def _merge_sample(o, conv_in, gates, s0, s1, x, conv_w, wa, wc, wo, g, b, *, alpha):
    m, d = x.shape
    dc = o.shape[1]
    in_specs = [
        pl.BlockSpec((m, dc), lambda i: (0, 0)),
        pl.BlockSpec((m, dc), lambda i: (0, 0)),
        pl.BlockSpec((m, dc), lambda i: (0, 1)),
        pl.BlockSpec((m, dc), lambda i: (0, 2)),
        pl.BlockSpec((m, dc), lambda i: (0, 0)),
        pl.BlockSpec((m, dc), lambda i: (0, 0)),
        pl.BlockSpec((m, d), lambda i: (0, 0)),
        pl.BlockSpec((m, d), lambda i: (0, 1)),
        pl.BlockSpec((m, d), lambda i: (0, 0)),
        _const_spec(conv_w.shape), _const_spec(wa.shape), _const_spec(wc.shape), _const_spec(wo.shape),
        _const_spec(g.shape), _const_spec(b.shape),
    ]
    return pl.pallas_call(
        functools.partial(_merge_sample_kernel, alpha=alpha),
        out_shape=(jax.ShapeDtypeStruct((m, d), F32), jax.ShapeDtypeStruct((m, dc), F32)),
        grid=(1,),
        in_specs=in_specs,
        out_specs=(pl.BlockSpec((m, d), lambda i: (0, 0)), pl.BlockSpec((m, dc), lambda i: (0, 0))),
        compiler_params=pltpu.CompilerParams(
            dimension_semantics=("arbitrary",), vmem_limit_bytes=VMEM_LIMIT),
        name="merge_sample",
    )(o, conv_in, conv_in, conv_in, s0, s1, gates, gates, x, conv_w, wa, wc, wo, g, b)


def _rope_tables(pos, head_dim, rot_dim):
    half = rot_dim // 2
    inv_freq = jnp.power(ROPE_THETA, -jnp.arange(0, rot_dim, 2, dtype=F32) / rot_dim)
    ang = pos.astype(F32)[:, None] * inv_freq[None, :]
    cos, sin = jnp.cos(ang), jnp.sin(ang)
    n = pos.shape[0]
    ones = jnp.ones((n, head_dim - rot_dim), F32)
    zeros_h = jnp.zeros((n, half), F32)
    zeros_r = jnp.zeros((n, head_dim - rot_dim), F32)
    c = jnp.concatenate([cos, cos, ones], axis=1)
    lo = jnp.concatenate([-sin, zeros_h, zeros_r], axis=1)
    hi = jnp.concatenate([zeros_h, sin, zeros_r], axis=1)
    two = lambda t: jnp.concatenate([t, t], axis=1)
    return two(c), two(lo), two(hi)


def kernel(x_prompt, x_sample, cache_k, cache_v, state_conv, page_table, ln_g, ln_b, ffn1_w_gate, ffn1_w_up, ffn1_w_down, w_in, conv_w, lambda_q1, lambda_k1, lambda_q2, lambda_k2, subln_g, w_attn_out, w_conv_out, w_o, ffn2_w_gate, ffn2_w_up, ffn2_w_down):
    batch, seq, d_model = x_prompt.shape
    dec_batch, dec_seq, _ = x_sample.shape
    assert dec_seq == 1
    depth = ln_g.shape[0]
    page_size, n_heads, v_dim = cache_k.shape[2:]
    head_dim = v_dim // 2
    rot_dim = head_dim // 4
    width = n_heads * v_dim
    d_conv = conv_w.shape[2]
    assert width == d_conv and v_dim == LANES and n_heads == SUBLANES
    past_len = page_table.shape[1] * page_size
    alpha = (2.0 * depth) ** 0.25
    q_scale = head_dim ** -0.5

    tabs_p = _rope_tables(jnp.arange(seq, dtype=jnp.int32), head_dim, rot_dim)
    tabs_s = _rope_tables(jnp.full((dec_batch,), past_len, jnp.int32), head_dim, rot_dim)

    y_p = x_prompt.reshape(batch * seq, d_model)
    y_s = x_sample.reshape(dec_batch, d_model)
    outs = [[] for _ in range(6)]
    for layer in range(depth):
        lam_init = 0.8 - 0.6 * math.exp(-0.3 * layer)
        bf = lambda w: w[layer].astype(BF16)
        wg1, wu1, wd1 = ffn1_w_gate[layer], ffn1_w_up[layer], ffn1_w_down[layer]
        wg2, wu2, wd2 = ffn2_w_gate[layer], ffn2_w_up[layer], ffn2_w_down[layer]
        win, wa, wc, wo = bf(w_in), bf(w_attn_out), bf(w_conv_out), bf(w_o)
        g = [ln_g[layer, i][None, :] for i in range(3)]
        b = [ln_b[layer, i][None, :] for i in range(3)]
        lam_vecs = [v[layer][None, :] for v in (lambda_q1, lambda_k1, lambda_q2, lambda_k2)]
        sg = subln_g[layer][None, :]
        cw = conv_w[layer]

        x1 = _ffn_ln(y_p, wg1, wu1, wd1, g[0], b[0], alpha=alpha, tm=1024, tf=256)
        q, k, v, kb, vb, conv_in, gates = _inproj(x1, win, *tabs_p, tm=512, tn=width, q_scale=q_scale,
                                                  rot_half=rot_dim // 2, n_heads=n_heads)
        s1 = _ffn_ln(y_s, wg1, wu1, wd1, g[0], b[0], alpha=alpha, tm=dec_batch, tf=512)
        qs, ks, vs, _, _, conv_in_s, gates_s = _inproj(s1, win, *tabs_s, tm=dec_batch, tn=width,
                                                       q_scale=q_scale, rot_half=rot_dim // 2, n_heads=n_heads)

        heads = lambda t: t.reshape(dec_batch, n_heads, v_dim)
        os_, o = _attention(page_table, heads(qs), heads(ks), heads(vs), cache_k[layer], cache_v[layer],
                            q, kb, vb, lam_vecs, sg, batch=batch, seq=seq, lam_init=lam_init,
                            pages_per_step=16, tq=256, row_chunks=1)

        x2, tails = _merge_prompt(o, conv_in, gates, x1, cw, wa, wc, wo, g[1], b[1],
                                  alpha=alpha, tm=256, seq=seq)
        y_p = _ffn_ln(x2, wg2, wu2, wd2, g[2], b[2], alpha=alpha, tm=1024, tf=256)
        tails = tails.reshape(batch, seq // 256, SUBLANES, d_conv)
        outs[0].append(k.reshape(batch, seq, n_heads, v_dim))
        outs[1].append(v.reshape(batch, seq, n_heads, v_dim))
        outs[2].append(tails[:, -1, SUBLANES - 2:, :])

        st = state_conv[layer]
        s2, u_s = _merge_sample(os_.reshape(dec_batch, width), conv_in_s, gates_s, st[:, 0], st[:, 1], s1,
                                cw, wa, wc, wo, g[1], b[1], alpha=alpha)
        y_s = _ffn_ln(s2, wg2, wu2, wd2, g[2], b[2], alpha=alpha, tm=dec_batch, tf=512)
        outs[3].append(ks.reshape(dec_batch, 1, n_heads, v_dim))
        outs[4].append(vs.reshape(dec_batch, 1, n_heads, v_dim))
        outs[5].append(jnp.stack([st[:, 1], u_s], axis=1))

    return (y_p.reshape(batch, seq, d_model), y_s.reshape(dec_batch, 1, d_model),
            jnp.stack(outs[0]), jnp.stack(outs[1]), jnp.stack(outs[2]),
            jnp.stack(outs[3]), jnp.stack(outs[4]), jnp.stack(outs[5]))
```

```python
import functools
import math

import jax
import jax.numpy as jnp
from jax import lax
from jax.experimental import pallas as pl
from jax.experimental.pallas import tpu as pltpu

ROPE_THETA = 500000.0
LN_EPS = 1e-5
NEG_INF = -1e30
LANES = 128
SUBLANES = 8
VMEM_LIMIT = 56 * 1024 * 1024
LN_ROWS = 256
SOFTMAX_UNROLL = 4
PAGE_GROUPS = 3
BF16 = jnp.bfloat16
F32 = jnp.float32


def _layer_norm(y, g, b):
    mu = jnp.mean(y, axis=-1, keepdims=True)
    d = y - mu
    var = jnp.mean(d * d, axis=-1, keepdims=True)
    return d * lax.rsqrt(var + LN_EPS) * g + b


def _ffn_ln_kernel(x_ref, wg_ref, wu_ref, wd_ref, g_ref, b_ref, o_ref, xb_ref, *, alpha):
    f = pl.program_id(1)
    nf = pl.num_programs(1)

    @pl.when(f == 0)
    def _():
        xb_ref[...] = x_ref[...].astype(BF16)
        o_ref[...] = jnp.zeros(o_ref.shape, F32)

    xb = xb_ref[...]
    hg = jnp.dot(xb, wg_ref[...].astype(BF16), preferred_element_type=F32)
    hu = jnp.dot(xb, wu_ref[...].astype(BF16), preferred_element_type=F32)
    h = (hg * jax.nn.sigmoid(hg) * hu).astype(BF16)
    o_ref[...] += jnp.dot(h, wd_ref[...].astype(BF16), preferred_element_type=F32)

    @pl.when(f == nf - 1)
    def _():
        rows = min(x_ref.shape[0], LN_ROWS)
        for r in range(x_ref.shape[0] // rows):
            sl = slice(r * rows, (r + 1) * rows)
            y = alpha * x_ref[sl, :] + 0.5 * o_ref[sl, :]
            o_ref[sl, :] = _layer_norm(y, g_ref[...], b_ref[...])


def _ffn_ln(x, wg, wu, wd, g, b, *, alpha, tm, tf):
    m, d = x.shape
    dff = wg.shape[1]
    return pl.pallas_call(
        functools.partial(_ffn_ln_kernel, alpha=alpha),
        out_shape=jax.ShapeDtypeStruct((m, d), F32),
        grid=(m // tm, dff // tf),
        in_specs=[
            pl.BlockSpec((tm, d), lambda i, f: (i, 0)),
            pl.BlockSpec((d, tf), lambda i, f: (0, f)),
            pl.BlockSpec((d, tf), lambda i, f: (0, f)),
            pl.BlockSpec((tf, d), lambda i, f: (f, 0)),
            pl.BlockSpec((1, d), lambda i, f: (0, 0)),
            pl.BlockSpec((1, d), lambda i, f: (0, 0)),
        ],
        out_specs=pl.BlockSpec((tm, d), lambda i, f: (i, 0)),
        scratch_shapes=[pltpu.VMEM((tm, d), BF16)],
        compiler_params=pltpu.CompilerParams(
            dimension_semantics=("parallel", "arbitrary"), vmem_limit_bytes=VMEM_LIMIT),
        name="ffn_ln",
    )(x, wg, wu, wd, g, b)


N_QKV_TILES = 3
N_CONV_TILES = 3
def _rope_tile(z, cos, sin_lo, sin_hi, rot_half):
    up = pltpu.roll(z, LANES - rot_half, 1)
    down = pltpu.roll(z, rot_half, 1)
    return z * cos + up * sin_lo + down * sin_hi


def _inproj_kernel(x_ref, w_ref, cos_ref, slo_ref, shi_ref,
                   q_ref, k_ref, v_ref, kb_ref, vb_ref, r_ref, gate_ref, xb_ref,
                   *, q_scale, rot_half, n_heads):
    j = pl.program_id(1)

    @pl.when(j == 0)
    def _():
        xb_ref[...] = x_ref[...].astype(BF16)

    @pl.when(j < N_QKV_TILES + N_CONV_TILES)
    def _():
        r_ref[...] = jnp.dot(xb_ref[...], w_ref[...], preferred_element_type=F32)

    @pl.when(j >= N_QKV_TILES + N_CONV_TILES)
    def _():
        gate_ref[...] = jnp.dot(xb_ref[...], w_ref[...], preferred_element_type=F32).astype(BF16)

    @pl.when(j == 0)
    def _():
        for h in range(n_heads):
            sl = slice(h * LANES, (h + 1) * LANES)
            r = _rope_tile(r_ref[:, sl], cos_ref[...], slo_ref[...], shi_ref[...], rot_half)
            q_ref[:, sl] = (r * q_scale).astype(BF16)

    @pl.when(j == 1)
    def _():
        for h in range(n_heads):
            sl = slice(h * LANES, (h + 1) * LANES)
            r = _rope_tile(r_ref[:, sl], cos_ref[...], slo_ref[...], shi_ref[...], rot_half)
            k_ref[:, sl] = r
            kb_ref[:, sl] = r.astype(BF16)

    @pl.when(j == 2)
    def _():
        v_ref[...] = r_ref[...]
        vb_ref[...] = r_ref[...].astype(BF16)


def _inproj(x, w, cos, slo, shi, *, tm, tn, q_scale, rot_half, n_heads):
    m, d = x.shape
    n = w.shape[1]
    nj = n // tn
    n_first = N_QKV_TILES + N_CONV_TILES
    n_gate_tiles = nj - n_first
    tab_blocks = cos.shape[0] // tm
    tab_spec = pl.BlockSpec((tm, LANES), lambda i, j: (i % tab_blocks, 0))
    row_spec = pl.BlockSpec((tm, tn), lambda i, j: (i, 0))
    return pl.pallas_call(
        functools.partial(_inproj_kernel, q_scale=q_scale, rot_half=rot_half, n_heads=n_heads),
        out_shape=(
            jax.ShapeDtypeStruct((m, tn), BF16),
            jax.ShapeDtypeStruct((m, tn), F32),
            jax.ShapeDtypeStruct((m, tn), F32),
            jax.ShapeDtypeStruct((m, tn), BF16),
            jax.ShapeDtypeStruct((m, tn), BF16),
            jax.ShapeDtypeStruct((m, N_CONV_TILES * tn), F32),
            jax.ShapeDtypeStruct((m, n_gate_tiles * tn), BF16),
        ),
        grid=(m // tm, nj),
        in_specs=[
            pl.BlockSpec((tm, d), lambda i, j: (i, 0)),
            pl.BlockSpec((d, tn), lambda i, j: (0, j)),
            tab_spec, tab_spec, tab_spec,
        ],
        out_specs=(row_spec, row_spec, row_spec, row_spec, row_spec,
                   pl.BlockSpec((tm, tn), lambda i, j: (i, jnp.clip(j - N_QKV_TILES, 0, N_CONV_TILES - 1))),
                   pl.BlockSpec((tm, tn), lambda i, j: (i, jnp.maximum(j - n_first, 0)))),
        scratch_shapes=[pltpu.VMEM((tm, d), BF16)],
        compiler_params=pltpu.CompilerParams(
            dimension_semantics=("parallel", "arbitrary"), vmem_limit_bytes=VMEM_LIMIT),
        name="inproj",
    )(x, w, cos, slo, shi)


def _lambda_value(lq1_ref, lk1_ref, lq2_ref, lk2_ref, lam_init):
    s1 = jnp.sum(lq1_ref[...] * lk1_ref[...], axis=-1, keepdims=True)
    s2 = jnp.sum(lq2_ref[...] * lk2_ref[...], axis=-1, keepdims=True)
    return jnp.exp(s1) - jnp.exp(s2) + lam_init


def _head_rmsnorm(o, g, lam_init):
    ms = jnp.mean(o * o, axis=-1, keepdims=True)
    return o * lax.rsqrt(ms + LN_EPS) * g * (1.0 - lam_init)


def _reduce_keys(x, reduce_fn, combine_fn, groups=8):
    n, w = x.shape
    x3 = x.reshape(n // SUBLANES, SUBLANES, w)
    per = x3.shape[0] // groups
    parts = [reduce_fn(x3[g * per:(g + 1) * per], axis=0) for g in range(groups)]
    while len(parts) > 1:
        parts = [combine_fn(parts[i], parts[i + 1]) for i in range(0, len(parts), 2)]
    return reduce_fn(parts[0], axis=0, keepdims=True)


def _prompt_attn_unit(qi, q_ref, k_ref, v_ref, lam, g_ref, o_ref, vt_ref,
                      *, lam_init, nq, row_chunks):
    tq, vd = q_ref.shape
    head_dim = vd // 2
    rc = tq // row_chunks
    seq = k_ref.shape[0]

    @pl.when(qi == 0)
    def _():
        for c in range(seq // tq):
            cols = slice(c * tq, (c + 1) * tq)
            vt_ref[:, cols] = v_ref[cols, :].astype(F32).T.astype(BF16)

    q = q_ref[...]
    lane = lax.broadcasted_iota(jnp.int32, q.shape, 1)
    zero = jnp.zeros_like(q)
    q_sub = (jnp.where(lane < head_dim, q, zero), jnp.where(lane >= head_dim, q, zero))
    key = lax.broadcasted_iota(jnp.int32, (rc, rc), 0)
    qry = lax.broadcasted_iota(jnp.int32, (rc, rc), 1)
    keep = jnp.concatenate([key <= qry, key <= qry], axis=1)

    def softmax_av(qc, ext):
        s = lax.dot_general(k_ref[0:ext, :], qc, (((1,), (1,)), ((), ())), preferred_element_type=F32)
        diag = jnp.where(keep, s[ext - rc:, :], NEG_INF)
        s = diag if ext == rc else jnp.concatenate([s[:ext - rc, :], diag], axis=0)
        m = _reduce_keys(s, jnp.max, jnp.maximum)
        p = jnp.exp(s - m)
        l = _reduce_keys(p, jnp.sum, jnp.add)
        return jnp.dot(vt_ref[:, 0:ext], p.astype(BF16), preferred_element_type=F32) / l

    for i in range(nq):
        @pl.when(qi == i)
        def _(i=i):
            for r in range(row_chunks):
                rows = slice(r * rc, (r + 1) * rc)
                ext = i * tq + (r + 1) * rc
                o = softmax_av(jnp.concatenate([q_sub[0][rows], q_sub[1][rows]], axis=0), ext)
                o = o[:, :rc] - lam * o[:, rc:]
                o_ref[rows, :] = _head_rmsnorm(o.T, g_ref[...], lam_init).astype(BF16)


def _class_reduce(x, op, n_classes):
    shift = n_classes
    while shift < x.shape[-1]:
        x = op(x, pltpu.roll(x, shift, 1))
        shift *= 2
    return x


def _attn_kernel(pt_ref, q_ref, kn_ref, vn_ref, lq1_ref, lk1_ref, lq2_ref, lk2_ref, g_ref,
                 qp_ref, kp_ref, vp_ref, ck_hbm, cv_hbm, o_ref, op_ref,
                 s_scr, a_scr, qt_scr, anew_scr, acc_scr, vt_scr, page_buf, page_sem,
                 *, lam_init, pages_per_step, n_pages, steps_per_unit, nq, row_chunks):
    pp = pages_per_step
    g = pl.program_id(1)
    ng = n_pages // pp
    steps_per_seq = 2 * ng
    n_steps = pl.num_programs(0) * steps_per_seq
    step = pl.program_id(0) * steps_per_seq + g

    def page_copy(src_hbm, page_id, slot):
        return pltpu.make_async_copy(src_hbm.at[page_id], page_buf.at[slot], page_sem.at[slot])

    def fetch_group(t):
        seq_id = t // steps_per_seq
        j = t % steps_per_seq
        is_k = j < ng
        first = seq_id * n_pages + jnp.where(is_k, j, j - ng) * pp
        slot0 = (t % PAGE_GROUPS) * pp

        @pl.when(is_k)
        def _():
            for p in range(pp):
                page_copy(ck_hbm, pt_ref[first + p], slot0 + p).start()

        @pl.when(jnp.logical_not(is_k))
        def _():
            for p in range(pp):
                page_copy(cv_hbm, pt_ref[first + p], slot0 + p).start()

    @pl.when(step == 0)
    def _():
        for t in range(PAGE_GROUPS - 1):
            fetch_group(jnp.int32(t))

    @pl.when(step + PAGE_GROUPS - 1 < n_steps)
    def _():
        fetch_group(step + PAGE_GROUPS - 1)

    @pl.when(step % steps_per_unit == 0)
    def _():
        lam = _lambda_value(lq1_ref, lk1_ref, lq2_ref, lk2_ref, lam_init)
        _prompt_attn_unit((step // steps_per_unit) % nq, qp_ref, kp_ref, vp_ref, lam, g_ref, op_ref, vt_scr,
                          lam_init=lam_init, nq=nq, row_chunks=row_chunks)

    slot0 = (step % PAGE_GROUPS) * pp
    for p in range(pp):
        page_copy(ck_hbm, 0, slot0 + p).wait()

    _, page, n_heads, vd = page_buf.shape
    rows = page * n_heads
    hd = vd // 2
    nt = (((1,), (1,)), ((), ()))

    def pad_rows(x):
        return jnp.concatenate([x, jnp.zeros((LANES - n_heads, vd), x.dtype)], axis=0)

    @pl.when(g == 0)
    def _():
        q = q_ref[...]
        qq = jnp.concatenate([q, q], axis=0)
        row = lax.broadcasted_iota(jnp.int32, qq.shape, 0)
        lane = lax.broadcasted_iota(jnp.int32, qq.shape, 1)
        qt_scr[...] = jnp.where(row // n_heads == lane // hd, qq, jnp.zeros_like(qq))

    @pl.when(g < ng)
    def _():
        for p in range(pp):
            kb = page_buf[slot0 + p].reshape(rows, vd).astype(BF16)
            s_scr[g * pp + p] = lax.dot_general(qt_scr[...], kb, nt, preferred_element_type=F32)

    @pl.when(g == ng)
    def _():
        lane1 = lax.broadcasted_iota(jnp.int32, (2 * n_heads, LANES), 1)
        s_new = lax.dot_general(qt_scr[...], pad_rows(kn_ref[...]).astype(BF16), nt,
                                preferred_element_type=F32)
        s_new = jnp.where(lane1 < n_heads, s_new, NEG_INF)

        def with_new(x, x_new, op):
            return jnp.concatenate([op(x[:, :LANES], x_new), x[:, LANES:]], axis=1)

        m = lax.fori_loop(0, n_pages, lambda i, m: jnp.maximum(m, s_scr[i]),
                          jnp.full((2 * n_heads, rows), NEG_INF, F32), unroll=SOFTMAX_UNROLL)
        m = _class_reduce(with_new(m, s_new, jnp.maximum), jnp.maximum, n_heads)

        def sum_body(i, l):
            p = jnp.exp(s_scr[i] - m)
            s_scr[i] = p
            return l + p
        l = lax.fori_loop(0, n_pages, sum_body, jnp.zeros((2 * n_heads, rows), F32), unroll=SOFTMAX_UNROLL)
        p_new = jnp.exp(s_new - m[:, :LANES])
        l = _class_reduce(with_new(l, p_new, jnp.add), jnp.add, n_heads)

        lam = _lambda_value(lq1_ref, lk1_ref, lq2_ref, lk2_ref, lam_init)
        row = lax.broadcasted_iota(jnp.int32, (n_heads, rows), 0)
        lane = lax.broadcasted_iota(jnp.int32, (n_heads, rows), 1)
        own = row == lane % n_heads

        def a_body(i, c):
            pn = s_scr[i] / l
            a_scr[i] = jnp.where(own, pn[:n_heads] - lam * pn[n_heads:], 0.0)
            return c
        lax.fori_loop(0, n_pages, a_body, 0, unroll=SOFTMAX_UNROLL)
        pn_new = p_new / l[:, :LANES]
        anew_scr[...] = jnp.where(own[:, :LANES], pn_new[:n_heads] - lam * pn_new[n_heads:], 0.0)
        acc_scr[...] = jnp.zeros(acc_scr.shape, F32)

    @pl.when(g >= ng)
    def _():
        acc = acc_scr[...]
        for p in range(pp):
            vb = page_buf[slot0 + p].reshape(rows, vd).astype(BF16)
            acc = acc + jnp.dot(a_scr[(g - ng) * pp + p].astype(BF16), vb, preferred_element_type=F32)
        acc_scr[...] = acc

    @pl.when(g == 2 * ng - 1)
    def _():
        o = acc_scr[...] + jnp.dot(anew_scr[...].astype(BF16), pad_rows(vn_ref[...]).astype(BF16),
                                   preferred_element_type=F32)
        o_ref[...] = _head_rmsnorm(o, g_ref[...], lam_init).astype(o_ref.dtype)


def _attention(page_table, q, k_new, v_new, cache_k, cache_v, qp, kp, vp, lam_vecs, subln_g,
               *, batch, seq, lam_init, pages_per_step, tq, row_chunks):
    nb = q.shape[0]
    n_pages = page_table.shape[1]
    pp = pages_per_step
    ng = n_pages // pp
    _, page, n_heads, vd = cache_k.shape
    rows = page * n_heads
    pt_flat = page_table.reshape(-1)
    nq = seq // tq
    n_units = batch * n_heads * nq
    n_steps = nb * 2 * ng
    steps_per_unit = n_steps // n_units
    assert steps_per_unit * n_units == n_steps

    def unit(b, g):
        u = (b * (2 * ng) + g) // steps_per_unit
        return u // (n_heads * nq), (u // nq) % n_heads, u % nq

    def qp_map(b, g, pt):
        pb, h, i = unit(b, g)
        return pb * nq + i, h

    def kvp_map(b, g, pt):
        pb, h, _ = unit(b, g)
        return pb, h

    head_spec = pl.BlockSpec((None, n_heads, vd), lambda b, g, pt: (b, 0, 0))
    vec = pl.BlockSpec((1, lam_vecs[0].shape[1]), lambda b, g, pt: (0, 0))

    grid_spec = pltpu.PrefetchScalarGridSpec(
        num_scalar_prefetch=1,
        grid=(nb, 2 * ng),
        in_specs=[head_spec, head_spec, head_spec, vec, vec, vec, vec,
                  pl.BlockSpec((1, vd), lambda b, g, pt: (0, 0)),
                  pl.BlockSpec((tq, vd), qp_map),
                  pl.BlockSpec((seq, vd), kvp_map),
                  pl.BlockSpec((seq, vd), kvp_map),
                  pl.BlockSpec(memory_space=pl.ANY),
                  pl.BlockSpec(memory_space=pl.ANY)],
        out_specs=(head_spec, pl.BlockSpec((tq, vd), qp_map)),
        scratch_shapes=[
            pltpu.VMEM((n_pages, 2 * n_heads, rows), F32),
            pltpu.VMEM((n_pages, n_heads, rows), F32),
            pltpu.VMEM((2 * n_heads, vd), BF16),
            pltpu.VMEM((n_heads, LANES), F32),
            pltpu.VMEM((n_heads, vd), F32),
            pltpu.VMEM((vd, seq), BF16),
            pltpu.VMEM((PAGE_GROUPS * pp, page, n_heads, vd), F32),
            pltpu.SemaphoreType.DMA((PAGE_GROUPS * pp,)),
        ],
    )
    return pl.pallas_call(
        functools.partial(_attn_kernel, lam_init=lam_init, pages_per_step=pp, n_pages=n_pages,
                          steps_per_unit=steps_per_unit, nq=nq, row_chunks=row_chunks),
        out_shape=(jax.ShapeDtypeStruct((nb, n_heads, vd), BF16),
                   jax.ShapeDtypeStruct(qp.shape, BF16)),
        grid_spec=grid_spec,
        compiler_params=pltpu.CompilerParams(
            dimension_semantics=("arbitrary", "arbitrary"), vmem_limit_bytes=VMEM_LIMIT),
        name="attention",
    )(pt_flat, q, k_new, v_new, *lam_vecs, subln_g, qp, kp, vp, cache_k, cache_v)


def _mixer_tail(o, cb, conv, ga, gc, x, wa_ref, wc_ref, wo_ref, g_ref, b_ref, alpha):
    ga = ga.astype(F32)
    gc = gc.astype(F32)
    y_attn = jnp.dot(o, wa_ref[...], preferred_element_type=F32)
    y_conv = jnp.dot((cb * conv).astype(BF16), wc_ref[...], preferred_element_type=F32)
    merged = jax.nn.sigmoid(ga) * y_attn + jax.nn.sigmoid(gc) * y_conv
    z = jnp.dot(merged.astype(BF16), wo_ref[...], preferred_element_type=F32)
    return _layer_norm(alpha * x + z, g_ref[...], b_ref[...])


def _merge_prompt_kernel(o_ref, cb_ref, cc_ref, ch_ref, pc_ref, ph_ref, ga_ref, gc_ref, x_ref,
                         cw_ref, wa_ref, wc_ref, wo_ref, g_ref, b_ref,
                         out_ref, tail_ref, u_scr, *, alpha, tiles_per_seq):
    i = pl.program_id(0)
    tm = cc_ref.shape[0]
    u = cc_ref[...] * ch_ref[...]
    prev = pc_ref[...] * ph_ref[...]
    prev = jnp.where(i % tiles_per_seq == 0, jnp.zeros_like(prev), prev)
    u_scr[0:SUBLANES, :] = prev
    u_scr[SUBLANES:SUBLANES + tm, :] = u
    u1 = u_scr[SUBLANES - 1:SUBLANES - 1 + tm, :]
    u2 = u_scr[SUBLANES - 2:SUBLANES - 2 + tm, :]
    conv = cw_ref[0:1, :] * u2 + cw_ref[1:2, :] * u1 + cw_ref[2:3, :] * u
    tail_ref[...] = u[tm - SUBLANES:tm, :]
    out_ref[...] = _mixer_tail(o_ref[...], cb_ref[...], conv, ga_ref[...], gc_ref[...],
                               x_ref[...], wa_ref, wc_ref, wo_ref, g_ref, b_ref, alpha)


def _merge_sample_kernel(o_ref, cb_ref, cc_ref, ch_ref, s0_ref, s1_ref, ga_ref, gc_ref, x_ref,
                         cw_ref, wa_ref, wc_ref, wo_ref, g_ref, b_ref,
                         out_ref, u_ref, *, alpha):
    u = cc_ref[...] * ch_ref[...]
    conv = cw_ref[0:1, :] * s0_ref[...] + cw_ref[1:2, :] * s1_ref[...] + cw_ref[2:3, :] * u
    u_ref[...] = u
    out_ref[...] = _mixer_tail(o_ref[...], cb_ref[...], conv, ga_ref[...], gc_ref[...],
                               x_ref[...], wa_ref, wc_ref, wo_ref, g_ref, b_ref, alpha)


def _const_spec(shape):
    return pl.BlockSpec(shape, lambda i: (0,) * len(shape), pipeline_mode=pl.Buffered(1))


def _merge_prompt(o, conv_in, gates, x, conv_w, wa, wc, wo, g, b, *, alpha, tm, seq):
    m, d = x.shape
    dc = o.shape[1]
    rows8 = tm // SUBLANES
    in_specs = [
        pl.BlockSpec((tm, dc), lambda i: (i, 0)),
        pl.BlockSpec((tm, dc), lambda i: (i, 0)),
        pl.BlockSpec((tm, dc), lambda i: (i, 1)),
        pl.BlockSpec((tm, dc), lambda i: (i, 2)),
        pl.BlockSpec((SUBLANES, dc), lambda i: (jnp.maximum(i * rows8 - 1, 0), 1)),
        pl.BlockSpec((SUBLANES, dc), lambda i: (jnp.maximum(i * rows8 - 1, 0), 2)),
        pl.BlockSpec((tm, d), lambda i: (i, 0)),
        pl.BlockSpec((tm, d), lambda i: (i, 1)),
        pl.BlockSpec((tm, d), lambda i: (i, 0)),
        _const_spec(conv_w.shape), _const_spec(wa.shape), _const_spec(wc.shape), _const_spec(wo.shape),
        _const_spec(g.shape), _const_spec(b.shape),
    ]
    return pl.pallas_call(
        functools.partial(_merge_prompt_kernel, alpha=alpha, tiles_per_seq=seq // tm),
        out_shape=(jax.ShapeDtypeStruct((m, d), F32),
                   jax.ShapeDtypeStruct((m // tm * SUBLANES, dc), F32)),
        grid=(m // tm,),
        in_specs=in_specs,
        out_specs=(pl.BlockSpec((tm, d), lambda i: (i, 0)),
                   pl.BlockSpec((SUBLANES, dc), lambda i: (i, 0))),
        scratch_shapes=[pltpu.VMEM((tm + SUBLANES, dc), F32)],
        compiler_params=pltpu.CompilerParams(
            dimension_semantics=("parallel",), vmem_limit_bytes=VMEM_LIMIT),
        name="merge_prompt",
    )(o, conv_in, conv_in, conv_in, conv_in, conv_in, gates, gates, x, conv_w, wa, wc, wo, g, b)


def _merge_sample(o, conv_in, gates, s0, s1, x, conv_w, wa, wc, wo, g, b, *, alpha):
    m, d = x.shape
    dc = o.shape[1]
    in_specs = [
        pl.BlockSpec((m, dc), lambda i: (0, 0)),
        pl.BlockSpec((m, dc), lambda i: (0, 0)),
        pl.BlockSpec((m, dc), lambda i: (0, 1)),
        pl.BlockSpec((m, dc), lambda i: (0, 2)),
        pl.BlockSpec((m, dc), lambda i: (0, 0)),
        pl.BlockSpec((m, dc), lambda i: (0, 0)),
        pl.BlockSpec((m, d), lambda i: (0, 0)),
        pl.BlockSpec((m, d), lambda i: (0, 1)),
        pl.BlockSpec((m, d), lambda i: (0, 0)),
        _const_spec(conv_w.shape), _const_spec(wa.shape), _const_spec(wc.shape), _const_spec(wo.shape),
        _const_spec(g.shape), _const_spec(b.shape),
    ]
    return pl.pallas_call(
        functools.partial(_merge_sample_kernel, alpha=alpha),
        out_shape=(jax.ShapeDtypeStruct((m, d), F32), jax.ShapeDtypeStruct((m, dc), F32)),
        grid=(1,),
        in_specs=in_specs,
        out_specs=(pl.BlockSpec((m, d), lambda i: (0, 0)), pl.BlockSpec((m, dc), lambda i: (0, 0))),
        compiler_params=pltpu.CompilerParams(
            dimension_semantics=("arbitrary",), vmem_limit_bytes=VMEM_LIMIT),
        name="merge_sample",
    )(o, conv_in, conv_in, conv_in, s0, s1, gates, gates, x, conv_w, wa, wc, wo, g, b)


def _rope_tables(pos, head_dim, rot_dim):
    half = rot_dim // 2
    inv_freq = jnp.power(ROPE_THETA, -jnp.arange(0, rot_dim, 2, dtype=F32) / rot_dim)
    ang = pos.astype(F32)[:, None] * inv_freq[None, :]
    cos, sin = jnp.cos(ang), jnp.sin(ang)
    n = pos.shape[0]
    ones = jnp.ones((n, head_dim - rot_dim), F32)
    zeros_h = jnp.zeros((n, half), F32)
    zeros_r = jnp.zeros((n, head_dim - rot_dim), F32)
    c = jnp.concatenate([cos, cos, ones], axis=1)
    lo = jnp.concatenate([-sin, zeros_h, zeros_r], axis=1)
    hi = jnp.concatenate([zeros_h, sin, zeros_r], axis=1)
    two = lambda t: jnp.concatenate([t, t], axis=1)
    return two(c), two(lo), two(hi)


def kernel(x_prompt, x_sample, cache_k, cache_v, state_conv, page_table, ln_g, ln_b, ffn1_w_gate, ffn1_w_up, ffn1_w_down, w_in, conv_w, lambda_q1, lambda_k1, lambda_q2, lambda_k2, subln_g, w_attn_out, w_conv_out, w_o, ffn2_w_gate, ffn2_w_up, ffn2_w_down):
    batch, seq, d_model = x_prompt.shape
    dec_batch, dec_seq, _ = x_sample.shape
    assert dec_seq == 1
    depth = ln_g.shape[0]
    page_size, n_heads, v_dim = cache_k.shape[2:]
    head_dim = v_dim // 2
    rot_dim = head_dim // 4
    width = n_heads * v_dim
    d_conv = conv_w.shape[2]
    assert width == d_conv and v_dim == LANES and n_heads == SUBLANES
    past_len = page_table.shape[1] * page_size
    alpha = (2.0 * depth) ** 0.25
    q_scale = head_dim ** -0.5

    tabs_p = _rope_tables(jnp.arange(seq, dtype=jnp.int32), head_dim, rot_dim)
    tabs_s = _rope_tables(jnp.full((dec_batch,), past_len, jnp.int32), head_dim, rot_dim)

    y_p = x_prompt.reshape(batch * seq, d_model)
    y_s = x_sample.reshape(dec_batch, d_model)
    outs = [[] for _ in range(6)]
    for layer in range(depth):
        lam_init = 0.8 - 0.6 * math.exp(-0.3 * layer)
        bf = lambda w: w[layer].astype(BF16)
        wg1, wu1, wd1 = ffn1_w_gate[layer], ffn1_w_up[layer], ffn1_w_down[layer]
        wg2, wu2, wd2 = ffn2_w_gate[layer], ffn2_w_up[layer], ffn2_w_down[layer]
        win, wa, wc, wo = bf(w_in), bf(w_attn_out), bf(w_conv_out), bf(w_o)
        g = [ln_g[layer, i][None, :] for i in range(3)]
        b = [ln_b[layer, i][None, :] for i in range(3)]
        lam_vecs = [v[layer][None, :] for v in (lambda_q1, lambda_k1, lambda_q2, lambda_k2)]
        sg = subln_g[layer][None, :]
        cw = conv_w[layer]

        x1 = _ffn_ln(y_p, wg1, wu1, wd1, g[0], b[0], alpha=alpha, tm=1024, tf=256)
        q, k, v, kb, vb, conv_in, gates = _inproj(x1, win, *tabs_p, tm=512, tn=width, q_scale=q_scale,
                                                  rot_half=rot_dim // 2, n_heads=n_heads)
        s1 = _ffn_ln(y_s, wg1, wu1, wd1, g[0], b[0], alpha=alpha, tm=dec_batch, tf=512)
        qs, ks, vs, _, _, conv_in_s, gates_s = _inproj(s1, win, *tabs_s, tm=dec_batch, tn=width,
                                                       q_scale=q_scale, rot_half=rot_dim // 2, n_heads=n_heads)

        heads = lambda t: t.reshape(dec_batch, n_heads, v_dim)
        os_, o = _attention(page_table, heads(qs), heads(ks), heads(vs), cache_k[layer], cache_v[layer],
                            q, kb, vb, lam_vecs, sg, batch=batch, seq=seq, lam_init=lam_init,
                            pages_per_step=16, tq=256, row_chunks=1)

        x2, tails = _merge_prompt(o, conv_in, gates, x1, cw, wa, wc, wo, g[1], b[1],
                                  alpha=alpha, tm=256, seq=seq)
        y_p = _ffn_ln(x2, wg2, wu2, wd2, g[2], b[2], alpha=alpha, tm=1024, tf=256)
        tails = tails.reshape(batch, seq // 256, SUBLANES, d_conv)
        outs[0].append(k.reshape(batch, seq, n_heads, v_dim))
        outs[1].append(v.reshape(batch, seq, n_heads, v_dim))
        outs[2].append(tails[:, -1, SUBLANES - 2:, :])

        st = state_conv[layer]
        s2, u_s = _merge_sample(os_.reshape(dec_batch, width), conv_in_s, gates_s, st[:, 0], st[:, 1], s1,
                                cw, wa, wc, wo, g[1], b[1], alpha=alpha)
        y_s = _ffn_ln(s2, wg2, wu2, wd2, g[2], b[2], alpha=alpha, tm=dec_batch, tf=512)
        outs[3].append(ks.reshape(dec_batch, 1, n_heads, v_dim))
        outs[4].append(vs.reshape(dec_batch, 1, n_heads, v_dim))
        outs[5].append(jnp.stack([st[:, 1], u_s], axis=1))

    return (y_p.reshape(batch, seq, d_model), y_s.reshape(dec_batch, 1, d_model),
            jnp.stack(outs[0]), jnp.stack(outs[1]), jnp.stack(outs[2]),
            jnp.stack(outs[3]), jnp.stack(outs[4]), jnp.stack(outs[5]))
```

```python
import functools
import math

import jax
import jax.numpy as jnp
from jax import lax
from jax.experimental import pallas as pl
from jax.experimental.pallas import tpu as pltpu

ROPE_THETA = 500000.0
LN_EPS = 1e-5
NEG_INF = -1e30
LANES = 128
SUBLANES = 8
VMEM_LIMIT = 56 * 1024 * 1024
LN_ROWS = 256
SOFTMAX_UNROLL = 4
PAGE_GROUPS = 3
BF16 = jnp.bfloat16
F32 = jnp.float32


def _layer_norm(y, g, b):
    mu = jnp.mean(y, axis=-1, keepdims=True)
    d = y - mu
    var = jnp.mean(d * d, axis=-1, keepdims=True)
    return d * lax.rsqrt(var + LN_EPS) * g + b


def _ffn_ln_kernel(x_ref, wg_ref, wu_ref, wd_ref, g_ref, b_ref, o_ref, *rest, alpha):
    xb_ref = rest[-1]
    ob_ref = rest[0] if len(rest) == 2 else None
    f = pl.program_id(1)
    nf = pl.num_programs(1)

    @pl.when(f == 0)
    def _():
        xb_ref[...] = x_ref[...].astype(BF16)
        o_ref[...] = jnp.zeros(o_ref.shape, F32)

    xb = xb_ref[...]
    hg = jnp.dot(xb, wg_ref[...].astype(BF16), preferred_element_type=F32)
    hu = jnp.dot(xb, wu_ref[...].astype(BF16), preferred_element_type=F32)
    h = (hg * jax.nn.sigmoid(hg) * hu).astype(BF16)
    o_ref[...] += jnp.dot(h, wd_ref[...].astype(BF16), preferred_element_type=F32)

    @pl.when(f == nf - 1)
    def _():
        rows = min(x_ref.shape[0], LN_ROWS)
        for r in range(x_ref.shape[0] // rows):
            sl = slice(r * rows, (r + 1) * rows)
            y = alpha * x_ref[sl, :] + 0.5 * o_ref[sl, :]
            out = _layer_norm(y, g_ref[...], b_ref[...])
            o_ref[sl, :] = out
            if ob_ref is not None:
                ob_ref[sl, :] = out.astype(BF16)


def _ffn_ln(x, wg, wu, wd, g, b, *, alpha, tm, tf, emit_bf16=False):
    m, d = x.shape
    dff = wg.shape[1]
    row_spec = pl.BlockSpec((tm, d), lambda i, f: (i, 0))
    x_spec = pl.BlockSpec((tm, d), lambda i, f: (i, 0), pipeline_mode=pl.Buffered(1)) if emit_bf16 else row_spec
    out_shape = jax.ShapeDtypeStruct((m, d), F32)
    return pl.pallas_call(
        functools.partial(_ffn_ln_kernel, alpha=alpha),
        out_shape=(out_shape, jax.ShapeDtypeStruct((m, d), BF16)) if emit_bf16 else out_shape,
        grid=(m // tm, dff // tf),
        in_specs=[
            x_spec,
            pl.BlockSpec((d, tf), lambda i, f: (0, f)),
            pl.BlockSpec((d, tf), lambda i, f: (0, f)),
            pl.BlockSpec((tf, d), lambda i, f: (f, 0)),
            pl.BlockSpec((1, d), lambda i, f: (0, 0)),
            pl.BlockSpec((1, d), lambda i, f: (0, 0)),
        ],
        out_specs=(row_spec, row_spec) if emit_bf16 else row_spec,
        scratch_shapes=[pltpu.VMEM((tm, d), BF16)],
        compiler_params=pltpu.CompilerParams(
            dimension_semantics=("parallel", "arbitrary"), vmem_limit_bytes=VMEM_LIMIT),
        name="ffn_ln",
    )(x, wg, wu, wd, g, b)


N_QKV_TILES = 3
N_CONV_TILES = 3
def _rope_tile(z, cos, sin_lo, sin_hi, rot_half):
    up = pltpu.roll(z, LANES - rot_half, 1)
    down = pltpu.roll(z, rot_half, 1)
    return z * cos + up * sin_lo + down * sin_hi


def _inproj_kernel(x_ref, w_ref, cos_ref, slo_ref, shi_ref,
                   q_ref, k_ref, v_ref, kb_ref, vb_ref, r_ref, gate_ref,
                   *, q_scale, rot_half, n_heads):
    j = pl.program_id(0)

    @pl.when(j < N_QKV_TILES + N_CONV_TILES)
    def _():
        r_ref[...] = jnp.dot(x_ref[...], w_ref[...], preferred_element_type=F32)

    @pl.when(j >= N_QKV_TILES + N_CONV_TILES)
    def _():
        gate_ref[...] = jnp.dot(x_ref[...], w_ref[...], preferred_element_type=F32).astype(BF16)

    @pl.when(j == 0)
    def _():
        for h in range(n_heads):
            sl = slice(h * LANES, (h + 1) * LANES)
            r = _rope_tile(r_ref[:, sl], cos_ref[...], slo_ref[...], shi_ref[...], rot_half)
            q_ref[:, sl] = (r * q_scale).astype(BF16)

    @pl.when(j == 1)
    def _():
        for h in range(n_heads):
            sl = slice(h * LANES, (h + 1) * LANES)
            r = _rope_tile(r_ref[:, sl], cos_ref[...], slo_ref[...], shi_ref[...], rot_half)
            k_ref[:, sl] = r
            kb_ref[:, sl] = r.astype(BF16)

    @pl.when(j == 2)
    def _():
        v_ref[...] = r_ref[...]
        vb_ref[...] = r_ref[...].astype(BF16)


def _inproj(x, w, cos, slo, shi, *, tm, tn, q_scale, rot_half, n_heads):
    m, d = x.shape
    n = w.shape[1]
    nj = n // tn
    n_first = N_QKV_TILES + N_CONV_TILES
    n_gate_tiles = nj - n_first
    tab_blocks = cos.shape[0] // tm
    last = m // tm - 1
    tab_spec = pl.BlockSpec((tm, LANES), lambda j, i: (i % tab_blocks, 0))

    def rows_while(j, i, first_tile, n_tiles):
        return jnp.where(j < first_tile, 0, jnp.where(j < first_tile + n_tiles, i, last))

    def tile_spec(t):
        return pl.BlockSpec((tm, tn), lambda j, i: (rows_while(j, i, t, 1), 0))

    return pl.pallas_call(
        functools.partial(_inproj_kernel, q_scale=q_scale, rot_half=rot_half, n_heads=n_heads),
        out_shape=(
            jax.ShapeDtypeStruct((m, tn), BF16),
            jax.ShapeDtypeStruct((m, tn), F32),
            jax.ShapeDtypeStruct((m, tn), F32),
            jax.ShapeDtypeStruct((m, tn), BF16),
            jax.ShapeDtypeStruct((m, tn), BF16),
            jax.ShapeDtypeStruct((m, N_CONV_TILES * tn), F32),
            jax.ShapeDtypeStruct((m, n_gate_tiles * tn), BF16),
        ),
        grid=(nj, m // tm),
        in_specs=[
            pl.BlockSpec((tm, d), lambda j, i: (i, 0)),
            pl.BlockSpec((d, tn), lambda j, i: (0, j)),
            tab_spec, tab_spec, tab_spec,
        ],
        out_specs=(tile_spec(0), tile_spec(1), tile_spec(2), tile_spec(1), tile_spec(2),
                   pl.BlockSpec((tm, tn), lambda j, i: (rows_while(j, i, N_QKV_TILES, N_CONV_TILES),
                                                        jnp.clip(j - N_QKV_TILES, 0, N_CONV_TILES - 1))),
                   pl.BlockSpec((tm, tn), lambda j, i: (rows_while(j, i, n_first, n_gate_tiles),
                                                        jnp.maximum(j - n_first, 0)))),
        compiler_params=pltpu.CompilerParams(
            dimension_semantics=("arbitrary", "arbitrary"), vmem_limit_bytes=VMEM_LIMIT),
        name="inproj",
    )(x, w, cos, slo, shi)


def _lambda_value(lq1_ref, lk1_ref, lq2_ref, lk2_ref, lam_init):
    s1 = jnp.sum(lq1_ref[...] * lk1_ref[...], axis=-1, keepdims=True)
    s2 = jnp.sum(lq2_ref[...] * lk2_ref[...], axis=-1, keepdims=True)
    return jnp.exp(s1) - jnp.exp(s2) + lam_init


def _head_rmsnorm(o, g, lam_init):
    ms = jnp.mean(o * o, axis=-1, keepdims=True)
    return o * lax.rsqrt(ms + LN_EPS) * g * (1.0 - lam_init)


def _reduce_keys(x, reduce_fn, combine_fn, groups=8):
    n, w = x.shape
    x3 = x.reshape(n // SUBLANES, SUBLANES, w)
    per = x3.shape[0] // groups
    parts = [reduce_fn(x3[g * per:(g + 1) * per], axis=0) for g in range(groups)]
    while len(parts) > 1:
        parts = [combine_fn(parts[i], parts[i + 1]) for i in range(0, len(parts), 2)]
    return reduce_fn(parts[0], axis=0, keepdims=True)


def _prompt_attn_unit(qi, q_ref, k_ref, v_ref, lam, g_ref, o_ref, vt_ref,
                      *, lam_init, nq, row_chunks):
    tq, vd = q_ref.shape
    head_dim = vd // 2
    rc = tq // row_chunks
    seq = k_ref.shape[0]

    @pl.when(qi == 0)
    def _():
        for c in range(seq // tq):
            cols = slice(c * tq, (c + 1) * tq)
            vt_ref[:, cols] = v_ref[cols, :].astype(F32).T.astype(BF16)

    q = q_ref[...]
    lane = lax.broadcasted_iota(jnp.int32, q.shape, 1)
    zero = jnp.zeros_like(q)
    q_sub = (jnp.where(lane < head_dim, q, zero), jnp.where(lane >= head_dim, q, zero))
    key = lax.broadcasted_iota(jnp.int32, (rc, rc), 0)
    qry = lax.broadcasted_iota(jnp.int32, (rc, rc), 1)
    keep = jnp.concatenate([key <= qry, key <= qry], axis=1)

    def softmax_av(qc, ext):
        s = lax.dot_general(k_ref[0:ext, :], qc, (((1,), (1,)), ((), ())), preferred_element_type=F32)
        diag = jnp.where(keep, s[ext - rc:, :], NEG_INF)
        s = diag if ext == rc else jnp.concatenate([s[:ext - rc, :], diag], axis=0)
        m = _reduce_keys(s, jnp.max, jnp.maximum)
        p = jnp.exp(s - m)
        l = _reduce_keys(p, jnp.sum, jnp.add)
        return jnp.dot(vt_ref[:, 0:ext], p.astype(BF16), preferred_element_type=F32) / l

    for i in range(nq):
        @pl.when(qi == i)
        def _(i=i):
            for r in range(row_chunks):
                rows = slice(r * rc, (r + 1) * rc)
                ext = i * tq + (r + 1) * rc
                o = softmax_av(jnp.concatenate([q_sub[0][rows], q_sub[1][rows]], axis=0), ext)
                o = o[:, :rc] - lam * o[:, rc:]
                o_ref[rows, :] = _head_rmsnorm(o.T, g_ref[...], lam_init).astype(BF16)


def _class_reduce(x, op, n_classes):
    shift = n_classes
    while shift < x.shape[-1]:
        x = op(x, pltpu.roll(x, shift, 1))
        shift *= 2
    return x


def _attn_kernel(pt_ref, q_ref, kn_ref, vn_ref, lq1_ref, lk1_ref, lq2_ref, lk2_ref, g_ref,
                 qp_ref, kp_ref, vp_ref, ck_hbm, cv_hbm, o_ref, op_ref,
                 s_scr, a_scr, qt_scr, anew_scr, acc_scr, vt_scr, page_buf, page_sem,
                 *, lam_init, pages_per_step, n_pages, steps_per_unit, nq, row_chunks):
    pp = pages_per_step
    g = pl.program_id(1)
    ng = n_pages // pp
    steps_per_seq = 2 * ng
    n_steps = pl.num_programs(0) * steps_per_seq
    step = pl.program_id(0) * steps_per_seq + g

    def page_copy(src_hbm, page_id, slot):
        return pltpu.make_async_copy(src_hbm.at[page_id], page_buf.at[slot], page_sem.at[slot])

    def fetch_group(t):
        seq_id = t // steps_per_seq
        j = t % steps_per_seq
        is_k = j < ng
        first = seq_id * n_pages + jnp.where(is_k, j, j - ng) * pp
        slot0 = (t % PAGE_GROUPS) * pp

        @pl.when(is_k)
        def _():
            for p in range(pp):
                page_copy(ck_hbm, pt_ref[first + p], slot0 + p).start()

        @pl.when(jnp.logical_not(is_k))
        def _():
            for p in range(pp):
                page_copy(cv_hbm, pt_ref[first + p], slot0 + p).start()

    @pl.when(step == 0)
    def _():
        for t in range(PAGE_GROUPS - 1):
            fetch_group(jnp.int32(t))

    @pl.when(step + PAGE_GROUPS - 1 < n_steps)
    def _():
        fetch_group(step + PAGE_GROUPS - 1)

    @pl.when(step % steps_per_unit == 0)
    def _():
        lam = _lambda_value(lq1_ref, lk1_ref, lq2_ref, lk2_ref, lam_init)
        _prompt_attn_unit((step // steps_per_unit) % nq, qp_ref, kp_ref, vp_ref, lam, g_ref, op_ref, vt_scr,
                          lam_init=lam_init, nq=nq, row_chunks=row_chunks)

    slot0 = (step % PAGE_GROUPS) * pp
    for p in range(pp):
        page_copy(ck_hbm, 0, slot0 + p).wait()

    _, page, n_heads, vd = page_buf.shape
    rows = page * n_heads
    hd = vd // 2
    nt = (((1,), (1,)), ((), ()))

    def pad_rows(x):
        return jnp.concatenate([x, jnp.zeros((LANES - n_heads, vd), x.dtype)], axis=0)

    @pl.when(g == 0)
    def _():
        q = q_ref[...]
        qq = jnp.concatenate([q, q], axis=0)
        row = lax.broadcasted_iota(jnp.int32, qq.shape, 0)
        lane = lax.broadcasted_iota(jnp.int32, qq.shape, 1)
        qt_scr[...] = jnp.where(row // n_heads == lane // hd, qq, jnp.zeros_like(qq))

    @pl.when(g < ng)
    def _():
        for p in range(pp):
            kb = page_buf[slot0 + p].reshape(rows, vd).astype(BF16)
            s_scr[g * pp + p] = lax.dot_general(qt_scr[...], kb, nt, preferred_element_type=F32)

    @pl.when(g == ng)
    def _():
        lane1 = lax.broadcasted_iota(jnp.int32, (2 * n_heads, LANES), 1)
        s_new = lax.dot_general(qt_scr[...], pad_rows(kn_ref[...]).astype(BF16), nt,
                                preferred_element_type=F32)
        s_new = jnp.where(lane1 < n_heads, s_new, NEG_INF)

        def with_new(x, x_new, op):
            return jnp.concatenate([op(x[:, :LANES], x_new), x[:, LANES:]], axis=1)

        m = lax.fori_loop(0, n_pages, lambda i, m: jnp.maximum(m, s_scr[i]),
                          jnp.full((2 * n_heads, rows), NEG_INF, F32), unroll=SOFTMAX_UNROLL)
        m = _class_reduce(with_new(m, s_new, jnp.maximum), jnp.maximum, n_heads)

        def sum_body(i, l):
            p = jnp.exp(s_scr[i] - m)
            s_scr[i] = p
            return l + p
        l = lax.fori_loop(0, n_pages, sum_body, jnp.zeros((2 * n_heads, rows), F32), unroll=SOFTMAX_UNROLL)
        p_new = jnp.exp(s_new - m[:, :LANES])
        l = _class_reduce(with_new(l, p_new, jnp.add), jnp.add, n_heads)

        lam = _lambda_value(lq1_ref, lk1_ref, lq2_ref, lk2_ref, lam_init)
        row = lax.broadcasted_iota(jnp.int32, (n_heads, rows), 0)
        lane = lax.broadcasted_iota(jnp.int32, (n_heads, rows), 1)
        own = row == lane % n_heads

        def a_body(i, c):
            pn = s_scr[i] / l
            a_scr[i] = jnp.where(own, pn[:n_heads] - lam * pn[n_heads:], 0.0)
            return c
        lax.fori_loop(0, n_pages, a_body, 0, unroll=SOFTMAX_UNROLL)
        pn_new = p_new / l[:, :LANES]
        anew_scr[...] = jnp.where(own[:, :LANES], pn_new[:n_heads] - lam * pn_new[n_heads:], 0.0)
        acc_scr[...] = jnp.zeros(acc_scr.shape, F32)

    @pl.when(g >= ng)
    def _():
        acc = acc_scr[...]
        for p in range(pp):
            vb = page_buf[slot0 + p].reshape(rows, vd).astype(BF16)
            acc = acc + jnp.dot(a_scr[(g - ng) * pp + p].astype(BF16), vb, preferred_element_type=F32)
        acc_scr[...] = acc

    @pl.when(g == 2 * ng - 1)
    def _():
        o = acc_scr[...] + jnp.dot(anew_scr[...].astype(BF16), pad_rows(vn_ref[...]).astype(BF16),
                                   preferred_element_type=F32)
        o_ref[...] = _head_rmsnorm(o, g_ref[...], lam_init).astype(o_ref.dtype)


def _attention(page_table, q, k_new, v_new, cache_k, cache_v, qp, kp, vp, lam_vecs, subln_g,
               *, batch, seq, lam_init, pages_per_step, tq, row_chunks):
    nb = q.shape[0]
    n_pages = page_table.shape[1]
    pp = pages_per_step
    ng = n_pages // pp
    _, page, n_heads, vd = cache_k.shape
    rows = page * n_heads
    pt_flat = page_table.reshape(-1)
    nq = seq // tq
    n_units = batch * n_heads * nq
    n_steps = nb * 2 * ng
    steps_per_unit = n_steps // n_units
    assert steps_per_unit * n_units == n_steps

    def unit(b, g):
        u = (b * (2 * ng) + g) // steps_per_unit
        return u // (n_heads * nq), (u // nq) % n_heads, u % nq

    def qp_map(b, g, pt):
        pb, h, i = unit(b, g)
        return pb * nq + i, h

    def kvp_map(b, g, pt):
        pb, h, _ = unit(b, g)
        return pb, h

    head_spec = pl.BlockSpec((None, n_heads, vd), lambda b, g, pt: (b, 0, 0))
    vec = pl.BlockSpec((1, lam_vecs[0].shape[1]), lambda b, g, pt: (0, 0))

    grid_spec = pltpu.PrefetchScalarGridSpec(
        num_scalar_prefetch=1,
        grid=(nb, 2 * ng),
        in_specs=[head_spec, head_spec, head_spec, vec, vec, vec, vec,
                  pl.BlockSpec((1, vd), lambda b, g, pt: (0, 0)),
                  pl.BlockSpec((tq, vd), qp_map),
                  pl.BlockSpec((seq, vd), kvp_map),
                  pl.BlockSpec((seq, vd), kvp_map),
                  pl.BlockSpec(memory_space=pl.ANY),
                  pl.BlockSpec(memory_space=pl.ANY)],
        out_specs=(head_spec, pl.BlockSpec((tq, vd), qp_map)),
        scratch_shapes=[
            pltpu.VMEM((n_pages, 2 * n_heads, rows), F32),
            pltpu.VMEM((n_pages, n_heads, rows), F32),
            pltpu.VMEM((2 * n_heads, vd), BF16),
            pltpu.VMEM((n_heads, LANES), F32),
            pltpu.VMEM((n_heads, vd), F32),
            pltpu.VMEM((vd, seq), BF16),
            pltpu.VMEM((PAGE_GROUPS * pp, page, n_heads, vd), F32),
            pltpu.SemaphoreType.DMA((PAGE_GROUPS * pp,)),
        ],
    )
    return pl.pallas_call(
        functools.partial(_attn_kernel, lam_init=lam_init, pages_per_step=pp, n_pages=n_pages,
                          steps_per_unit=steps_per_unit, nq=nq, row_chunks=row_chunks),
        out_shape=(jax.ShapeDtypeStruct((nb, n_heads, vd), BF16),
                   jax.ShapeDtypeStruct(qp.shape, BF16)),
        grid_spec=grid_spec,
        compiler_params=pltpu.CompilerParams(
            dimension_semantics=("arbitrary", "arbitrary"), vmem_limit_bytes=VMEM_LIMIT),
        name="attention",
    )(pt_flat, q, k_new, v_new, *lam_vecs, subln_g, qp, kp, vp, cache_k, cache_v)


def _mixer_tail(o, cb, conv, ga, gc, x, wa_ref, wc_ref, wo_ref, g_ref, b_ref, alpha):
    ga = ga.astype(F32)
    gc = gc.astype(F32)
    y_attn = jnp.dot(o, wa_ref[...], preferred_element_type=F32)
    y_conv = jnp.dot((cb * conv).astype(BF16), wc_ref[...], preferred_element_type=F32)
    merged = jax.nn.sigmoid(ga) * y_attn + jax.nn.sigmoid(gc) * y_conv
    z = jnp.dot(merged.astype(BF16), wo_ref[...], preferred_element_type=F32)
    return _layer_norm(alpha * x + z, g_ref[...], b_ref[...])


def _merge_prompt_kernel(o_ref, cb_ref, cc_ref, ch_ref, pc_ref, ph_ref, ga_ref, gc_ref, x_ref,
                         cw_ref, wa_ref, wc_ref, wo_ref, g_ref, b_ref,
                         out_ref, tail_ref, u_scr, *, alpha, tiles_per_seq):
    i = pl.program_id(0)
    tm = cc_ref.shape[0]
    u = cc_ref[...] * ch_ref[...]
    prev = pc_ref[...] * ph_ref[...]
    prev = jnp.where(i % tiles_per_seq == 0, jnp.zeros_like(prev), prev)
    u_scr[0:SUBLANES, :] = prev
    u_scr[SUBLANES:SUBLANES + tm, :] = u
    u1 = u_scr[SUBLANES - 1:SUBLANES - 1 + tm, :]
    u2 = u_scr[SUBLANES - 2:SUBLANES - 2 + tm, :]
    conv = cw_ref[0:1, :] * u2 + cw_ref[1:2, :] * u1 + cw_ref[2:3, :] * u
    tail_ref[...] = u[tm - SUBLANES:tm, :]
    out_ref[...] = _mixer_tail(o_ref[...], cb_ref[...], conv, ga_ref[...], gc_ref[...],
                               x_ref[...], wa_ref, wc_ref, wo_ref, g_ref, b_ref, alpha)


def _merge_sample_kernel(o_ref, cb_ref, cc_ref, ch_ref, s0_ref, s1_ref, ga_ref, gc_ref, x_ref,
                         cw_ref, wa_ref, wc_ref, wo_ref, g_ref, b_ref,
                         out_ref, u_ref, *, alpha):
    u = cc_ref[...] * ch_ref[...]
    conv = cw_ref[0:1, :] * s0_ref[...] + cw_ref[1:2, :] * s1_ref[...] + cw_ref[2:3, :] * u
    u_ref[...] = u
    out_ref[...] = _mixer_tail(o_ref[...], cb_ref[...], conv, ga_ref[...], gc_ref[...],
                               x_ref[...], wa_ref, wc_ref, wo_ref, g_ref, b_ref, alpha)


def _const_spec(shape):
    return pl.BlockSpec(shape, lambda i: (0,) * len(shape), pipeline_mode=pl.Buffered(1))


def _merge_prompt(o, conv_in, gates, x, conv_w, wa, wc, wo, g, b, *, alpha, tm, seq):
    m, d = x.shape
    dc = o.shape[1]
    rows8 = tm // SUBLANES
    in_specs = [
        pl.BlockSpec((tm, dc), lambda i: (i, 0)),
        pl.BlockSpec((tm, dc), lambda i: (i, 0)),
        pl.BlockSpec((tm, dc), lambda i: (i, 1)),
        pl.BlockSpec((tm, dc), lambda i: (i, 2)),
        pl.BlockSpec((SUBLANES, dc), lambda i: (jnp.maximum(i * rows8 - 1, 0), 1)),
        pl.BlockSpec((SUBLANES, dc), lambda i: (jnp.maximum(i * rows8 - 1, 0), 2)),
        pl.BlockSpec((tm, d), lambda i: (i, 0)),
        pl.BlockSpec((tm, d), lambda i: (i, 1)),
        pl.BlockSpec((tm, d), lambda i: (i, 0)),
        _const_spec(conv_w.shape), _const_spec(wa.shape), _const_spec(wc.shape), _const_spec(wo.shape),
        _const_spec(g.shape), _const_spec(b.shape),
    ]
    return pl.pallas_call(
        functools.partial(_merge_prompt_kernel, alpha=alpha, tiles_per_seq=seq // tm),
        out_shape=(jax.ShapeDtypeStruct((m, d), F32),
                   jax.ShapeDtypeStruct((m // tm * SUBLANES, dc), F32)),
        grid=(m // tm,),
        in_specs=in_specs,
        out_specs=(pl.BlockSpec((tm, d), lambda i: (i, 0)),
                   pl.BlockSpec((SUBLANES, dc), lambda i: (i, 0))),
        scratch_shapes=[pltpu.VMEM((tm + SUBLANES, dc), F32)],
        compiler_params=pltpu.CompilerParams(
            dimension_semantics=("parallel",), vmem_limit_bytes=VMEM_LIMIT),
        name="merge_prompt",
    )(o, conv_in, conv_in, conv_in, conv_in, conv_in, gates, gates, x, conv_w, wa, wc, wo, g, b)


def _merge_sample(o, conv_in, gates, s0, s1, x, conv_w, wa, wc, wo, g, b, *, alpha):
    m, d = x.shape
    dc = o.shape[1]
    in_specs = [
        pl.BlockSpec((m, dc), lambda i: (0, 0)),
        pl.BlockSpec((m, dc), lambda i: (0, 0)),
        pl.BlockSpec((m, dc), lambda i: (0, 1)),
        pl.BlockSpec((m, dc), lambda i: (0, 2)),
        pl.BlockSpec((m, dc), lambda i: (0, 0)),
        pl.BlockSpec((m, dc), lambda i: (0, 0)),
        pl.BlockSpec((m, d), lambda i: (0, 0)),
        pl.BlockSpec((m, d), lambda i: (0, 1)),
        pl.BlockSpec((m, d), lambda i: (0, 0)),
        _const_spec(conv_w.shape), _const_spec(wa.shape), _const_spec(wc.shape), _const_spec(wo.shape),
        _const_spec(g.shape), _const_spec(b.shape),
    ]
    return pl.pallas_call(
        functools.partial(_merge_sample_kernel, alpha=alpha),
        out_shape=(jax.ShapeDtypeStruct((m, d), F32), jax.ShapeDtypeStruct((m, dc), F32)),
        grid=(1,),
        in_specs=in_specs,
        out_specs=(pl.BlockSpec((m, d), lambda i: (0, 0)), pl.BlockSpec((m, dc), lambda i: (0, 0))),
        compiler_params=pltpu.CompilerParams(
            dimension_semantics=("arbitrary",), vmem_limit_bytes=VMEM_LIMIT),
        name="merge_sample",
    )(o, conv_in, conv_in, conv_in, s0, s1, gates, gates, x, conv_w, wa, wc, wo, g, b)


def _rope_tables(pos, head_dim, rot_dim):
    half = rot_dim // 2
    inv_freq = jnp.power(ROPE_THETA, -jnp.arange(0, rot_dim, 2, dtype=F32) / rot_dim)
    ang = pos.astype(F32)[:, None] * inv_freq[None, :]
    cos, sin = jnp.cos(ang), jnp.sin(ang)
    n = pos.shape[0]
    ones = jnp.ones((n, head_dim - rot_dim), F32)
    zeros_h = jnp.zeros((n, half), F32)
    zeros_r = jnp.zeros((n, head_dim - rot_dim), F32)
    c = jnp.concatenate([cos, cos, ones], axis=1)
    lo = jnp.concatenate([-sin, zeros_h, zeros_r], axis=1)
    hi = jnp.concatenate([zeros_h, sin, zeros_r], axis=1)
    two = lambda t: jnp.concatenate([t, t], axis=1)
    return two(c), two(lo), two(hi)


def kernel(x_prompt, x_sample, cache_k, cache_v, state_conv, page_table, ln_g, ln_b, ffn1_w_gate, ffn1_w_up, ffn1_w_down, w_in, conv_w, lambda_q1, lambda_k1, lambda_q2, lambda_k2, subln_g, w_attn_out, w_conv_out, w_o, ffn2_w_gate, ffn2_w_up, ffn2_w_down):
    batch, seq, d_model = x_prompt.shape
    dec_batch, dec_seq, _ = x_sample.shape
    assert dec_seq == 1
    depth = ln_g.shape[0]
    page_size, n_heads, v_dim = cache_k.shape[2:]
    head_dim = v_dim // 2
    rot_dim = head_dim // 4
    width = n_heads * v_dim
    d_conv = conv_w.shape[2]
    assert width == d_conv and v_dim == LANES and n_heads == SUBLANES
    past_len = page_table.shape[1] * page_size
    alpha = (2.0 * depth) ** 0.25
    q_scale = head_dim ** -0.5

    tabs_p = _rope_tables(jnp.arange(seq, dtype=jnp.int32), head_dim, rot_dim)
    tabs_s = _rope_tables(jnp.full((dec_batch,), past_len, jnp.int32), head_dim, rot_dim)

    y_p = x_prompt.reshape(batch * seq, d_model)
    y_s = x_sample.reshape(dec_batch, d_model)
    outs = [[] for _ in range(6)]
    for layer in range(depth):
        lam_init = 0.8 - 0.6 * math.exp(-0.3 * layer)
        bf = lambda w: w[layer].astype(BF16)
        wg1, wu1, wd1 = ffn1_w_gate[layer], ffn1_w_up[layer], ffn1_w_down[layer]
        wg2, wu2, wd2 = ffn2_w_gate[layer], ffn2_w_up[layer], ffn2_w_down[layer]
        win, wa, wc, wo = bf(w_in), bf(w_attn_out), bf(w_conv_out), bf(w_o)
        g = [ln_g[layer, i][None, :] for i in range(3)]
        b = [ln_b[layer, i][None, :] for i in range(3)]
        lam_vecs = [v[layer][None, :] for v in (lambda_q1, lambda_k1, lambda_q2, lambda_k2)]
        sg = subln_g[layer][None, :]
        cw = conv_w[layer]

        x1, x1b = _ffn_ln(y_p, wg1, wu1, wd1, g[0], b[0], alpha=alpha, tm=1024, tf=256, emit_bf16=True)
        q, k, v, kb, vb, conv_in, gates = _inproj(x1b, win, *tabs_p, tm=512, tn=width, q_scale=q_scale,
                                                  rot_half=rot_dim // 2, n_heads=n_heads)
        s1, s1b = _ffn_ln(y_s, wg1, wu1, wd1, g[0], b[0], alpha=alpha, tm=dec_batch, tf=512, emit_bf16=True)
        qs, ks, vs, _, _, conv_in_s, gates_s = _inproj(s1b, win, *tabs_s, tm=dec_batch, tn=width,
                                                       q_scale=q_scale, rot_half=rot_dim // 2, n_heads=n_heads)

        heads = lambda t: t.reshape(dec_batch, n_heads, v_dim)
        os_, o = _attention(page_table, heads(qs), heads(ks), heads(vs), cache_k[layer], cache_v[layer],
                            q, kb, vb, lam_vecs, sg, batch=batch, seq=seq, lam_init=lam_init,
                            pages_per_step=16, tq=256, row_chunks=1)

        x2, tails = _merge_prompt(o, conv_in, gates, x1, cw, wa, wc, wo, g[1], b[1],
                                  alpha=alpha, tm=256, seq=seq)
        y_p = _ffn_ln(x2, wg2, wu2, wd2, g[2], b[2], alpha=alpha, tm=1024, tf=256)
        tails = tails.reshape(batch, seq // 256, SUBLANES, d_conv)
        outs[0].append(k.reshape(batch, seq, n_heads, v_dim))
        outs[1].append(v.reshape(batch, seq, n_heads, v_dim))
        outs[2].append(tails[:, -1, SUBLANES - 2:, :])

        st = state_conv[layer]
        s2, u_s = _merge_sample(os_.reshape(dec_batch, width), conv_in_s, gates_s, st[:, 0], st[:, 1], s1,
                                cw, wa, wc, wo, g[1], b[1], alpha=alpha)
        y_s = _ffn_ln(s2, wg2, wu2, wd2, g[2], b[2], alpha=alpha, tm=dec_batch, tf=512)
        outs[3].append(ks.reshape(dec_batch, 1, n_heads, v_dim))
        outs[4].append(vs.reshape(dec_batch, 1, n_heads, v_dim))
        outs[5].append(jnp.stack([st[:, 1], u_s], axis=1))

    return (y_p.reshape(batch, seq, d_model), y_s.reshape(dec_batch, 1, d_model),
            jnp.stack(outs[0]), jnp.stack(outs[1]), jnp.stack(outs[2]),
            jnp.stack(outs[3]), jnp.stack(outs[4]), jnp.stack(outs[5]))
```

```python
import functools
import math

import jax
import jax.numpy as jnp
from jax import lax
from jax.experimental import pallas as pl
from jax.experimental.pallas import tpu as pltpu

ROPE_THETA = 500000.0
LN_EPS = 1e-5
NEG_INF = -1e30
LANES = 128
SUBLANES = 8
VMEM_LIMIT = 56 * 1024 * 1024
LN_ROWS = 256
SOFTMAX_UNROLL = 4
PAGE_GROUPS = 4
BF16 = jnp.bfloat16
F32 = jnp.float32


def _layer_norm(y, g, b):
    mu = jnp.mean(y, axis=-1, keepdims=True)
    d = y - mu
    var = jnp.mean(d * d, axis=-1, keepdims=True)
    return d * lax.rsqrt(var + LN_EPS) * g + b


def _ffn_ln_kernel(x_ref, xs_ref, wg_ref, wu_ref, wd_ref, g_ref, b_ref, o_ref, os_ref, xb_ref, *, alpha):
    i = pl.program_id(0)
    f = pl.program_id(1)
    nf = pl.num_programs(1)
    tm = x_ref.shape[0]
    first_tile = i == 0

    @pl.when(f == 0)
    def _():
        xb_ref[0:tm, :] = x_ref[...].astype(BF16)
        o_ref[...] = jnp.zeros(o_ref.shape, F32)

    @pl.when(jnp.logical_and(first_tile, f == 0))
    def _():
        xb_ref[tm:, :] = xs_ref[...].astype(BF16)
        os_ref[...] = jnp.zeros(os_ref.shape, F32)

    xb = xb_ref[...]
    hg = jnp.dot(xb, wg_ref[...].astype(BF16), preferred_element_type=F32)
    hu = jnp.dot(xb, wu_ref[...].astype(BF16), preferred_element_type=F32)
    h = (hg * jax.nn.sigmoid(hg) * hu).astype(BF16)
    part = jnp.dot(h, wd_ref[...].astype(BF16), preferred_element_type=F32)
    o_ref[...] += part[:tm]

    @pl.when(first_tile)
    def _():
        os_ref[...] += part[tm:]

    @pl.when(f == nf - 1)
    def _():
        rows = min(tm, LN_ROWS)
        for r in range(tm // rows):
            sl = slice(r * rows, (r + 1) * rows)
            y = alpha * x_ref[sl, :] + 0.5 * o_ref[sl, :]
            o_ref[sl, :] = _layer_norm(y, g_ref[...], b_ref[...])

    @pl.when(jnp.logical_and(first_tile, f == nf - 1))
    def _():
        y = alpha * xs_ref[...] + 0.5 * os_ref[...]
        os_ref[...] = _layer_norm(y, g_ref[...], b_ref[...])


def _ffn_ln(x, xs, wg, wu, wd, g, b, *, alpha, tm, tf):
    m, d = x.shape
    ms = xs.shape[0]
    dff = wg.shape[1]
    return pl.pallas_call(
        functools.partial(_ffn_ln_kernel, alpha=alpha),
        out_shape=(jax.ShapeDtypeStruct((m, d), F32), jax.ShapeDtypeStruct((ms, d), F32)),
        grid=(m // tm, dff // tf),
        in_specs=[
            pl.BlockSpec((tm, d), lambda i, f: (i, 0)),
            pl.BlockSpec((ms, d), lambda i, f: (0, 0)),
            pl.BlockSpec((d, tf), lambda i, f: (0, f)),
            pl.BlockSpec((d, tf), lambda i, f: (0, f)),
            pl.BlockSpec((tf, d), lambda i, f: (f, 0)),
            pl.BlockSpec((1, d), lambda i, f: (0, 0)),
            pl.BlockSpec((1, d), lambda i, f: (0, 0)),
        ],
        out_specs=(pl.BlockSpec((tm, d), lambda i, f: (i, 0)),
                   pl.BlockSpec((ms, d), lambda i, f: (0, 0))),
        scratch_shapes=[pltpu.VMEM((tm + ms, d), BF16)],
        compiler_params=pltpu.CompilerParams(
            dimension_semantics=("arbitrary", "arbitrary"), vmem_limit_bytes=VMEM_LIMIT),
        name="ffn_ln",
    )(x, xs, wg, wu, wd, g, b)


N_QKV_TILES = 3
N_CONV_TILES = 3
def _rope_tile(z, cos, sin_lo, sin_hi, rot_half):
    up = pltpu.roll(z, LANES - rot_half, 1)
    down = pltpu.roll(z, rot_half, 1)
    return z * cos + up * sin_lo + down * sin_hi


def _inproj_kernel(x_ref, w_ref, cos_ref, slo_ref, shi_ref,
                   q_ref, k_ref, v_ref, kb_ref, vb_ref, r_ref, gate_ref, wb_ref,
                   *, q_scale, rot_half, n_heads):
    j = pl.program_id(0)

    @pl.when(pl.program_id(1) == 0)
    def _():
        wb_ref[...] = w_ref[...].astype(BF16)

    xb = x_ref[...].astype(BF16)

    @pl.when(j < N_QKV_TILES + N_CONV_TILES)
    def _():
        r_ref[...] = jnp.dot(xb, wb_ref[...], preferred_element_type=F32)

    @pl.when(j >= N_QKV_TILES + N_CONV_TILES)
    def _():
        gate_ref[...] = jnp.dot(xb, wb_ref[...], preferred_element_type=F32).astype(BF16)

    @pl.when(j == 0)
    def _():
        for h in range(n_heads):
            sl = slice(h * LANES, (h + 1) * LANES)
            r = _rope_tile(r_ref[:, sl], cos_ref[...], slo_ref[...], shi_ref[...], rot_half)
            q_ref[:, sl] = (r * q_scale).astype(BF16)

    @pl.when(j == 1)
    def _():
        for h in range(n_heads):
            sl = slice(h * LANES, (h + 1) * LANES)
            r = _rope_tile(r_ref[:, sl], cos_ref[...], slo_ref[...], shi_ref[...], rot_half)
            k_ref[:, sl] = r
            kb_ref[:, sl] = r.astype(BF16)

    @pl.when(j == 2)
    def _():
        v_ref[...] = r_ref[...]
        vb_ref[...] = r_ref[...].astype(BF16)


def _inproj(x, w, cos, slo, shi, *, tm, tn, q_scale, rot_half, n_heads):
    m, d = x.shape
    n = w.shape[1]
    nj = n // tn
    n_first = N_QKV_TILES + N_CONV_TILES
    n_gate_tiles = nj - n_first
    tab_blocks = cos.shape[0] // tm
    last = m // tm - 1
    tab_spec = pl.BlockSpec((tm, LANES), lambda j, i: (i % tab_blocks, 0))

    def rows_while(j, i, first_tile, n_tiles):
        return jnp.where(j < first_tile, 0, jnp.where(j < first_tile + n_tiles, i, last))

    def tile_spec(t):
        return pl.BlockSpec((tm, tn), lambda j, i: (rows_while(j, i, t, 1), 0))

    return pl.pallas_call(
        functools.partial(_inproj_kernel, q_scale=q_scale, rot_half=rot_half, n_heads=n_heads),
        out_shape=(
            jax.ShapeDtypeStruct((m, tn), BF16),
            jax.ShapeDtypeStruct((m, tn), F32),
            jax.ShapeDtypeStruct((m, tn), F32),
            jax.ShapeDtypeStruct((m, tn), BF16),
            jax.ShapeDtypeStruct((m, tn), BF16),
            jax.ShapeDtypeStruct((m, N_CONV_TILES * tn), F32),
            jax.ShapeDtypeStruct((m, n_gate_tiles * tn), BF16),
        ),
        grid=(nj, m // tm),
        in_specs=[
            pl.BlockSpec((tm, d), lambda j, i: (i, 0)),
            pl.BlockSpec((d, tn), lambda j, i: (0, j)),
            tab_spec, tab_spec, tab_spec,
        ],
        out_specs=(tile_spec(0), tile_spec(1), tile_spec(2), tile_spec(1), tile_spec(2),
                   pl.BlockSpec((tm, tn), lambda j, i: (rows_while(j, i, N_QKV_TILES, N_CONV_TILES),
                                                        jnp.clip(j - N_QKV_TILES, 0, N_CONV_TILES - 1))),
                   pl.BlockSpec((tm, tn), lambda j, i: (rows_while(j, i, n_first, n_gate_tiles),
                                                        jnp.maximum(j - n_first, 0)))),
        scratch_shapes=[pltpu.VMEM((d, tn), BF16)],
        compiler_params=pltpu.CompilerParams(
            dimension_semantics=("arbitrary", "arbitrary"), vmem_limit_bytes=VMEM_LIMIT),
        name="inproj",
    )(x, w, cos, slo, shi)


def _lambda_value(lq1_ref, lk1_ref, lq2_ref, lk2_ref, lam_init):
    s1 = jnp.sum(lq1_ref[...] * lk1_ref[...], axis=-1, keepdims=True)
    s2 = jnp.sum(lq2_ref[...] * lk2_ref[...], axis=-1, keepdims=True)
    return jnp.exp(s1) - jnp.exp(s2) + lam_init


def _head_rmsnorm(o, g, lam_init):
    ms = jnp.mean(o * o, axis=-1, keepdims=True)
    return o * lax.rsqrt(ms + LN_EPS) * g * (1.0 - lam_init)


def _reduce_keys(x, reduce_fn, combine_fn, groups=8):
    n, w = x.shape
    x3 = x.reshape(n // SUBLANES, SUBLANES, w)
    per = x3.shape[0] // groups
    parts = [reduce_fn(x3[g * per:(g + 1) * per], axis=0) for g in range(groups)]
    while len(parts) > 1:
        parts = [combine_fn(parts[i], parts[i + 1]) for i in range(0, len(parts), 2)]
    return reduce_fn(parts[0], axis=0, keepdims=True)


def _prompt_attn_unit(qi, q_ref, k_ref, v_ref, lam, g_ref, o_ref, vt_ref,
                      *, lam_init, nq, row_chunks):
    tq, vd = q_ref.shape
    head_dim = vd // 2
    rc = tq // row_chunks
    seq = k_ref.shape[0]

    @pl.when(qi == 0)
    def _():
        for c in range(seq // tq):
            cols = slice(c * tq, (c + 1) * tq)
            vt_ref[:, cols] = v_ref[cols, :].astype(F32).T.astype(BF16)

    q = q_ref[...]
    lane = lax.broadcasted_iota(jnp.int32, q.shape, 1)
    zero = jnp.zeros_like(q)
    q_sub = (jnp.where(lane < head_dim, q, zero), jnp.where(lane >= head_dim, q, zero))
    key = lax.broadcasted_iota(jnp.int32, (rc, rc), 0)
    qry = lax.broadcasted_iota(jnp.int32, (rc, rc), 1)
    keep = jnp.concatenate([key <= qry, key <= qry], axis=1)

    def softmax_av(qc, ext):
        s = lax.dot_general(k_ref[0:ext, :], qc, (((1,), (1,)), ((), ())), preferred_element_type=F32)
        diag = jnp.where(keep, s[ext - rc:, :], NEG_INF)
        s = diag if ext == rc else jnp.concatenate([s[:ext - rc, :], diag], axis=0)
        m = _reduce_keys(s, jnp.max, jnp.maximum)
        p = jnp.exp(s - m)
        l = _reduce_keys(p, jnp.sum, jnp.add)
        return jnp.dot(vt_ref[:, 0:ext], p.astype(BF16), preferred_element_type=F32) / l

    for i in range(nq):
        @pl.when(qi == i)
        def _(i=i):
            for r in range(row_chunks):
                rows = slice(r * rc, (r + 1) * rc)
                ext = i * tq + (r + 1) * rc
                o = softmax_av(jnp.concatenate([q_sub[0][rows], q_sub[1][rows]], axis=0), ext)
                o = o[:, :rc] - lam * o[:, rc:]
                o_ref[rows, :] = _head_rmsnorm(o.T, g_ref[...], lam_init).astype(BF16)


def _class_reduce(x, op, n_classes):
    shift = n_classes
    while shift < x.shape[-1]:
        x = op(x, pltpu.roll(x, shift, 1))
        shift *= 2
    return x


def _attn_kernel(pt_ref, q_ref, kn_ref, vn_ref, lq1_ref, lk1_ref, lq2_ref, lk2_ref, g_ref,
                 qp_ref, kp_ref, vp_ref, ck_hbm, cv_hbm, o_ref, op_ref,
                 s_scr, a_scr, qt_scr, anew_scr, acc_scr, vt_scr, page_buf, page_sem,
                 *, lam_init, pages_per_step, n_pages, steps_per_unit, nq, row_chunks):
    pp = pages_per_step
    g = pl.program_id(1)
    ng = n_pages // pp
    steps_per_seq = 2 * ng
    n_steps = pl.num_programs(0) * steps_per_seq
    step = pl.program_id(0) * steps_per_seq + g

    def page_copy(src_hbm, page_id, slot):
        return pltpu.make_async_copy(src_hbm.at[page_id], page_buf.at[slot], page_sem.at[slot])

    def fetch_group(t):
        seq_id = t // steps_per_seq
        j = t % steps_per_seq
        is_k = j < ng
        first = seq_id * n_pages + jnp.where(is_k, j, j - ng) * pp
        slot0 = (t % PAGE_GROUPS) * pp

        @pl.when(is_k)
        def _():
            for p in range(pp):
                page_copy(ck_hbm, pt_ref[first + p], slot0 + p).start()

        @pl.when(jnp.logical_not(is_k))
        def _():
            for p in range(pp):
                page_copy(cv_hbm, pt_ref[first + p], slot0 + p).start()

    @pl.when(step == 0)
    def _():
        for t in range(PAGE_GROUPS - 1):
            fetch_group(jnp.int32(t))

    @pl.when(step + PAGE_GROUPS - 1 < n_steps)
    def _():
        fetch_group(step + PAGE_GROUPS - 1)

    @pl.when(step % steps_per_unit == 0)
    def _():
        lam = _lambda_value(lq1_ref, lk1_ref, lq2_ref, lk2_ref, lam_init)
        _prompt_attn_unit((step // steps_per_unit) % nq, qp_ref, kp_ref, vp_ref, lam, g_ref, op_ref, vt_scr,
                          lam_init=lam_init, nq=nq, row_chunks=row_chunks)

    slot0 = (step % PAGE_GROUPS) * pp
    for p in range(pp):
        page_copy(ck_hbm, 0, slot0 + p).wait()

    _, page, n_heads, vd = page_buf.shape
    rows = page * n_heads
    hd = vd // 2
    nt = (((1,), (1,)), ((), ()))

    def pad_rows(x):
        return jnp.concatenate([x, jnp.zeros((LANES - n_heads, vd), x.dtype)], axis=0)

    @pl.when(g == 0)
    def _():
        q = q_ref[...]
        qq = jnp.concatenate([q, q], axis=0)
        row = lax.broadcasted_iota(jnp.int32, qq.shape, 0)
        lane = lax.broadcasted_iota(jnp.int32, qq.shape, 1)
        qt_scr[...] = jnp.where(row // n_heads == lane // hd, qq, jnp.zeros_like(qq))

    @pl.when(g < ng)
    def _():
        for p in range(pp):
            kb = page_buf[slot0 + p].reshape(rows, vd).astype(BF16)
            s_scr[g * pp + p] = lax.dot_general(qt_scr[...], kb, nt, preferred_element_type=F32)

    @pl.when(g == ng)
    def _():
        lane1 = lax.broadcasted_iota(jnp.int32, (2 * n_heads, LANES), 1)
        s_new = lax.dot_general(qt_scr[...], pad_rows(kn_ref[...]).astype(BF16), nt,
                                preferred_element_type=F32)
        s_new = jnp.where(lane1 < n_heads, s_new, NEG_INF)

        def with_new(x, x_new, op):
            return jnp.concatenate([op(x[:, :LANES], x_new), x[:, LANES:]], axis=1)

        m = lax.fori_loop(0, n_pages, lambda i, m: jnp.maximum(m, s_scr[i]),
                          jnp.full((2 * n_heads, rows), NEG_INF, F32), unroll=SOFTMAX_UNROLL)
        m = _class_reduce(with_new(m, s_new, jnp.maximum), jnp.maximum, n_heads)

        def sum_body(i, l):
            p = jnp.exp(s_scr[i] - m)
            s_scr[i] = p
            return l + p
        l = lax.fori_loop(0, n_pages, sum_body, jnp.zeros((2 * n_heads, rows), F32), unroll=SOFTMAX_UNROLL)
        p_new = jnp.exp(s_new - m[:, :LANES])
        l = _class_reduce(with_new(l, p_new, jnp.add), jnp.add, n_heads)

        lam = _lambda_value(lq1_ref, lk1_ref, lq2_ref, lk2_ref, lam_init)
        row = lax.broadcasted_iota(jnp.int32, (n_heads, rows), 0)
        lane = lax.broadcasted_iota(jnp.int32, (n_heads, rows), 1)
        own = row == lane % n_heads

        def a_body(i, c):
            pn = s_scr[i] / l
            a_scr[i] = jnp.where(own, pn[:n_heads] - lam * pn[n_heads:], 0.0)
            return c
        lax.fori_loop(0, n_pages, a_body, 0, unroll=SOFTMAX_UNROLL)
        pn_new = p_new / l[:, :LANES]
        anew_scr[...] = jnp.where(own[:, :LANES], pn_new[:n_heads] - lam * pn_new[n_heads:], 0.0)
        acc_scr[...] = jnp.zeros(acc_scr.shape, F32)

    @pl.when(g >= ng)
    def _():
        acc = acc_scr[...]
        for p in range(pp):
            vb = page_buf[slot0 + p].reshape(rows, vd).astype(BF16)
            acc = acc + jnp.dot(a_scr[(g - ng) * pp + p].astype(BF16), vb, preferred_element_type=F32)
        acc_scr[...] = acc

    @pl.when(g == 2 * ng - 1)
    def _():
        o = acc_scr[...] + jnp.dot(anew_scr[...].astype(BF16), pad_rows(vn_ref[...]).astype(BF16),
                                   preferred_element_type=F32)
        o_ref[...] = _head_rmsnorm(o, g_ref[...], lam_init).astype(o_ref.dtype)


def _attention(page_table, q, k_new, v_new, cache_k, cache_v, qp, kp, vp, lam_vecs, subln_g,
               *, batch, seq, lam_init, pages_per_step, tq, row_chunks):
    nb = q.shape[0]
    n_pages = page_table.shape[1]
    pp = pages_per_step
    ng = n_pages // pp
    _, page, n_heads, vd = cache_k.shape
    rows = page * n_heads
    pt_flat = page_table.reshape(-1)
    nq = seq // tq
    n_units = batch * n_heads * nq
    n_steps = nb * 2 * ng
    steps_per_unit = n_steps // n_units
    assert steps_per_unit * n_units == n_steps

    def unit(b, g):
        u = (b * (2 * ng) + g) // steps_per_unit
        return u // (n_heads * nq), (u // nq) % n_heads, u % nq

    def qp_map(b, g, pt):
        pb, h, i = unit(b, g)
        return pb * nq + i, h

    def kvp_map(b, g, pt):
        pb, h, _ = unit(b, g)
        return pb, h

    head_spec = pl.BlockSpec((None, n_heads, vd), lambda b, g, pt: (b, 0, 0))
    vec = pl.BlockSpec((1, lam_vecs[0].shape[1]), lambda b, g, pt: (0, 0))

    grid_spec = pltpu.PrefetchScalarGridSpec(
        num_scalar_prefetch=1,
        grid=(nb, 2 * ng),
        in_specs=[head_spec, head_spec, head_spec, vec, vec, vec, vec,
                  pl.BlockSpec((1, vd), lambda b, g, pt: (0, 0)),
                  pl.BlockSpec((tq, vd), qp_map),
                  pl.BlockSpec((seq, vd), kvp_map),
                  pl.BlockSpec((seq, vd), kvp_map),
                  pl.BlockSpec(memory_space=pl.ANY),
                  pl.BlockSpec(memory_space=pl.ANY)],
        out_specs=(head_spec, pl.BlockSpec((tq, vd), qp_map)),
        scratch_shapes=[
            pltpu.VMEM((n_pages, 2 * n_heads, rows), F32),
            pltpu.VMEM((n_pages, n_heads, rows), F32),
            pltpu.VMEM((2 * n_heads, vd), BF16),
            pltpu.VMEM((n_heads, LANES), F32),
            pltpu.VMEM((n_heads, vd), F32),
            pltpu.VMEM((vd, seq), BF16),
            pltpu.VMEM((PAGE_GROUPS * pp, page, n_heads, vd), F32),
            pltpu.SemaphoreType.DMA((PAGE_GROUPS * pp,)),
        ],
    )
    return pl.pallas_call(
        functools.partial(_attn_kernel, lam_init=lam_init, pages_per_step=pp, n_pages=n_pages,
                          steps_per_unit=steps_per_unit, nq=nq, row_chunks=row_chunks),
        out_shape=(jax.ShapeDtypeStruct((nb, n_heads, vd), BF16),
                   jax.ShapeDtypeStruct(qp.shape, BF16)),
        grid_spec=grid_spec,
        compiler_params=pltpu.CompilerParams(
            dimension_semantics=("arbitrary", "arbitrary"), vmem_limit_bytes=VMEM_LIMIT),
        name="attention",
    )(pt_flat, q, k_new, v_new, *lam_vecs, subln_g, qp, kp, vp, cache_k, cache_v)


def _mixer_tail(o, cb, conv, ga, gc, x, wa_ref, wc_ref, wo_ref, g_ref, b_ref, alpha):
    ga = ga.astype(F32)
    gc = gc.astype(F32)
    y_attn = jnp.dot(o, wa_ref[...], preferred_element_type=F32)
    y_conv = jnp.dot((cb * conv).astype(BF16), wc_ref[...], preferred_element_type=F32)
    merged = jax.nn.sigmoid(ga) * y_attn + jax.nn.sigmoid(gc) * y_conv
    z = jnp.dot(merged.astype(BF16), wo_ref[...], preferred_element_type=F32)
    return _layer_norm(alpha * x + z, g_ref[...], b_ref[...])


def _merge_prompt_kernel(o_ref, cb_ref, cc_ref, ch_ref, pc_ref, ph_ref, ga_ref, gc_ref, x_ref,
                         cw_ref, wa_ref, wc_ref, wo_ref, g_ref, b_ref,
                         out_ref, tail_ref, u_scr, *, alpha, tiles_per_seq):
    i = pl.program_id(0)
    tm = cc_ref.shape[0]
    u = cc_ref[...] * ch_ref[...]
    prev = pc_ref[...] * ph_ref[...]
    prev = jnp.where(i % tiles_per_seq == 0, jnp.zeros_like(prev), prev)
    u_scr[0:SUBLANES, :] = prev
    u_scr[SUBLANES:SUBLANES + tm, :] = u
    u1 = u_scr[SUBLANES - 1:SUBLANES - 1 + tm, :]
    u2 = u_scr[SUBLANES - 2:SUBLANES - 2 + tm, :]
    conv = cw_ref[0:1, :] * u2 + cw_ref[1:2, :] * u1 + cw_ref[2:3, :] * u
    tail_ref[...] = u[tm - SUBLANES:tm, :]
    out_ref[...] = _mixer_tail(o_ref[...], cb_ref[...], conv, ga_ref[...], gc_ref[...],
                               x_ref[...], wa_ref, wc_ref, wo_ref, g_ref, b_ref, alpha)


def _merge_sample_kernel(o_ref, cb_ref, cc_ref, ch_ref, s0_ref, s1_ref, ga_ref, gc_ref, x_ref,
                         cw_ref, wa_ref, wc_ref, wo_ref, g_ref, b_ref,
                         out_ref, u_ref, *, alpha):
    u = cc_ref[...] * ch_ref[...]
    conv = cw_ref[0:1, :] * s0_ref[...] + cw_ref[1:2, :] * s1_ref[...] + cw_ref[2:3, :] * u
    u_ref[...] = u
    out_ref[...] = _mixer_tail(o_ref[...], cb_ref[...], conv, ga_ref[...], gc_ref[...],
                               x_ref[...], wa_ref, wc_ref, wo_ref, g_ref, b_ref, alpha)


def _const_spec(shape):
    return pl.BlockSpec(shape, lambda i: (0,) * len(shape), pipeline_mode=pl.Buffered(1))


def _merge_prompt(o, conv_in, gates, x, conv_w, wa, wc, wo, g, b, *, alpha, tm, seq):
    m, d = x.shape
    dc = o.shape[1]
    rows8 = tm // SUBLANES
    in_specs = [
        pl.BlockSpec((tm, dc), lambda i: (i, 0)),
        pl.BlockSpec((tm, dc), lambda i: (i, 0)),
        pl.BlockSpec((tm, dc), lambda i: (i, 1)),
        pl.BlockSpec((tm, dc), lambda i: (i, 2)),
        pl.BlockSpec((SUBLANES, dc), lambda i: (jnp.maximum(i * rows8 - 1, 0), 1)),
        pl.BlockSpec((SUBLANES, dc), lambda i: (jnp.maximum(i * rows8 - 1, 0), 2)),
        pl.BlockSpec((tm, d), lambda i: (i, 0)),
        pl.BlockSpec((tm, d), lambda i: (i, 1)),
        pl.BlockSpec((tm, d), lambda i: (i, 0)),
        _const_spec(conv_w.shape), _const_spec(wa.shape), _const_spec(wc.shape), _const_spec(wo.shape),
        _const_spec(g.shape), _const_spec(b.shape),
    ]
    return pl.pallas_call(
        functools.partial(_merge_prompt_kernel, alpha=alpha, tiles_per_seq=seq // tm),
        out_shape=(jax.ShapeDtypeStruct((m, d), F32),
                   jax.ShapeDtypeStruct((m // tm * SUBLANES, dc), F32)),
        grid=(m // tm,),
        in_specs=in_specs,
        out_specs=(pl.BlockSpec((tm, d), lambda i: (i, 0)),
                   pl.BlockSpec((SUBLANES, dc), lambda i: (i, 0))),
        scratch_shapes=[pltpu.VMEM((tm + SUBLANES, dc), F32)],
        compiler_params=pltpu.CompilerParams(
            dimension_semantics=("parallel",), vmem_limit_bytes=VMEM_LIMIT),
        name="merge_prompt",
    )(o, conv_in, conv_in, conv_in, conv_in, conv_in, gates, gates, x, conv_w, wa, wc, wo, g, b)


def _merge_sample(o, conv_in, gates, s0, s1, x, conv_w, wa, wc, wo, g, b, *, alpha):
    m, d = x.shape
    dc = o.shape[1]
    in_specs = [
        pl.BlockSpec((m, dc), lambda i: (0, 0)),
        pl.BlockSpec((m, dc), lambda i: (0, 0)),
        pl.BlockSpec((m, dc), lambda i: (0, 1)),
        pl.BlockSpec((m, dc), lambda i: (0, 2)),
        pl.BlockSpec((m, dc), lambda i: (0, 0)),
        pl.BlockSpec((m, dc), lambda i: (0, 0)),
        pl.BlockSpec((m, d), lambda i: (0, 0)),
        pl.BlockSpec((m, d), lambda i: (0, 1)),
        pl.BlockSpec((m, d), lambda i: (0, 0)),
        _const_spec(conv_w.shape), _const_spec(wa.shape), _const_spec(wc.shape), _const_spec(wo.shape),
        _const_spec(g.shape), _const_spec(b.shape),
    ]
    return pl.pallas_call(
        functools.partial(_merge_sample_kernel, alpha=alpha),
        out_shape=(jax.ShapeDtypeStruct((m, d), F32), jax.ShapeDtypeStruct((m, dc), F32)),
        grid=(1,),
        in_specs=in_specs,
        out_specs=(pl.BlockSpec((m, d), lambda i: (0, 0)), pl.BlockSpec((m, dc), lambda i: (0, 0))),
        compiler_params=pltpu.CompilerParams(
            dimension_semantics=("arbitrary",), vmem_limit_bytes=VMEM_LIMIT),
        name="merge_sample",
    )(o, conv_in, conv_in, conv_in, s0, s1, gates, gates, x, conv_w, wa, wc, wo, g, b)


def _rope_tables(pos, head_dim, rot_dim):
    half = rot_dim // 2
    inv_freq = jnp.power(ROPE_THETA, -jnp.arange(0, rot_dim, 2, dtype=F32) / rot_dim)
    ang = pos.astype(F32)[:, None] * inv_freq[None, :]
    cos, sin = jnp.cos(ang), jnp.sin(ang)
    n = pos.shape[0]
    ones = jnp.ones((n, head_dim - rot_dim), F32)
    zeros_h = jnp.zeros((n, half), F32)
    zeros_r = jnp.zeros((n, head_dim - rot_dim), F32)
    c = jnp.concatenate([cos, cos, ones], axis=1)
    lo = jnp.concatenate([-sin, zeros_h, zeros_r], axis=1)
    hi = jnp.concatenate([zeros_h, sin, zeros_r], axis=1)
    two = lambda t: jnp.concatenate([t, t], axis=1)
    return two(c), two(lo), two(hi)


def kernel(x_prompt, x_sample, cache_k, cache_v, state_conv, page_table, ln_g, ln_b, ffn1_w_gate, ffn1_w_up, ffn1_w_down, w_in, conv_w, lambda_q1, lambda_k1, lambda_q2, lambda_k2, subln_g, w_attn_out, w_conv_out, w_o, ffn2_w_gate, ffn2_w_up, ffn2_w_down):
    batch, seq, d_model = x_prompt.shape
    dec_batch, dec_seq, _ = x_sample.shape
    assert dec_seq == 1
    depth = ln_g.shape[0]
    page_size, n_heads, v_dim = cache_k.shape[2:]
    head_dim = v_dim // 2
    rot_dim = head_dim // 4
    width = n_heads * v_dim
    d_conv = conv_w.shape[2]
    assert width == d_conv and v_dim == LANES and n_heads == SUBLANES
    past_len = page_table.shape[1] * page_size
    alpha = (2.0 * depth) ** 0.25
    q_scale = head_dim ** -0.5

    tabs_p = _rope_tables(jnp.arange(seq, dtype=jnp.int32), head_dim, rot_dim)
    tabs_s = _rope_tables(jnp.full((dec_batch,), past_len, jnp.int32), head_dim, rot_dim)

    y_p = x_prompt.reshape(batch * seq, d_model)
    y_s = x_sample.reshape(dec_batch, d_model)
    outs = [[] for _ in range(6)]
    for layer in range(depth):
        lam_init = 0.8 - 0.6 * math.exp(-0.3 * layer)
        bf = lambda w: w[layer].astype(BF16)
        wg1, wu1, wd1 = ffn1_w_gate[layer], ffn1_w_up[layer], ffn1_w_down[layer]
        wg2, wu2, wd2 = ffn2_w_gate[layer], ffn2_w_up[layer], ffn2_w_down[layer]
        win, wa, wc, wo = w_in[layer], bf(w_attn_out), bf(w_conv_out), bf(w_o)
        g = [ln_g[layer, i][None, :] for i in range(3)]
        b = [ln_b[layer, i][None, :] for i in range(3)]
        lam_vecs = [v[layer][None, :] for v in (lambda_q1, lambda_k1, lambda_q2, lambda_k2)]
        sg = subln_g[layer][None, :]
        cw = conv_w[layer]

        x1, s1 = _ffn_ln(y_p, y_s, wg1, wu1, wd1, g[0], b[0], alpha=alpha, tm=1024, tf=256)
        q, k, v, kb, vb, conv_in, gates = _inproj(x1, win, *tabs_p, tm=512, tn=width, q_scale=q_scale,
                                                  rot_half=rot_dim // 2, n_heads=n_heads)
        qs, ks, vs, _, _, conv_in_s, gates_s = _inproj(s1, win, *tabs_s, tm=dec_batch, tn=width,
                                                       q_scale=q_scale, rot_half=rot_dim // 2, n_heads=n_heads)

        heads = lambda t: t.reshape(dec_batch, n_heads, v_dim)
        os_, o = _attention(page_table, heads(qs), heads(ks), heads(vs), cache_k[layer], cache_v[layer],
                            q, kb, vb, lam_vecs, sg, batch=batch, seq=seq, lam_init=lam_init,
                            pages_per_step=16, tq=256, row_chunks=1)

        x2, tails = _merge_prompt(o, conv_in, gates, x1, cw, wa, wc, wo, g[1], b[1],
                                  alpha=alpha, tm=256, seq=seq)
        st = state_conv[layer]
        s2, u_s = _merge_sample(os_.reshape(dec_batch, width), conv_in_s, gates_s, st[:, 0], st[:, 1], s1,
                                cw, wa, wc, wo, g[1], b[1], alpha=alpha)
        y_p, y_s = _ffn_ln(x2, s2, wg2, wu2, wd2, g[2], b[2], alpha=alpha, tm=1024, tf=256)
        tails = tails.reshape(batch, seq // 256, SUBLANES, d_conv)
        outs[0].append(k.reshape(batch, seq, n_heads, v_dim))
        outs[1].append(v.reshape(batch, seq, n_heads, v_dim))
        outs[2].append(tails[:, -1, SUBLANES - 2:, :])
        outs[3].append(ks.reshape(dec_batch, 1, n_heads, v_dim))
        outs[4].append(vs.reshape(dec_batch, 1, n_heads, v_dim))
        outs[5].append(jnp.stack([st[:, 1], u_s], axis=1))

    return (y_p.reshape(batch, seq, d_model), y_s.reshape(dec_batch, 1, d_model),
            jnp.stack(outs[0]), jnp.stack(outs[1]), jnp.stack(outs[2]),
            jnp.stack(outs[3]), jnp.stack(outs[4]), jnp.stack(outs[5]))
```

```python
import functools
import math

import jax
import jax.numpy as jnp
from jax import lax
from jax.experimental import pallas as pl
from jax.experimental.pallas import tpu as pltpu

ROPE_THETA = 500000.0
LN_EPS = 1e-5
NEG_INF = -1e30
LANES = 128
SUBLANES = 8
VMEM_LIMIT = 56 * 1024 * 1024
LN_ROWS = 256
SOFTMAX_UNROLL = 4
PAGE_GROUPS = 3
BF16 = jnp.bfloat16
F32 = jnp.float32


def _layer_norm(y, g, b):
    mu = jnp.mean(y, axis=-1, keepdims=True)
    d = y - mu
    var = jnp.mean(d * d, axis=-1, keepdims=True)
    return d * lax.rsqrt(var + LN_EPS) * g + b


def _ffn_ln_kernel(x_ref, xs_ref, wg_ref, wu_ref, wd_ref, g_ref, b_ref, o_ref, os_ref, xb_ref, *, alpha):
    i = pl.program_id(0)
    f = pl.program_id(1)
    nf = pl.num_programs(1)
    tm = x_ref.shape[0]
    first_tile = i == 0

    @pl.when(f == 0)
    def _():
        xb_ref[0:tm, :] = x_ref[...].astype(BF16)
        o_ref[...] = jnp.zeros(o_ref.shape, F32)

    @pl.when(jnp.logical_and(first_tile, f == 0))
    def _():
        xb_ref[tm:, :] = xs_ref[...].astype(BF16)
        os_ref[...] = jnp.zeros(os_ref.shape, F32)

    xb = xb_ref[...]
    hg = jnp.dot(xb, wg_ref[...].astype(BF16), preferred_element_type=F32)
    hu = jnp.dot(xb, wu_ref[...].astype(BF16), preferred_element_type=F32)
    h = (hg * jax.nn.sigmoid(hg) * hu).astype(BF16)
    part = jnp.dot(h, wd_ref[...].astype(BF16), preferred_element_type=F32)
    o_ref[...] += part[:tm]

    @pl.when(first_tile)
    def _():
        os_ref[...] += part[tm:]

    @pl.when(f == nf - 1)
    def _():
        rows = min(tm, LN_ROWS)
        for r in range(tm // rows):
            sl = slice(r * rows, (r + 1) * rows)
            y = alpha * x_ref[sl, :] + 0.5 * o_ref[sl, :]
            o_ref[sl, :] = _layer_norm(y, g_ref[...], b_ref[...])

    @pl.when(jnp.logical_and(first_tile, f == nf - 1))
    def _():
        y = alpha * xs_ref[...] + 0.5 * os_ref[...]
        os_ref[...] = _layer_norm(y, g_ref[...], b_ref[...])


def _ffn_ln(x, xs, wg, wu, wd, g, b, *, alpha, tm, tf):
    m, d = x.shape
    ms = xs.shape[0]
    dff = wg.shape[1]
    return pl.pallas_call(
        functools.partial(_ffn_ln_kernel, alpha=alpha),
        out_shape=(jax.ShapeDtypeStruct((m, d), F32), jax.ShapeDtypeStruct((ms, d), F32)),
        grid=(m // tm, dff // tf),
        in_specs=[
            pl.BlockSpec((tm, d), lambda i, f: (i, 0)),
            pl.BlockSpec((ms, d), lambda i, f: (0, 0)),
            pl.BlockSpec((d, tf), lambda i, f: (0, f)),
            pl.BlockSpec((d, tf), lambda i, f: (0, f)),
            pl.BlockSpec((tf, d), lambda i, f: (f, 0)),
            pl.BlockSpec((1, d), lambda i, f: (0, 0)),
            pl.BlockSpec((1, d), lambda i, f: (0, 0)),
        ],
        out_specs=(pl.BlockSpec((tm, d), lambda i, f: (i, 0)),
                   pl.BlockSpec((ms, d), lambda i, f: (0, 0))),
        scratch_shapes=[pltpu.VMEM((tm + ms, d), BF16)],
        compiler_params=pltpu.CompilerParams(
            dimension_semantics=("arbitrary", "arbitrary"), vmem_limit_bytes=VMEM_LIMIT),
        name="ffn_ln",
    )(x, xs, wg, wu, wd, g, b)


N_QKV_TILES = 3
N_CONV_TILES = 3
def _rope_tile(z, cos, sin_lo, sin_hi, rot_half):
    up = pltpu.roll(z, LANES - rot_half, 1)
    down = pltpu.roll(z, rot_half, 1)
    return z * cos + up * sin_lo + down * sin_hi


def _inproj_kernel(x_ref, w_ref, cos_ref, slo_ref, shi_ref,
                   q_ref, k_ref, v_ref, kb_ref, vb_ref, r_ref, gate_ref, xb_ref,
                   *, q_scale, rot_half, n_heads):
    j = pl.program_id(1)

    @pl.when(j == 0)
    def _():
        xb_ref[...] = x_ref[...].astype(BF16)

    @pl.when(j < N_QKV_TILES + N_CONV_TILES)
    def _():
        r_ref[...] = jnp.dot(xb_ref[...], w_ref[...], preferred_element_type=F32)

    @pl.when(j >= N_QKV_TILES + N_CONV_TILES)
    def _():
        gate_ref[...] = jnp.dot(xb_ref[...], w_ref[...], preferred_element_type=F32).astype(BF16)

    @pl.when(j == 0)
    def _():
        for h in range(n_heads):
            sl = slice(h * LANES, (h + 1) * LANES)
            r = _rope_tile(r_ref[:, sl], cos_ref[...], slo_ref[...], shi_ref[...], rot_half)
            q_ref[:, sl] = (r * q_scale).astype(BF16)

    @pl.when(j == 1)
    def _():
        for h in range(n_heads):
            sl = slice(h * LANES, (h + 1) * LANES)
            r = _rope_tile(r_ref[:, sl], cos_ref[...], slo_ref[...], shi_ref[...], rot_half)
            k_ref[:, sl] = r
            kb_ref[:, sl] = r.astype(BF16)

    @pl.when(j == 2)
    def _():
        v_ref[...] = r_ref[...]
        vb_ref[...] = r_ref[...].astype(BF16)


def _inproj(x, w, cos, slo, shi, *, tm, tn, q_scale, rot_half, n_heads):
    m, d = x.shape
    n = w.shape[1]
    nj = n // tn
    n_first = N_QKV_TILES + N_CONV_TILES
    n_gate_tiles = nj - n_first
    tab_blocks = cos.shape[0] // tm
    tab_spec = pl.BlockSpec((tm, LANES), lambda i, j: (i % tab_blocks, 0))
    row_spec = pl.BlockSpec((tm, tn), lambda i, j: (i, 0), pipeline_mode=pl.Buffered(1))
    return pl.pallas_call(
        functools.partial(_inproj_kernel, q_scale=q_scale, rot_half=rot_half, n_heads=n_heads),
        out_shape=(
            jax.ShapeDtypeStruct((m, tn), BF16),
            jax.ShapeDtypeStruct((m, tn), F32),
            jax.ShapeDtypeStruct((m, tn), F32),
            jax.ShapeDtypeStruct((m, tn), BF16),
            jax.ShapeDtypeStruct((m, tn), BF16),
            jax.ShapeDtypeStruct((m, N_CONV_TILES * tn), F32),
            jax.ShapeDtypeStruct((m, n_gate_tiles * tn), BF16),
        ),
        grid=(m // tm, nj),
        in_specs=[
            pl.BlockSpec((tm, d), lambda i, j: (i, 0), pipeline_mode=pl.Buffered(1)),
            pl.BlockSpec((d, tn), lambda i, j: (0, j)),
            tab_spec, tab_spec, tab_spec,
        ],
        out_specs=(row_spec, row_spec, row_spec, row_spec, row_spec,
                   pl.BlockSpec((tm, tn), lambda i, j: (i, jnp.clip(j - N_QKV_TILES, 0, N_CONV_TILES - 1))),
                   pl.BlockSpec((tm, tn), lambda i, j: (i, jnp.maximum(j - n_first, 0)))),
        scratch_shapes=[pltpu.VMEM((tm, d), BF16)],
        compiler_params=pltpu.CompilerParams(
            dimension_semantics=("arbitrary", "arbitrary"), vmem_limit_bytes=VMEM_LIMIT),
        name="inproj",
    )(x, w, cos, slo, shi)


def _lambda_value(lq1_ref, lk1_ref, lq2_ref, lk2_ref, lam_init):
    s1 = jnp.sum(lq1_ref[...] * lk1_ref[...], axis=-1, keepdims=True)
    s2 = jnp.sum(lq2_ref[...] * lk2_ref[...], axis=-1, keepdims=True)
    return jnp.exp(s1) - jnp.exp(s2) + lam_init


def _head_rmsnorm(o, g, lam_init):
    ms = jnp.mean(o * o, axis=-1, keepdims=True)
    return o * lax.rsqrt(ms + LN_EPS) * g * (1.0 - lam_init)


def _reduce_keys(x, reduce_fn, combine_fn, groups=8):
    n, w = x.shape
    x3 = x.reshape(n // SUBLANES, SUBLANES, w)
    per = x3.shape[0] // groups
    parts = [reduce_fn(x3[g * per:(g + 1) * per], axis=0) for g in range(groups)]
    while len(parts) > 1:
        parts = [combine_fn(parts[i], parts[i + 1]) for i in range(0, len(parts), 2)]
    return reduce_fn(parts[0], axis=0, keepdims=True)


def _prompt_attn_unit(qi, q_ref, k_ref, v_ref, lam, g_ref, o_ref, vt_ref,
                      *, lam_init, nq, row_chunks):
    tq, vd = q_ref.shape
    head_dim = vd // 2
    rc = tq // row_chunks
    seq = k_ref.shape[0]

    @pl.when(qi == 0)
    def _():
        for c in range(seq // tq):
            cols = slice(c * tq, (c + 1) * tq)
            vt_ref[:, cols] = v_ref[cols, :].astype(F32).T.astype(BF16)

    q = q_ref[...]
    lane = lax.broadcasted_iota(jnp.int32, q.shape, 1)
    zero = jnp.zeros_like(q)
    q_sub = (jnp.where(lane < head_dim, q, zero), jnp.where(lane >= head_dim, q, zero))
    key = lax.broadcasted_iota(jnp.int32, (rc, rc), 0)
    qry = lax.broadcasted_iota(jnp.int32, (rc, rc), 1)
    keep = jnp.concatenate([key <= qry, key <= qry], axis=1)

    def softmax_av(qc, ext):
        s = lax.dot_general(k_ref[0:ext, :], qc, (((1,), (1,)), ((), ())), preferred_element_type=F32)
        diag = jnp.where(keep, s[ext - rc:, :], NEG_INF)
        s = diag if ext == rc else jnp.concatenate([s[:ext - rc, :], diag], axis=0)
        m = _reduce_keys(s, jnp.max, jnp.maximum)
        p = jnp.exp(s - m)
        l = _reduce_keys(p, jnp.sum, jnp.add)
        return jnp.dot(vt_ref[:, 0:ext], p.astype(BF16), preferred_element_type=F32) / l

    for i in range(nq):
        @pl.when(qi == i)
        def _(i=i):
            for r in range(row_chunks):
                rows = slice(r * rc, (r + 1) * rc)
                ext = i * tq + (r + 1) * rc
                o = softmax_av(jnp.concatenate([q_sub[0][rows], q_sub[1][rows]], axis=0), ext)
                o = o[:, :rc] - lam * o[:, rc:]
                o_ref[rows, :] = _head_rmsnorm(o.T, g_ref[...], lam_init).astype(BF16)


def _class_reduce(x, op, n_classes):
    shift = n_classes
    while shift < x.shape[-1]:
        x = op(x, pltpu.roll(x, shift, 1))
        shift *= 2
    return x


def _attn_kernel(pt_ref, q_ref, kn_ref, vn_ref, lq1_ref, lk1_ref, lq2_ref, lk2_ref, g_ref,
                 qp_ref, kp_ref, vp_ref, ck_hbm, cv_hbm, o_ref, op_ref,
                 s_scr, a_scr, qt_scr, anew_scr, acc_scr, vt_scr, page_buf, page_sem,
                 *, lam_init, pages_per_step, n_pages, steps_per_unit, nq, row_chunks):
    pp = pages_per_step
    g = pl.program_id(1)
    ng = n_pages // pp
    steps_per_seq = 2 * ng
    n_steps = pl.num_programs(0) * steps_per_seq
    step = pl.program_id(0) * steps_per_seq + g

    def page_copy(src_hbm, page_id, slot):
        return pltpu.make_async_copy(src_hbm.at[page_id], page_buf.at[slot], page_sem.at[slot])

    def fetch_group(t):
        seq_id = t // steps_per_seq
        j = t % steps_per_seq
        is_k = j < ng
        first = seq_id * n_pages + jnp.where(is_k, j, j - ng) * pp
        slot0 = (t % PAGE_GROUPS) * pp

        @pl.when(is_k)
        def _():
            for p in range(pp):
                page_copy(ck_hbm, pt_ref[first + p], slot0 + p).start()

        @pl.when(jnp.logical_not(is_k))
        def _():
            for p in range(pp):
                page_copy(cv_hbm, pt_ref[first + p], slot0 + p).start()

    @pl.when(step == 0)
    def _():
        for t in range(PAGE_GROUPS - 1):
            fetch_group(jnp.int32(t))

    @pl.when(step + PAGE_GROUPS - 1 < n_steps)
    def _():
        fetch_group(step + PAGE_GROUPS - 1)

    @pl.when(step % steps_per_unit == 0)
    def _():
        lam = _lambda_value(lq1_ref, lk1_ref, lq2_ref, lk2_ref, lam_init)
        _prompt_attn_unit((step // steps_per_unit) % nq, qp_ref, kp_ref, vp_ref, lam, g_ref, op_ref, vt_scr,
                          lam_init=lam_init, nq=nq, row_chunks=row_chunks)

    slot0 = (step % PAGE_GROUPS) * pp
    for p in range(pp):
        page_copy(ck_hbm, 0, slot0 + p).wait()

    _, page, n_heads, vd = page_buf.shape
    rows = page * n_heads
    hd = vd // 2
    nt = (((1,), (1,)), ((), ()))

    def pad_rows(x):
        return jnp.concatenate([x, jnp.zeros((LANES - n_heads, vd), x.dtype)], axis=0)

    @pl.when(g == 0)
    def _():
        q = q_ref[...]
        qq = jnp.concatenate([q, q], axis=0)
        row = lax.broadcasted_iota(jnp.int32, qq.shape, 0)
        lane = lax.broadcasted_iota(jnp.int32, qq.shape, 1)
        qt_scr[...] = jnp.where(row // n_heads == lane // hd, qq, jnp.zeros_like(qq))

    @pl.when(g < ng)
    def _():
        for p in range(pp):
            kb = page_buf[slot0 + p].reshape(rows, vd).astype(BF16)
            s_scr[g * pp + p] = lax.dot_general(qt_scr[...], kb, nt, preferred_element_type=F32)

    @pl.when(g == ng)
    def _():
        lane1 = lax.broadcasted_iota(jnp.int32, (2 * n_heads, LANES), 1)
        s_new = lax.dot_general(qt_scr[...], pad_rows(kn_ref[...]).astype(BF16), nt,
                                preferred_element_type=F32)
        s_new = jnp.where(lane1 < n_heads, s_new, NEG_INF)

        def with_new(x, x_new, op):
            return jnp.concatenate([op(x[:, :LANES], x_new), x[:, LANES:]], axis=1)

        m = lax.fori_loop(0, n_pages, lambda i, m: jnp.maximum(m, s_scr[i]),
                          jnp.full((2 * n_heads, rows), NEG_INF, F32), unroll=SOFTMAX_UNROLL)
        m = _class_reduce(with_new(m, s_new, jnp.maximum), jnp.maximum, n_heads)

        def sum_body(i, l):
            p = jnp.exp(s_scr[i] - m)
            s_scr[i] = p
            return l + p
        l = lax.fori_loop(0, n_pages, sum_body, jnp.zeros((2 * n_heads, rows), F32), unroll=SOFTMAX_UNROLL)
        p_new = jnp.exp(s_new - m[:, :LANES])
        l = _class_reduce(with_new(l, p_new, jnp.add), jnp.add, n_heads)

        lam = _lambda_value(lq1_ref, lk1_ref, lq2_ref, lk2_ref, lam_init)
        row = lax.broadcasted_iota(jnp.int32, (n_heads, rows), 0)
        lane = lax.broadcasted_iota(jnp.int32, (n_heads, rows), 1)
        own = row == lane % n_heads

        def a_body(i, c):
            pn = s_scr[i] / l
            a_scr[i] = jnp.where(own, pn[:n_heads] - lam * pn[n_heads:], 0.0)
            return c
        lax.fori_loop(0, n_pages, a_body, 0, unroll=SOFTMAX_UNROLL)
        pn_new = p_new / l[:, :LANES]
        anew_scr[...] = jnp.where(own[:, :LANES], pn_new[:n_heads] - lam * pn_new[n_heads:], 0.0)
        acc_scr[...] = jnp.zeros(acc_scr.shape, F32)

    @pl.when(g >= ng)
    def _():
        acc = acc_scr[...]
        for p in range(pp):
            vb = page_buf[slot0 + p].reshape(rows, vd).astype(BF16)
            acc = acc + jnp.dot(a_scr[(g - ng) * pp + p].astype(BF16), vb, preferred_element_type=F32)
        acc_scr[...] = acc

    @pl.when(g == 2 * ng - 1)
    def _():
        o = acc_scr[...] + jnp.dot(anew_scr[...].astype(BF16), pad_rows(vn_ref[...]).astype(BF16),
                                   preferred_element_type=F32)
        o_ref[...] = _head_rmsnorm(o, g_ref[...], lam_init).astype(o_ref.dtype)


def _attention(page_table, q, k_new, v_new, cache_k, cache_v, qp, kp, vp, lam_vecs, subln_g,
               *, batch, seq, lam_init, pages_per_step, tq, row_chunks):
    nb = q.shape[0]
    n_pages = page_table.shape[1]
    pp = pages_per_step
    ng = n_pages // pp
    _, page, n_heads, vd = cache_k.shape
    rows = page * n_heads
    pt_flat = page_table.reshape(-1)
    nq = seq // tq
    n_units = batch * n_heads * nq
    n_steps = nb * 2 * ng
    steps_per_unit = n_steps // n_units
    assert steps_per_unit * n_units == n_steps

    def unit(b, g):
        u = (b * (2 * ng) + g) // steps_per_unit
        return u // (n_heads * nq), (u // nq) % n_heads, u % nq

    def qp_map(b, g, pt):
        pb, h, i = unit(b, g)
        return pb * nq + i, h

    def kvp_map(b, g, pt):
        pb, h, _ = unit(b, g)
        return pb, h

    head_spec = pl.BlockSpec((None, n_heads, vd), lambda b, g, pt: (b, 0, 0))
    vec = pl.BlockSpec((1, lam_vecs[0].shape[1]), lambda b, g, pt: (0, 0))

    grid_spec = pltpu.PrefetchScalarGridSpec(
        num_scalar_prefetch=1,
        grid=(nb, 2 * ng),
        in_specs=[head_spec, head_spec, head_spec, vec, vec, vec, vec,
                  pl.BlockSpec((1, vd), lambda b, g, pt: (0, 0)),
                  pl.BlockSpec((tq, vd), qp_map),
                  pl.BlockSpec((seq, vd), kvp_map),
                  pl.BlockSpec((seq, vd), kvp_map),
                  pl.BlockSpec(memory_space=pl.ANY),
                  pl.BlockSpec(memory_space=pl.ANY)],
        out_specs=(head_spec, pl.BlockSpec((tq, vd), qp_map)),
        scratch_shapes=[
            pltpu.VMEM((n_pages, 2 * n_heads, rows), F32),
            pltpu.VMEM((n_pages, n_heads, rows), F32),
            pltpu.VMEM((2 * n_heads, vd), BF16),
            pltpu.VMEM((n_heads, LANES), F32),
            pltpu.VMEM((n_heads, vd), F32),
            pltpu.VMEM((vd, seq), BF16),
            pltpu.VMEM((PAGE_GROUPS * pp, page, n_heads, vd), F32),
            pltpu.SemaphoreType.DMA((PAGE_GROUPS * pp,)),
        ],
    )
    return pl.pallas_call(
        functools.partial(_attn_kernel, lam_init=lam_init, pages_per_step=pp, n_pages=n_pages,
                          steps_per_unit=steps_per_unit, nq=nq, row_chunks=row_chunks),
        out_shape=(jax.ShapeDtypeStruct((nb, n_heads, vd), BF16),
                   jax.ShapeDtypeStruct(qp.shape, BF16)),
        grid_spec=grid_spec,
        compiler_params=pltpu.CompilerParams(
            dimension_semantics=("arbitrary", "arbitrary"), vmem_limit_bytes=VMEM_LIMIT),
        name="attention",
    )(pt_flat, q, k_new, v_new, *lam_vecs, subln_g, qp, kp, vp, cache_k, cache_v)


def _mixer_tail(o, cb, conv, ga, gc, x, wa_ref, wc_ref, wo_ref, g_ref, b_ref, alpha):
    ga = ga.astype(F32)
    gc = gc.astype(F32)
    y_attn = jnp.dot(o, wa_ref[...], preferred_element_type=F32)
    y_conv = jnp.dot((cb * conv).astype(BF16), wc_ref[...], preferred_element_type=F32)
    merged = jax.nn.sigmoid(ga) * y_attn + jax.nn.sigmoid(gc) * y_conv
    z = jnp.dot(merged.astype(BF16), wo_ref[...], preferred_element_type=F32)
    return _layer_norm(alpha * x + z, g_ref[...], b_ref[...])


def _merge_prompt_kernel(o_ref, cb_ref, cc_ref, ch_ref, pc_ref, ph_ref, ga_ref, gc_ref, x_ref,
                         cw_ref, wa_ref, wc_ref, wo_ref, g_ref, b_ref,
                         out_ref, tail_ref, u_scr, *, alpha, tiles_per_seq):
    i = pl.program_id(0)
    tm = cc_ref.shape[0]
    u = cc_ref[...] * ch_ref[...]
    prev = pc_ref[...] * ph_ref[...]
    prev = jnp.where(i % tiles_per_seq == 0, jnp.zeros_like(prev), prev)
    u_scr[0:SUBLANES, :] = prev
    u_scr[SUBLANES:SUBLANES + tm, :] = u
    u1 = u_scr[SUBLANES - 1:SUBLANES - 1 + tm, :]
    u2 = u_scr[SUBLANES - 2:SUBLANES - 2 + tm, :]
    conv = cw_ref[0:1, :] * u2 + cw_ref[1:2, :] * u1 + cw_ref[2:3, :] * u
    tail_ref[...] = u[tm - SUBLANES:tm, :]
    out_ref[...] = _mixer_tail(o_ref[...], cb_ref[...], conv, ga_ref[...], gc_ref[...],
                               x_ref[...], wa_ref, wc_ref, wo_ref, g_ref, b_ref, alpha)


def _merge_sample_kernel(o_ref, cb_ref, cc_ref, ch_ref, s0_ref, s1_ref, ga_ref, gc_ref, x_ref,
                         cw_ref, wa_ref, wc_ref, wo_ref, g_ref, b_ref,
                         out_ref, u_ref, *, alpha):
    u = cc_ref[...] * ch_ref[...]
    conv = cw_ref[0:1, :] * s0_ref[...] + cw_ref[1:2, :] * s1_ref[...] + cw_ref[2:3, :] * u
    u_ref[...] = u
    out_ref[...] = _mixer_tail(o_ref[...], cb_ref[...], conv, ga_ref[...], gc_ref[...],
                               x_ref[...], wa_ref, wc_ref, wo_ref, g_ref, b_ref, alpha)


def _const_spec(shape):
    return pl.BlockSpec(shape, lambda i: (0,) * len(shape), pipeline_mode=pl.Buffered(1))


def _merge_prompt(o, conv_in, gates, x, conv_w, wa, wc, wo, g, b, *, alpha, tm, seq):
    m, d = x.shape
    dc = o.shape[1]
    rows8 = tm // SUBLANES
    in_specs = [
        pl.BlockSpec((tm, dc), lambda i: (i, 0)),
        pl.BlockSpec((tm, dc), lambda i: (i, 0)),
        pl.BlockSpec((tm, dc), lambda i: (i, 1)),
        pl.BlockSpec((tm, dc), lambda i: (i, 2)),
        pl.BlockSpec((SUBLANES, dc), lambda i: (jnp.maximum(i * rows8 - 1, 0), 1)),
        pl.BlockSpec((SUBLANES, dc), lambda i: (jnp.maximum(i * rows8 - 1, 0), 2)),
        pl.BlockSpec((tm, d), lambda i: (i, 0)),
        pl.BlockSpec((tm, d), lambda i: (i, 1)),
        pl.BlockSpec((tm, d), lambda i: (i, 0)),
        _const_spec(conv_w.shape), _const_spec(wa.shape), _const_spec(wc.shape), _const_spec(wo.shape),
        _const_spec(g.shape), _const_spec(b.shape),
    ]
    return pl.pallas_call(
        functools.partial(_merge_prompt_kernel, alpha=alpha, tiles_per_seq=seq // tm),
        out_shape=(jax.ShapeDtypeStruct((m, d), F32),
                   jax.ShapeDtypeStruct((m // tm * SUBLANES, dc), F32)),
        grid=(m // tm,),
        in_specs=in_specs,
        out_specs=(pl.BlockSpec((tm, d), lambda i: (i, 0)),
                   pl.BlockSpec((SUBLANES, dc), lambda i: (i, 0))),
        scratch_shapes=[pltpu.VMEM((tm + SUBLANES, dc), F32)],
        compiler_params=pltpu.CompilerParams(
            dimension_semantics=("parallel",), vmem_limit_bytes=VMEM_LIMIT),
        name="merge_prompt",
    )(o, conv_in, conv_in, conv_in, conv_in, conv_in, gates, gates, x, conv_w, wa, wc, wo, g, b)


def _merge_sample(o, conv_in, gates, s0, s1, x, conv_w, wa, wc, wo, g, b, *, alpha):
    m, d = x.shape
    dc = o.shape[1]
    in_specs = [
        pl.BlockSpec((m, dc), lambda i: (0, 0)),
        pl.BlockSpec((m, dc), lambda i: (0, 0)),
        pl.BlockSpec((m, dc), lambda i: (0, 1)),
        pl.BlockSpec((m, dc), lambda i: (0, 2)),
        pl.BlockSpec((m, dc), lambda i: (0, 0)),
        pl.BlockSpec((m, dc), lambda i: (0, 0)),
        pl.BlockSpec((m, d), lambda i: (0, 0)),
        pl.BlockSpec((m, d), lambda i: (0, 1)),
        pl.BlockSpec((m, d), lambda i: (0, 0)),
        _const_spec(conv_w.shape), _const_spec(wa.shape), _const_spec(wc.shape), _const_spec(wo.shape),
        _const_spec(g.shape), _const_spec(b.shape),
    ]
    return pl.pallas_call(
        functools.partial(_merge_sample_kernel, alpha=alpha),
        out_shape=(jax.ShapeDtypeStruct((m, d), F32), jax.ShapeDtypeStruct((m, dc), F32)),
        grid=(1,),
        in_specs=in_specs,
        out_specs=(pl.BlockSpec((m, d), lambda i: (0, 0)), pl.BlockSpec((m, dc), lambda i: (0, 0))),
        compiler_params=pltpu.CompilerParams(
            dimension_semantics=("arbitrary",), vmem_limit_bytes=VMEM_LIMIT),
        name="merge_sample",
    )(o, conv_in, conv_in, conv_in, s0, s1, gates, gates, x, conv_w, wa, wc, wo, g, b)


def _rope_tables(pos, head_dim, rot_dim):
    half = rot_dim // 2
    inv_freq = jnp.power(ROPE_THETA, -jnp.arange(0, rot_dim, 2, dtype=F32) / rot_dim)
    ang = pos.astype(F32)[:, None] * inv_freq[None, :]
    cos, sin = jnp.cos(ang), jnp.sin(ang)
    n = pos.shape[0]
    ones = jnp.ones((n, head_dim - rot_dim), F32)
    zeros_h = jnp.zeros((n, half), F32)
    zeros_r = jnp.zeros((n, head_dim - rot_dim), F32)
    c = jnp.concatenate([cos, cos, ones], axis=1)
    lo = jnp.concatenate([-sin, zeros_h, zeros_r], axis=1)
    hi = jnp.concatenate([zeros_h, sin, zeros_r], axis=1)
    two = lambda t: jnp.concatenate([t, t], axis=1)
    return two(c), two(lo), two(hi)


def kernel(x_prompt, x_sample, cache_k, cache_v, state_conv, page_table, ln_g, ln_b, ffn1_w_gate, ffn1_w_up, ffn1_w_down, w_in, conv_w, lambda_q1, lambda_k1, lambda_q2, lambda_k2, subln_g, w_attn_out, w_conv_out, w_o, ffn2_w_gate, ffn2_w_up, ffn2_w_down):
    batch, seq, d_model = x_prompt.shape
    dec_batch, dec_seq, _ = x_sample.shape
    assert dec_seq == 1
    depth = ln_g.shape[0]
    page_size, n_heads, v_dim = cache_k.shape[2:]
    head_dim = v_dim // 2
    rot_dim = head_dim // 4
    width = n_heads * v_dim
    d_conv = conv_w.shape[2]
    assert width == d_conv and v_dim == LANES and n_heads == SUBLANES
    past_len = page_table.shape[1] * page_size
    alpha = (2.0 * depth) ** 0.25
    q_scale = head_dim ** -0.5

    tabs_p = _rope_tables(jnp.arange(seq, dtype=jnp.int32), head_dim, rot_dim)
    tabs_s = _rope_tables(jnp.full((dec_batch,), past_len, jnp.int32), head_dim, rot_dim)

    y_p = x_prompt.reshape(batch * seq, d_model)
    y_s = x_sample.reshape(dec_batch, d_model)
    outs = [[] for _ in range(6)]
    for layer in range(depth):
        lam_init = 0.8 - 0.6 * math.exp(-0.3 * layer)
        bf = lambda w: w[layer].astype(BF16)
        wg1, wu1, wd1 = ffn1_w_gate[layer], ffn1_w_up[layer], ffn1_w_down[layer]
        wg2, wu2, wd2 = ffn2_w_gate[layer], ffn2_w_up[layer], ffn2_w_down[layer]
        win, wa, wc, wo = bf(w_in), bf(w_attn_out), bf(w_conv_out), bf(w_o)
        g = [ln_g[layer, i][None, :] for i in range(3)]
        b = [ln_b[layer, i][None, :] for i in range(3)]
        lam_vecs = [v[layer][None, :] for v in (lambda_q1, lambda_k1, lambda_q2, lambda_k2)]
        sg = subln_g[layer][None, :]
        cw = conv_w[layer]

        x1, s1 = _ffn_ln(y_p, y_s, wg1, wu1, wd1, g[0], b[0], alpha=alpha, tm=1024, tf=256)
        q, k, v, kb, vb, conv_in, gates = _inproj(x1, win, *tabs_p, tm=1024, tn=width, q_scale=q_scale,
                                                  rot_half=rot_dim // 2, n_heads=n_heads)
        qs, ks, vs, _, _, conv_in_s, gates_s = _inproj(s1, win, *tabs_s, tm=dec_batch, tn=width,
                                                       q_scale=q_scale, rot_half=rot_dim // 2, n_heads=n_heads)

        heads = lambda t: t.reshape(dec_batch, n_heads, v_dim)
        os_, o = _attention(page_table, heads(qs), heads(ks), heads(vs), cache_k[layer], cache_v[layer],
                            q, kb, vb, lam_vecs, sg, batch=batch, seq=seq, lam_init=lam_init,
                            pages_per_step=16, tq=256, row_chunks=1)

        x2, tails = _merge_prompt(o, conv_in, gates, x1, cw, wa, wc, wo, g[1], b[1],
                                  alpha=alpha, tm=256, seq=seq)
        st = state_conv[layer]
        s2, u_s = _merge_sample(os_.reshape(dec_batch, width), conv_in_s, gates_s, st[:, 0], st[:, 1], s1,
                                cw, wa, wc, wo, g[1], b[1], alpha=alpha)
        y_p, y_s = _ffn_ln(x2, s2, wg2, wu2, wd2, g[2], b[2], alpha=alpha, tm=1024, tf=256)
        tails = tails.reshape(batch, seq // 256, SUBLANES, d_conv)
        outs[0].append(k.reshape(batch, seq, n_heads, v_dim))
        outs[1].append(v.reshape(batch, seq, n_heads, v_dim))
        outs[2].append(tails[:, -1, SUBLANES - 2:, :])
        outs[3].append(ks.reshape(dec_batch, 1, n_heads, v_dim))
        outs[4].append(vs.reshape(dec_batch, 1, n_heads, v_dim))
        outs[5].append(jnp.stack([st[:, 1], u_s], axis=1))

    return (y_p.reshape(batch, seq, d_model), y_s.reshape(dec_batch, 1, d_model),
            jnp.stack(outs[0]), jnp.stack(outs[1]), jnp.stack(outs[2]),
            jnp.stack(outs[3]), jnp.stack(outs[4]), jnp.stack(outs[5]))
```

```python
import functools
import math

import jax
import jax.numpy as jnp
from jax import lax
from jax.experimental import pallas as pl
from jax.experimental.pallas import tpu as pltpu

ROPE_THETA = 500000.0
LN_EPS = 1e-5
NEG_INF = -1e30
LANES = 128
SUBLANES = 8
VMEM_LIMIT = 56 * 1024 * 1024
LN_ROWS = 256
SOFTMAX_UNROLL = 4
PAGE_GROUPS = 3
BF16 = jnp.bfloat16
F32 = jnp.float32


def _layer_norm(y, g, b):
    mu = jnp.mean(y, axis=-1, keepdims=True)
    d = y - mu
    var = jnp.mean(d * d, axis=-1, keepdims=True)
    return d * lax.rsqrt(var + LN_EPS) * g + b


def _ffn_ln_kernel(x_ref, xs_ref, wg_ref, wu_ref, wd_ref, g_ref, b_ref, o_ref, os_ref, xb_ref, *, alpha):
    i = pl.program_id(0)
    f = pl.program_id(1)
    nf = pl.num_programs(1)
    tm = x_ref.shape[0]
    first_tile = i == 0

    @pl.when(f == 0)
    def _():
        xb_ref[0:tm, :] = x_ref[...].astype(BF16)
        o_ref[...] = jnp.zeros(o_ref.shape, F32)

    @pl.when(jnp.logical_and(first_tile, f == 0))
    def _():
        xb_ref[tm:, :] = xs_ref[...].astype(BF16)
        os_ref[...] = jnp.zeros(os_ref.shape, F32)

    xb = xb_ref[...]
    hg = jnp.dot(xb, wg_ref[...].astype(BF16), preferred_element_type=F32)
    hu = jnp.dot(xb, wu_ref[...].astype(BF16), preferred_element_type=F32)
    h = (hg * jax.nn.sigmoid(hg) * hu).astype(BF16)
    part = jnp.dot(h, wd_ref[...].astype(BF16), preferred_element_type=F32)
    o_ref[...] += part[:tm]

    @pl.when(first_tile)
    def _():
        os_ref[...] += part[tm:]

    @pl.when(f == nf - 1)
    def _():
        rows = min(tm, LN_ROWS)
        for r in range(tm // rows):
            sl = slice(r * rows, (r + 1) * rows)
            y = alpha * x_ref[sl, :] + 0.5 * o_ref[sl, :]
            o_ref[sl, :] = _layer_norm(y, g_ref[...], b_ref[...])

    @pl.when(jnp.logical_and(first_tile, f == nf - 1))
    def _():
        y = alpha * xs_ref[...] + 0.5 * os_ref[...]
        os_ref[...] = _layer_norm(y, g_ref[...], b_ref[...])


def _ffn_ln(x, xs, wg, wu, wd, g, b, *, alpha, tm, tf):
    m, d = x.shape
    ms = xs.shape[0]
    dff = wg.shape[1]
    return pl.pallas_call(
        functools.partial(_ffn_ln_kernel, alpha=alpha),
        out_shape=(jax.ShapeDtypeStruct((m, d), F32), jax.ShapeDtypeStruct((ms, d), F32)),
        grid=(m // tm, dff // tf),
        in_specs=[
            pl.BlockSpec((tm, d), lambda i, f: (i, 0)),
            pl.BlockSpec((ms, d), lambda i, f: (0, 0)),
            pl.BlockSpec((d, tf), lambda i, f: (0, f)),
            pl.BlockSpec((d, tf), lambda i, f: (0, f)),
            pl.BlockSpec((tf, d), lambda i, f: (f, 0)),
            pl.BlockSpec((1, d), lambda i, f: (0, 0)),
            pl.BlockSpec((1, d), lambda i, f: (0, 0)),
        ],
        out_specs=(pl.BlockSpec((tm, d), lambda i, f: (i, 0)),
                   pl.BlockSpec((ms, d), lambda i, f: (0, 0))),
        scratch_shapes=[pltpu.VMEM((tm + ms, d), BF16)],
        compiler_params=pltpu.CompilerParams(
            dimension_semantics=("arbitrary", "arbitrary"), vmem_limit_bytes=VMEM_LIMIT),
        name="ffn_ln",
    )(x, xs, wg, wu, wd, g, b)


N_QKV_TILES = 3
N_CONV_TILES = 3
def _rope_tile(z, cos, sin_lo, sin_hi, rot_half):
    up = pltpu.roll(z, LANES - rot_half, 1)
    down = pltpu.roll(z, rot_half, 1)
    return z * cos + up * sin_lo + down * sin_hi


def _inproj_kernel(x_ref, w_ref, cos_ref, slo_ref, shi_ref,
                   q_ref, k_ref, v_ref, kb_ref, vb_ref, r_ref, gate_ref, xb_ref,
                   *, q_scale, rot_half, n_heads):
    j = pl.program_id(1)

    @pl.when(j == 0)
    def _():
        xb_ref[...] = x_ref[...].astype(BF16)

    @pl.when(j < N_QKV_TILES + N_CONV_TILES)
    def _():
        r_ref[...] = jnp.dot(xb_ref[...], w_ref[...], preferred_element_type=F32)

    @pl.when(j >= N_QKV_TILES + N_CONV_TILES)
    def _():
        gate_ref[...] = jnp.dot(xb_ref[...], w_ref[...], preferred_element_type=F32).astype(BF16)

    @pl.when(j == 0)
    def _():
        for h in range(n_heads):
            sl = slice(h * LANES, (h + 1) * LANES)
            r = _rope_tile(r_ref[:, sl], cos_ref[...], slo_ref[...], shi_ref[...], rot_half)
            q_ref[:, sl] = (r * q_scale).astype(BF16)

    @pl.when(j == 1)
    def _():
        for h in range(n_heads):
            sl = slice(h * LANES, (h + 1) * LANES)
            r = _rope_tile(r_ref[:, sl], cos_ref[...], slo_ref[...], shi_ref[...], rot_half)
            k_ref[:, sl] = r
            kb_ref[:, sl] = r.astype(BF16)

    @pl.when(j == 2)
    def _():
        v_ref[...] = r_ref[...]
        vb_ref[...] = r_ref[...].astype(BF16)


def _inproj(x, w, cos, slo, shi, *, tm, tn, q_scale, rot_half, n_heads):
    m, d = x.shape
    n = w.shape[1]
    nj = n // tn
    n_first = N_QKV_TILES + N_CONV_TILES
    n_gate_tiles = nj - n_first
    tab_blocks = cos.shape[0] // tm
    tab_spec = pl.BlockSpec((tm, LANES), lambda i, j: (i % tab_blocks, 0))
    row_spec = pl.BlockSpec((tm, tn), lambda i, j: (i, 0))
    return pl.pallas_call(
        functools.partial(_inproj_kernel, q_scale=q_scale, rot_half=rot_half, n_heads=n_heads),
        out_shape=(
            jax.ShapeDtypeStruct((m, tn), BF16),
            jax.ShapeDtypeStruct((m, tn), F32),
            jax.ShapeDtypeStruct((m, tn), F32),
            jax.ShapeDtypeStruct((m, tn), BF16),
            jax.ShapeDtypeStruct((m, tn), BF16),
            jax.ShapeDtypeStruct((m, N_CONV_TILES * tn), F32),
            jax.ShapeDtypeStruct((m, n_gate_tiles * tn), BF16),
        ),
        grid=(m // tm, nj),
        in_specs=[
            pl.BlockSpec((tm, d), lambda i, j: (i, 0)),
            pl.BlockSpec((d, tn), lambda i, j: (0, j)),
            tab_spec, tab_spec, tab_spec,
        ],
        out_specs=(row_spec, row_spec, row_spec, row_spec, row_spec,
                   pl.BlockSpec((tm, tn), lambda i, j: (i, jnp.clip(j - N_QKV_TILES, 0, N_CONV_TILES - 1))),
                   pl.BlockSpec((tm, tn), lambda i, j: (i, jnp.maximum(j - n_first, 0)))),
        scratch_shapes=[pltpu.VMEM((tm, d), BF16)],
        compiler_params=pltpu.CompilerParams(
            dimension_semantics=("arbitrary", "arbitrary"), vmem_limit_bytes=VMEM_LIMIT),
        name="inproj",
    )(x, w, cos, slo, shi)


def _lambda_value(lq1_ref, lk1_ref, lq2_ref, lk2_ref, lam_init):
    s1 = jnp.sum(lq1_ref[...] * lk1_ref[...], axis=-1, keepdims=True)
    s2 = jnp.sum(lq2_ref[...] * lk2_ref[...], axis=-1, keepdims=True)
    return jnp.exp(s1) - jnp.exp(s2) + lam_init


def _head_rmsnorm(o, g, lam_init):
    ms = jnp.mean(o * o, axis=-1, keepdims=True)
    return o * lax.rsqrt(ms + LN_EPS) * g * (1.0 - lam_init)


def _reduce_keys(x, reduce_fn, combine_fn, groups=8):
    n, w = x.shape
    x3 = x.reshape(n // SUBLANES, SUBLANES, w)
    per = x3.shape[0] // groups
    parts = [reduce_fn(x3[g * per:(g + 1) * per], axis=0) for g in range(groups)]
    while len(parts) > 1:
        parts = [combine_fn(parts[i], parts[i + 1]) for i in range(0, len(parts), 2)]
    return reduce_fn(parts[0], axis=0, keepdims=True)


def _prompt_attn_unit(qi, q_ref, k_ref, v_ref, lam, g_ref, o_ref, vt_ref,
                      *, lam_init, nq, row_chunks):
    tq, vd = q_ref.shape
    head_dim = vd // 2
    rc = tq // row_chunks
    seq = k_ref.shape[0]

    @pl.when(qi == 0)
    def _():
        for c in range(seq // tq):
            cols = slice(c * tq, (c + 1) * tq)
            vt_ref[:, cols] = v_ref[cols, :].astype(F32).T.astype(BF16)

    q = q_ref[...]
    lane = lax.broadcasted_iota(jnp.int32, q.shape, 1)
    zero = jnp.zeros_like(q)
    q_sub = (jnp.where(lane < head_dim, q, zero), jnp.where(lane >= head_dim, q, zero))
    key = lax.broadcasted_iota(jnp.int32, (rc, rc), 0)
    qry = lax.broadcasted_iota(jnp.int32, (rc, rc), 1)
    keep = jnp.concatenate([key <= qry, key <= qry], axis=1)

    def softmax_av(qc, ext):
        s = lax.dot_general(k_ref[0:ext, :], qc, (((1,), (1,)), ((), ())), preferred_element_type=F32)
        diag = jnp.where(keep, s[ext - rc:, :], NEG_INF)
        s = diag if ext == rc else jnp.concatenate([s[:ext - rc, :], diag], axis=0)
        m = _reduce_keys(s, jnp.max, jnp.maximum)
        p = jnp.exp(s - m)
        l = _reduce_keys(p, jnp.sum, jnp.add)
        return jnp.dot(vt_ref[:, 0:ext], p.astype(BF16), preferred_element_type=F32) / l

    for i in range(nq):
        @pl.when(qi == i)
        def _(i=i):
            for r in range(row_chunks):
                rows = slice(r * rc, (r + 1) * rc)
                ext = i * tq + (r + 1) * rc
                o = softmax_av(jnp.concatenate([q_sub[0][rows], q_sub[1][rows]], axis=0), ext)
                o = o[:, :rc] - lam * o[:, rc:]
                o_ref[rows, :] = _head_rmsnorm(o.T, g_ref[...], lam_init).astype(BF16)


def _class_reduce(x, op, n_classes):
    shift = n_classes
    while shift < x.shape[-1]:
        x = op(x, pltpu.roll(x, shift, 1))
        shift *= 2
    return x


def _attn_kernel(pt_ref, q_ref, kn_ref, vn_ref, lq1_ref, lk1_ref, lq2_ref, lk2_ref, g_ref,
                 qp_ref, kp_ref, vp_ref, ck_hbm, cv_hbm, o_ref, op_ref,
                 s_scr, a_scr, qt_scr, anew_scr, acc_scr, vt_scr, page_buf, page_sem,
                 *, lam_init, pages_per_step, n_pages, steps_per_unit, nq, row_chunks):
    pp = pages_per_step
    g = pl.program_id(1)
    ng = n_pages // pp
    steps_per_seq = 2 * ng
    n_steps = pl.num_programs(0) * steps_per_seq
    step = pl.program_id(0) * steps_per_seq + g

    def page_copy(src_hbm, page_id, slot):
        return pltpu.make_async_copy(src_hbm.at[page_id], page_buf.at[slot], page_sem.at[slot])

    def fetch_group(t):
        seq_id = t // steps_per_seq
        j = t % steps_per_seq
        is_k = j < ng
        first = seq_id * n_pages + jnp.where(is_k, j, j - ng) * pp
        slot0 = (t % PAGE_GROUPS) * pp

        @pl.when(is_k)
        def _():
            for p in range(pp):
                page_copy(ck_hbm, pt_ref[first + p], slot0 + p).start()

        @pl.when(jnp.logical_not(is_k))
        def _():
            for p in range(pp):
                page_copy(cv_hbm, pt_ref[first + p], slot0 + p).start()

    @pl.when(step == 0)
    def _():
        for t in range(PAGE_GROUPS - 1):
            fetch_group(jnp.int32(t))

    @pl.when(step + PAGE_GROUPS - 1 < n_steps)
    def _():
        fetch_group(step + PAGE_GROUPS - 1)

    @pl.when(step % steps_per_unit == 0)
    def _():
        lam = _lambda_value(lq1_ref, lk1_ref, lq2_ref, lk2_ref, lam_init)
        _prompt_attn_unit((step // steps_per_unit) % nq, qp_ref, kp_ref, vp_ref, lam, g_ref, op_ref, vt_scr,
                          lam_init=lam_init, nq=nq, row_chunks=row_chunks)

    slot0 = (step % PAGE_GROUPS) * pp
    for p in range(pp):
        page_copy(ck_hbm, 0, slot0 + p).wait()

    _, page, n_heads, vd = page_buf.shape
    rows = page * n_heads
    hd = vd // 2
    nt = (((1,), (1,)), ((), ()))

    def pad_rows(x):
        return jnp.concatenate([x, jnp.zeros((LANES - n_heads, vd), x.dtype)], axis=0)

    @pl.when(g == 0)
    def _():
        q = q_ref[...]
        qq = jnp.concatenate([q, q], axis=0)
        row = lax.broadcasted_iota(jnp.int32, qq.shape, 0)
        lane = lax.broadcasted_iota(jnp.int32, qq.shape, 1)
        qt_scr[...] = jnp.where(row // n_heads == lane // hd, qq, jnp.zeros_like(qq))

    @pl.when(g < ng)
    def _():
        for p in range(pp):
            kb = page_buf[slot0 + p].reshape(rows, vd).astype(BF16)
            s_scr[g * pp + p] = lax.dot_general(qt_scr[...], kb, nt, preferred_element_type=F32)

    @pl.when(g == ng)
    def _():
        lane1 = lax.broadcasted_iota(jnp.int32, (2 * n_heads, LANES), 1)
        s_new = lax.dot_general(qt_scr[...], pad_rows(kn_ref[...]).astype(BF16), nt,
                                preferred_element_type=F32)
        s_new = jnp.where(lane1 < n_heads, s_new, NEG_INF)

        def with_new(x, x_new, op):
            return jnp.concatenate([op(x[:, :LANES], x_new), x[:, LANES:]], axis=1)

        m = lax.fori_loop(0, n_pages, lambda i, m: jnp.maximum(m, s_scr[i]),
                          jnp.full((2 * n_heads, rows), NEG_INF, F32), unroll=SOFTMAX_UNROLL)
        m = _class_reduce(with_new(m, s_new, jnp.maximum), jnp.maximum, n_heads)

        def sum_body(i, l):
            p = jnp.exp(s_scr[i] - m)
            s_scr[i] = p
            return l + p
        l = lax.fori_loop(0, n_pages, sum_body, jnp.zeros((2 * n_heads, rows), F32), unroll=SOFTMAX_UNROLL)
        p_new = jnp.exp(s_new - m[:, :LANES])
        l = _class_reduce(with_new(l, p_new, jnp.add), jnp.add, n_heads)

        lam = _lambda_value(lq1_ref, lk1_ref, lq2_ref, lk2_ref, lam_init)
        row = lax.broadcasted_iota(jnp.int32, (n_heads, rows), 0)
        lane = lax.broadcasted_iota(jnp.int32, (n_heads, rows), 1)
        own = row == lane % n_heads

        def a_body(i, c):
            pn = s_scr[i] / l
            a_scr[i] = jnp.where(own, pn[:n_heads] - lam * pn[n_heads:], 0.0)
            return c
        lax.fori_loop(0, n_pages, a_body, 0, unroll=SOFTMAX_UNROLL)
        pn_new = p_new / l[:, :LANES]
        anew_scr[...] = jnp.where(own[:, :LANES], pn_new[:n_heads] - lam * pn_new[n_heads:], 0.0)
        acc_scr[...] = jnp.zeros(acc_scr.shape, F32)

    @pl.when(g >= ng)
    def _():
        acc = acc_scr[...]
        for p in range(pp):
            vb = page_buf[slot0 + p].reshape(rows, vd).astype(BF16)
            acc = acc + jnp.dot(a_scr[(g - ng) * pp + p].astype(BF16), vb, preferred_element_type=F32)
        acc_scr[...] = acc

    @pl.when(g == 2 * ng - 1)
    def _():
        o = acc_scr[...] + jnp.dot(anew_scr[...].astype(BF16), pad_rows(vn_ref[...]).astype(BF16),
                                   preferred_element_type=F32)
        o_ref[...] = _head_rmsnorm(o, g_ref[...], lam_init).astype(o_ref.dtype)


def _attention(page_table, q, k_new, v_new, cache_k, cache_v, qp, kp, vp, lam_vecs, subln_g,
               *, batch, seq, lam_init, pages_per_step, tq, row_chunks):
    nb = q.shape[0]
    n_pages = page_table.shape[1]
    pp = pages_per_step
    ng = n_pages // pp
    _, page, n_heads, vd = cache_k.shape
    rows = page * n_heads
    pt_flat = page_table.reshape(-1)
    nq = seq // tq
    n_units = batch * n_heads * nq
    n_steps = nb * 2 * ng
    steps_per_unit = n_steps // n_units
    assert steps_per_unit * n_units == n_steps

    def unit(b, g):
        u = (b * (2 * ng) + g) // steps_per_unit
        return u // (n_heads * nq), (u // nq) % n_heads, u % nq

    def qp_map(b, g, pt):
        pb, h, i = unit(b, g)
        return pb * nq + i, h

    def kvp_map(b, g, pt):
        pb, h, _ = unit(b, g)
        return pb, h

    head_spec = pl.BlockSpec((None, n_heads, vd), lambda b, g, pt: (b, 0, 0))
    vec = pl.BlockSpec((1, lam_vecs[0].shape[1]), lambda b, g, pt: (0, 0))

    grid_spec = pltpu.PrefetchScalarGridSpec(
        num_scalar_prefetch=1,
        grid=(nb, 2 * ng),
        in_specs=[head_spec, head_spec, head_spec, vec, vec, vec, vec,
                  pl.BlockSpec((1, vd), lambda b, g, pt: (0, 0)),
                  pl.BlockSpec((tq, vd), qp_map),
                  pl.BlockSpec((seq, vd), kvp_map),
                  pl.BlockSpec((seq, vd), kvp_map),
                  pl.BlockSpec(memory_space=pl.ANY),
                  pl.BlockSpec(memory_space=pl.ANY)],
        out_specs=(head_spec, pl.BlockSpec((tq, vd), qp_map)),
        scratch_shapes=[
            pltpu.VMEM((n_pages, 2 * n_heads, rows), F32),
            pltpu.VMEM((n_pages, n_heads, rows), F32),
            pltpu.VMEM((2 * n_heads, vd), BF16),
            pltpu.VMEM((n_heads, LANES), F32),
            pltpu.VMEM((n_heads, vd), F32),
            pltpu.VMEM((vd, seq), BF16),
            pltpu.VMEM((PAGE_GROUPS * pp, page, n_heads, vd), F32),
            pltpu.SemaphoreType.DMA((PAGE_GROUPS * pp,)),
        ],
    )
    return pl.pallas_call(
        functools.partial(_attn_kernel, lam_init=lam_init, pages_per_step=pp, n_pages=n_pages,
                          steps_per_unit=steps_per_unit, nq=nq, row_chunks=row_chunks),
        out_shape=(jax.ShapeDtypeStruct((nb, n_heads, vd), BF16),
                   jax.ShapeDtypeStruct(qp.shape, BF16)),
        grid_spec=grid_spec,
        compiler_params=pltpu.CompilerParams(
            dimension_semantics=("arbitrary", "arbitrary"), vmem_limit_bytes=VMEM_LIMIT),
        name="attention",
    )(pt_flat, q, k_new, v_new, *lam_vecs, subln_g, qp, kp, vp, cache_k, cache_v)


def _mixer_tail(o, cb, conv, ga, gc, x, wa_ref, wc_ref, wo_ref, g_ref, b_ref, alpha):
    ga = ga.astype(F32)
    gc = gc.astype(F32)
    y_attn = jnp.dot(o, wa_ref[...], preferred_element_type=F32)
    y_conv = jnp.dot((cb * conv).astype(BF16), wc_ref[...], preferred_element_type=F32)
    merged = jax.nn.sigmoid(ga) * y_attn + jax.nn.sigmoid(gc) * y_conv
    z = jnp.dot(merged.astype(BF16), wo_ref[...], preferred_element_type=F32)
    return _layer_norm(alpha * x + z, g_ref[...], b_ref[...])


def _merge_prompt_kernel(o_ref, cb_ref, cc_ref, ch_ref, pc_ref, ph_ref, ga_ref, gc_ref, x_ref,
                         cw_ref, wa_ref, wc_ref, wo_ref, g_ref, b_ref,
                         out_ref, tail_ref, u_scr, *, alpha, tiles_per_seq):
    i = pl.program_id(0)
    tm = cc_ref.shape[0]
    u = cc_ref[...] * ch_ref[...]
    prev = pc_ref[...] * ph_ref[...]
    prev = jnp.where(i % tiles_per_seq == 0, jnp.zeros_like(prev), prev)
    u_scr[0:SUBLANES, :] = prev
    u_scr[SUBLANES:SUBLANES + tm, :] = u
    u1 = u_scr[SUBLANES - 1:SUBLANES - 1 + tm, :]
    u2 = u_scr[SUBLANES - 2:SUBLANES - 2 + tm, :]
    conv = cw_ref[0:1, :] * u2 + cw_ref[1:2, :] * u1 + cw_ref[2:3, :] * u
    tail_ref[...] = u[tm - SUBLANES:tm, :]
    out_ref[...] = _mixer_tail(o_ref[...], cb_ref[...], conv, ga_ref[...], gc_ref[...],
                               x_ref[...], wa_ref, wc_ref, wo_ref, g_ref, b_ref, alpha)


def _merge_sample_kernel(o_ref, cb_ref, cc_ref, ch_ref, s0_ref, s1_ref, ga_ref, gc_ref, x_ref,
                         cw_ref, wa_ref, wc_ref, wo_ref, g_ref, b_ref,
                         out_ref, u_ref, *, alpha):
    u = cc_ref[...] * ch_ref[...]
    conv = cw_ref[0:1, :] * s0_ref[...] + cw_ref[1:2, :] * s1_ref[...] + cw_ref[2:3, :] * u
    u_ref[...] = u
    out_ref[...] = _mixer_tail(o_ref[...], cb_ref[...], conv, ga_ref[...], gc_ref[...],
                               x_ref[...], wa_ref, wc_ref, wo_ref, g_ref, b_ref, alpha)


def _const_spec(shape):
    return pl.BlockSpec(shape, lambda i: (0,) * len(shape), pipeline_mode=pl.Buffered(1))


def _merge_prompt(o, conv_in, gates, x, conv_w, wa, wc, wo, g, b, *, alpha, tm, seq):
    m, d = x.shape
    dc = o.shape[1]
    rows8 = tm // SUBLANES
    in_specs = [
        pl.BlockSpec((tm, dc), lambda i: (i, 0)),
        pl.BlockSpec((tm, dc), lambda i: (i, 0)),
        pl.BlockSpec((tm, dc), lambda i: (i, 1)),
        pl.BlockSpec((tm, dc), lambda i: (i, 2)),
        pl.BlockSpec((SUBLANES, dc), lambda i: (jnp.maximum(i * rows8 - 1, 0), 1)),
        pl.BlockSpec((SUBLANES, dc), lambda i: (jnp.maximum(i * rows8 - 1, 0), 2)),
        pl.BlockSpec((tm, d), lambda i: (i, 0)),
        pl.BlockSpec((tm, d), lambda i: (i, 1)),
        pl.BlockSpec((tm, d), lambda i: (i, 0)),
        _const_spec(conv_w.shape), _const_spec(wa.shape), _const_spec(wc.shape), _const_spec(wo.shape),
        _const_spec(g.shape), _const_spec(b.shape),
    ]
    return pl.pallas_call(
        functools.partial(_merge_prompt_kernel, alpha=alpha, tiles_per_seq=seq // tm),
        out_shape=(jax.ShapeDtypeStruct((m, d), F32),
                   jax.ShapeDtypeStruct((m // tm * SUBLANES, dc), F32)),
        grid=(m // tm,),
        in_specs=in_specs,
        out_specs=(pl.BlockSpec((tm, d), lambda i: (i, 0)),
                   pl.BlockSpec((SUBLANES, dc), lambda i: (i, 0))),
        scratch_shapes=[pltpu.VMEM((tm + SUBLANES, dc), F32)],
        compiler_params=pltpu.CompilerParams(
            dimension_semantics=("parallel",), vmem_limit_bytes=VMEM_LIMIT),
        name="merge_prompt",
    )(o, conv_in, conv_in, conv_in, conv_in, conv_in, gates, gates, x, conv_w, wa, wc, wo, g, b)


def _merge_sample(o, conv_in, gates, s0, s1, x, conv_w, wa, wc, wo, g, b, *, alpha):
    m, d = x.shape
    dc = o.shape[1]
    in_specs = [
        pl.BlockSpec((m, dc), lambda i: (0, 0)),
        pl.BlockSpec((m, dc), lambda i: (0, 0)),
        pl.BlockSpec((m, dc), lambda i: (0, 1)),
        pl.BlockSpec((m, dc), lambda i: (0, 2)),
        pl.BlockSpec((m, dc), lambda i: (0, 0)),
        pl.BlockSpec((m, dc), lambda i: (0, 0)),
        pl.BlockSpec((m, d), lambda i: (0, 0)),
        pl.BlockSpec((m, d), lambda i: (0, 1)),
        pl.BlockSpec((m, d), lambda i: (0, 0)),
        _const_spec(conv_w.shape), _const_spec(wa.shape), _const_spec(wc.shape), _const_spec(wo.shape),
        _const_spec(g.shape), _const_spec(b.shape),
    ]
    return pl.pallas_call(
        functools.partial(_merge_sample_kernel, alpha=alpha),
        out_shape=(jax.ShapeDtypeStruct((m, d), F32), jax.ShapeDtypeStruct((m, dc), F32)),
        grid=(1,),
        in_specs=in_specs,
        out_specs=(pl.BlockSpec((m, d), lambda i: (0, 0)), pl.BlockSpec((m, dc), lambda i: (0, 0))),
        compiler_params=pltpu.CompilerParams(
            dimension_semantics=("arbitrary",), vmem_limit_bytes=VMEM_LIMIT),
        name="merge_sample",
    )(o, conv_in, conv_in, conv_in, s0, s1, gates, gates, x, conv_w, wa, wc, wo, g, b)


def _rope_tables(pos, head_dim, rot_dim):
    half = rot_dim // 2
    inv_freq = jnp.power(ROPE_THETA, -jnp.arange(0, rot_dim, 2, dtype=F32) / rot_dim)
    ang = pos.astype(F32)[:, None] * inv_freq[None, :]
    cos, sin = jnp.cos(ang), jnp.sin(ang)
    n = pos.shape[0]
    ones = jnp.ones((n, head_dim - rot_dim), F32)
    zeros_h = jnp.zeros((n, half), F32)
    zeros_r = jnp.zeros((n, head_dim - rot_dim), F32)
    c = jnp.concatenate([cos, cos, ones], axis=1)
    lo = jnp.concatenate([-sin, zeros_h, zeros_r], axis=1)
    hi = jnp.concatenate([zeros_h, sin, zeros_r], axis=1)
    two = lambda t: jnp.concatenate([t, t], axis=1)
    return two(c), two(lo), two(hi)


def kernel(x_prompt, x_sample, cache_k, cache_v, state_conv, page_table, ln_g, ln_b, ffn1_w_gate, ffn1_w_up, ffn1_w_down, w_in, conv_w, lambda_q1, lambda_k1, lambda_q2, lambda_k2, subln_g, w_attn_out, w_conv_out, w_o, ffn2_w_gate, ffn2_w_up, ffn2_w_down):
    batch, seq, d_model = x_prompt.shape
    dec_batch, dec_seq, _ = x_sample.shape
    assert dec_seq == 1
    depth = ln_g.shape[0]
    page_size, n_heads, v_dim = cache_k.shape[2:]
    head_dim = v_dim // 2
    rot_dim = head_dim // 4
    width = n_heads * v_dim
    d_conv = conv_w.shape[2]
    assert width == d_conv and v_dim == LANES and n_heads == SUBLANES
    past_len = page_table.shape[1] * page_size
    alpha = (2.0 * depth) ** 0.25
    q_scale = head_dim ** -0.5

    tabs_p = _rope_tables(jnp.arange(seq, dtype=jnp.int32), head_dim, rot_dim)
    tabs_s = _rope_tables(jnp.full((dec_batch,), past_len, jnp.int32), head_dim, rot_dim)

    y_p = x_prompt.reshape(batch * seq, d_model)
    y_s = x_sample.reshape(dec_batch, d_model)
    outs = [[] for _ in range(6)]
    for layer in range(depth):
        lam_init = 0.8 - 0.6 * math.exp(-0.3 * layer)
        bf = lambda w: w[layer].astype(BF16)
        wg1, wu1, wd1 = ffn1_w_gate[layer], ffn1_w_up[layer], ffn1_w_down[layer]
        wg2, wu2, wd2 = ffn2_w_gate[layer], ffn2_w_up[layer], ffn2_w_down[layer]
        win, wa, wc, wo = bf(w_in), bf(w_attn_out), bf(w_conv_out), bf(w_o)
        g = [ln_g[layer, i][None, :] for i in range(3)]
        b = [ln_b[layer, i][None, :] for i in range(3)]
        lam_vecs = [v[layer][None, :] for v in (lambda_q1, lambda_k1, lambda_q2, lambda_k2)]
        sg = subln_g[layer][None, :]
        cw = conv_w[layer]

        x1, s1 = _ffn_ln(y_p, y_s, wg1, wu1, wd1, g[0], b[0], alpha=alpha, tm=1024, tf=256)
        q, k, v, kb, vb, conv_in, gates = _inproj(x1, win, *tabs_p, tm=512, tn=width, q_scale=q_scale,
                                                  rot_half=rot_dim // 2, n_heads=n_heads)
        qs, ks, vs, _, _, conv_in_s, gates_s = _inproj(s1, win, *tabs_s, tm=dec_batch, tn=width,
                                                       q_scale=q_scale, rot_half=rot_dim // 2, n_heads=n_heads)

        heads = lambda t: t.reshape(dec_batch, n_heads, v_dim)
        os_, o = _attention(page_table, heads(qs), heads(ks), heads(vs), cache_k[layer], cache_v[layer],
                            q, kb, vb, lam_vecs, sg, batch=batch, seq=seq, lam_init=lam_init,
                            pages_per_step=16, tq=256, row_chunks=1)

        x2, tails = _merge_prompt(o, conv_in, gates, x1, cw, wa, wc, wo, g[1], b[1],
                                  alpha=alpha, tm=256, seq=seq)
        st = state_conv[layer]
        s2, u_s = _merge_sample(os_.reshape(dec_batch, width), conv_in_s, gates_s, st[:, 0], st[:, 1], s1,
                                cw, wa, wc, wo, g[1], b[1], alpha=alpha)
        y_p, y_s = _ffn_ln(x2, s2, wg2, wu2, wd2, g[2], b[2], alpha=alpha, tm=1024, tf=256)
        tails = tails.reshape(batch, seq // 256, SUBLANES, d_conv)
        outs[0].append(k.reshape(batch, seq, n_heads, v_dim))
        outs[1].append(v.reshape(batch, seq, n_heads, v_dim))
        outs[2].append(tails[:, -1, SUBLANES - 2:, :])
        outs[3].append(ks.reshape(dec_batch, 1, n_heads, v_dim))
        outs[4].append(vs.reshape(dec_batch, 1, n_heads, v_dim))
        outs[5].append(jnp.stack([st[:, 1], u_s], axis=1))

    return (y_p.reshape(batch, seq, d_model), y_s.reshape(dec_batch, 1, d_model),
            jnp.stack(outs[0]), jnp.stack(outs[1]), jnp.stack(outs[2]),
            jnp.stack(outs[3]), jnp.stack(outs[4]), jnp.stack(outs[5]))
```

```python
import functools
import math

import jax
import jax.numpy as jnp
from jax import lax
from jax.experimental import pallas as pl
from jax.experimental.pallas import tpu as pltpu

ROPE_THETA = 500000.0
LN_EPS = 1e-5
NEG_INF = -1e30
LANES = 128
SUBLANES = 8
VMEM_LIMIT = 56 * 1024 * 1024
LN_ROWS = 256
SOFTMAX_UNROLL = 4
PAGE_GROUPS = 3
BF16 = jnp.bfloat16
F32 = jnp.float32


def _layer_norm(y, g, b):
    mu = jnp.mean(y, axis=-1, keepdims=True)
    d = y - mu
    var = jnp.mean(d * d, axis=-1, keepdims=True)
    return d * lax.rsqrt(var + LN_EPS) * g + b


def _ffn_ln_kernel(x_ref, xs_ref, wg_ref, wu_ref, wd_ref, g_ref, b_ref, o_ref, os_ref, xb_ref, *, alpha):
    i = pl.program_id(0)
    f = pl.program_id(1)
    nf = pl.num_programs(1)
    tm = x_ref.shape[0]
    first_tile = i == 0

    @pl.when(f == 0)
    def _():
        xb_ref[0:tm, :] = x_ref[...].astype(BF16)
        o_ref[...] = jnp.zeros(o_ref.shape, F32)

    @pl.when(jnp.logical_and(first_tile, f == 0))
    def _():
        xb_ref[tm:, :] = xs_ref[...].astype(BF16)
        os_ref[...] = jnp.zeros(os_ref.shape, F32)

    xb = xb_ref[...]
    hg = jnp.dot(xb, wg_ref[...].astype(BF16), preferred_element_type=F32)
    hu = jnp.dot(xb, wu_ref[...].astype(BF16), preferred_element_type=F32)
    h = (hg * jax.nn.sigmoid(hg) * hu).astype(BF16)
    part = jnp.dot(h, wd_ref[...].astype(BF16), preferred_element_type=F32)
    o_ref[...] += part[:tm]

    @pl.when(first_tile)
    def _():
        os_ref[...] += part[tm:]

    @pl.when(f == nf - 1)
    def _():
        rows = min(tm, LN_ROWS)
        for r in range(tm // rows):
            sl = slice(r * rows, (r + 1) * rows)
            y = alpha * x_ref[sl, :] + 0.5 * o_ref[sl, :]
            o_ref[sl, :] = _layer_norm(y, g_ref[...], b_ref[...])

    @pl.when(jnp.logical_and(first_tile, f == nf - 1))
    def _():
        y = alpha * xs_ref[...] + 0.5 * os_ref[...]
        os_ref[...] = _layer_norm(y, g_ref[...], b_ref[...])


def _ffn_ln(x, xs, wg, wu, wd, g, b, *, alpha, tm, tf):
    m, d = x.shape
    ms = xs.shape[0]
    dff = wg.shape[1]
    return pl.pallas_call(
        functools.partial(_ffn_ln_kernel, alpha=alpha),
        out_shape=(jax.ShapeDtypeStruct((m, d), F32), jax.ShapeDtypeStruct((ms, d), F32)),
        grid=(m // tm, dff // tf),
        in_specs=[
            pl.BlockSpec((tm, d), lambda i, f: (i, 0)),
            pl.BlockSpec((ms, d), lambda i, f: (0, 0)),
            pl.BlockSpec((d, tf), lambda i, f: (0, f)),
            pl.BlockSpec((d, tf), lambda i, f: (0, f)),
            pl.BlockSpec((tf, d), lambda i, f: (f, 0)),
            pl.BlockSpec((1, d), lambda i, f: (0, 0)),
            pl.BlockSpec((1, d), lambda i, f: (0, 0)),
        ],
        out_specs=(pl.BlockSpec((tm, d), lambda i, f: (i, 0)),
                   pl.BlockSpec((ms, d), lambda i, f: (0, 0))),
        scratch_shapes=[pltpu.VMEM((tm + ms, d), BF16)],
        compiler_params=pltpu.CompilerParams(
            dimension_semantics=("arbitrary", "arbitrary"), vmem_limit_bytes=VMEM_LIMIT),
        name="ffn_ln",
    )(x, xs, wg, wu, wd, g, b)


N_QKV_TILES = 3
N_CONV_TILES = 3
def _rope_tile(z, cos, sin_lo, sin_hi, rot_half):
    up = pltpu.roll(z, LANES - rot_half, 1)
    down = pltpu.roll(z, rot_half, 1)
    return z * cos + up * sin_lo + down * sin_hi


def _inproj_kernel(x_ref, w_ref, cos_ref, slo_ref, shi_ref,
                   q_ref, k_ref, v_ref, kb_ref, vb_ref, r_ref, gate_ref, xb_ref,
                   *, q_scale, rot_half, n_heads):
    j = pl.program_id(1)

    @pl.when(j == 0)
    def _():
        xb_ref[...] = x_ref[...].astype(BF16)

    @pl.when(j < N_QKV_TILES + N_CONV_TILES)
    def _():
        r_ref[...] = jnp.dot(xb_ref[...], w_ref[...], preferred_element_type=F32)

    @pl.when(j >= N_QKV_TILES + N_CONV_TILES)
    def _():
        gate_ref[...] = jnp.dot(xb_ref[...], w_ref[...], preferred_element_type=F32).astype(BF16)

    @pl.when(j == 0)
    def _():
        for h in range(n_heads):
            sl = slice(h * LANES, (h + 1) * LANES)
            r = _rope_tile(r_ref[:, sl], cos_ref[...], slo_ref[...], shi_ref[...], rot_half)
            q_ref[:, sl] = (r * q_scale).astype(BF16)

    @pl.when(j == 1)
    def _():
        for h in range(n_heads):
            sl = slice(h * LANES, (h + 1) * LANES)
            r = _rope_tile(r_ref[:, sl], cos_ref[...], slo_ref[...], shi_ref[...], rot_half)
            k_ref[:, sl] = r
            kb_ref[:, sl] = r.astype(BF16)

    @pl.when(j == 2)
    def _():
        v_ref[...] = r_ref[...]
        vb_ref[...] = r_ref[...].astype(BF16)


def _inproj(x, w, cos, slo, shi, *, tm, tn, q_scale, rot_half, n_heads):
    m, d = x.shape
    n = w.shape[1]
    nj = n // tn
    n_first = N_QKV_TILES + N_CONV_TILES
    n_gate_tiles = nj - n_first
    tab_blocks = cos.shape[0] // tm
    tab_spec = pl.BlockSpec((tm, LANES), lambda i, j: (i % tab_blocks, 0))
    row_spec = pl.BlockSpec((tm, tn), lambda i, j: (i, 0))
    return pl.pallas_call(
        functools.partial(_inproj_kernel, q_scale=q_scale, rot_half=rot_half, n_heads=n_heads),
        out_shape=(
            jax.ShapeDtypeStruct((m, tn), BF16),
            jax.ShapeDtypeStruct((m, tn), F32),
            jax.ShapeDtypeStruct((m, tn), F32),
            jax.ShapeDtypeStruct((m, tn), BF16),
            jax.ShapeDtypeStruct((m, tn), BF16),
            jax.ShapeDtypeStruct((m, N_CONV_TILES * tn), F32),
            jax.ShapeDtypeStruct((m, n_gate_tiles * tn), BF16),
        ),
        grid=(m // tm, nj),
        in_specs=[
            pl.BlockSpec((tm, d), lambda i, j: (i, 0)),
            pl.BlockSpec((d, tn), lambda i, j: (0, j)),
            tab_spec, tab_spec, tab_spec,
        ],
        out_specs=(row_spec, row_spec, row_spec, row_spec, row_spec,
                   pl.BlockSpec((tm, tn), lambda i, j: (i, jnp.clip(j - N_QKV_TILES, 0, N_CONV_TILES - 1))),
                   pl.BlockSpec((tm, tn), lambda i, j: (i, jnp.maximum(j - n_first, 0)))),
        scratch_shapes=[pltpu.VMEM((tm, d), BF16)],
        compiler_params=pltpu.CompilerParams(
            dimension_semantics=("arbitrary", "arbitrary"), vmem_limit_bytes=VMEM_LIMIT),
        name="inproj",
    )(x, w, cos, slo, shi)


def _lambda_value(lq1_ref, lk1_ref, lq2_ref, lk2_ref, lam_init):
    s1 = jnp.sum(lq1_ref[...] * lk1_ref[...], axis=-1, keepdims=True)
    s2 = jnp.sum(lq2_ref[...] * lk2_ref[...], axis=-1, keepdims=True)
    return jnp.exp(s1) - jnp.exp(s2) + lam_init


def _head_rmsnorm(o, g, lam_init):
    ms = jnp.mean(o * o, axis=-1, keepdims=True)
    return o * lax.rsqrt(ms + LN_EPS) * g * (1.0 - lam_init)


def _reduce_keys(x, reduce_fn, combine_fn, groups=8):
    n, w = x.shape
    x3 = x.reshape(n // SUBLANES, SUBLANES, w)
    per = x3.shape[0] // groups
    parts = [reduce_fn(x3[g * per:(g + 1) * per], axis=0) for g in range(groups)]
    while len(parts) > 1:
        parts = [combine_fn(parts[i], parts[i + 1]) for i in range(0, len(parts), 2)]
    return reduce_fn(parts[0], axis=0, keepdims=True)


def _prompt_attn_unit(qi, q_ref, k_ref, v_ref, lam, g_ref, o_ref, vt_ref,
                      *, lam_init, nq, row_chunks):
    tq, vd = q_ref.shape
    head_dim = vd // 2
    rc = tq // row_chunks
    seq = k_ref.shape[0]

    @pl.when(qi == 0)
    def _():
        for c in range(seq // tq):
            cols = slice(c * tq, (c + 1) * tq)
            vt_ref[:, cols] = v_ref[cols, :].astype(F32).T.astype(BF16)

    q = q_ref[...]
    lane = lax.broadcasted_iota(jnp.int32, q.shape, 1)
    zero = jnp.zeros_like(q)
    q_sub = (jnp.where(lane < head_dim, q, zero), jnp.where(lane >= head_dim, q, zero))
    key = lax.broadcasted_iota(jnp.int32, (rc, rc), 0)
    qry = lax.broadcasted_iota(jnp.int32, (rc, rc), 1)
    keep = jnp.concatenate([key <= qry, key <= qry], axis=1)

    def softmax_av(qc, ext):
        s = lax.dot_general(k_ref[0:ext, :], qc, (((1,), (1,)), ((), ())), preferred_element_type=F32)
        diag = jnp.where(keep, s[ext - rc:, :], NEG_INF)
        s = diag if ext == rc else jnp.concatenate([s[:ext - rc, :], diag], axis=0)
        m = _reduce_keys(s, jnp.max, jnp.maximum)
        p = jnp.exp(s - m)
        l = _reduce_keys(p, jnp.sum, jnp.add)
        return jnp.dot(vt_ref[:, 0:ext], p.astype(BF16), preferred_element_type=F32) / l

    for i in range(nq):
        @pl.when(qi == i)
        def _(i=i):
            for r in range(row_chunks):
                rows = slice(r * rc, (r + 1) * rc)
                ext = i * tq + (r + 1) * rc
                o = softmax_av(jnp.concatenate([q_sub[0][rows], q_sub[1][rows]], axis=0), ext)
                o = o[:, :rc] - lam * o[:, rc:]
                o_ref[rows, :] = _head_rmsnorm(o.T, g_ref[...], lam_init).astype(BF16)


def _class_reduce(x, op, n_classes):
    shift = n_classes
    while shift < x.shape[-1]:
        x = op(x, pltpu.roll(x, shift, 1))
        shift *= 2
    return x


def _attn_kernel(pt_ref, q_ref, kn_ref, vn_ref, lq1_ref, lk1_ref, lq2_ref, lk2_ref, g_ref,
                 qp_ref, kp_ref, vp_ref, ck_hbm, cv_hbm, o_ref, op_ref,
                 s_scr, a_scr, qt_scr, anew_scr, acc_scr, vt_scr, page_buf, page_sem,
                 *, lam_init, pages_per_step, n_pages, steps_per_unit, nq, row_chunks):
    pp = pages_per_step
    g = pl.program_id(1)
    ng = n_pages // pp
    steps_per_seq = 2 * ng
    n_steps = pl.num_programs(0) * steps_per_seq
    step = pl.program_id(0) * steps_per_seq + g

    def page_copy(src_hbm, page_id, slot):
        return pltpu.make_async_copy(src_hbm.at[page_id], page_buf.at[slot], page_sem.at[slot])

    def fetch_group(t):
        seq_id = t // steps_per_seq
        j = t % steps_per_seq
        is_k = j < ng
        first = seq_id * n_pages + jnp.where(is_k, j, j - ng) * pp
        slot0 = (t % PAGE_GROUPS) * pp

        @pl.when(is_k)
        def _():
            for p in range(pp):
                page_copy(ck_hbm, pt_ref[first + p], slot0 + p).start()

        @pl.when(jnp.logical_not(is_k))
        def _():
            for p in range(pp):
                page_copy(cv_hbm, pt_ref[first + p], slot0 + p).start()

    @pl.when(step == 0)
    def _():
        for t in range(PAGE_GROUPS - 1):
            fetch_group(jnp.int32(t))

    @pl.when(step + PAGE_GROUPS - 1 < n_steps)
    def _():
        fetch_group(step + PAGE_GROUPS - 1)

    @pl.when(step % steps_per_unit == 0)
    def _():
        lam = _lambda_value(lq1_ref, lk1_ref, lq2_ref, lk2_ref, lam_init)
        _prompt_attn_unit((step // steps_per_unit) % nq, qp_ref, kp_ref, vp_ref, lam, g_ref, op_ref, vt_scr,
                          lam_init=lam_init, nq=nq, row_chunks=row_chunks)

    slot0 = (step % PAGE_GROUPS) * pp
    for p in range(pp):
        page_copy(ck_hbm, 0, slot0 + p).wait()

    _, page, n_heads, vd = page_buf.shape
    rows = page * n_heads
    hd = vd // 2
    nt = (((1,), (1,)), ((), ()))

    def pad_rows(x):
        return jnp.concatenate([x, jnp.zeros((LANES - n_heads, vd), x.dtype)], axis=0)

    @pl.when(g == 0)
    def _():
        q = q_ref[...]
        qq = jnp.concatenate([q, q], axis=0)
        row = lax.broadcasted_iota(jnp.int32, qq.shape, 0)
        lane = lax.broadcasted_iota(jnp.int32, qq.shape, 1)
        qt_scr[...] = jnp.where(row // n_heads == lane // hd, qq, jnp.zeros_like(qq))

    @pl.when(g < ng)
    def _():
        for p in range(pp):
            kb = page_buf[slot0 + p].reshape(rows, vd).astype(BF16)
            s_scr[g * pp + p] = lax.dot_general(qt_scr[...], kb, nt, preferred_element_type=F32)

    @pl.when(g == ng)
    def _():
        lane1 = lax.broadcasted_iota(jnp.int32, (2 * n_heads, LANES), 1)
        s_new = lax.dot_general(qt_scr[...], pad_rows(kn_ref[...]).astype(BF16), nt,
                                preferred_element_type=F32)
        s_new = jnp.where(lane1 < n_heads, s_new, NEG_INF)

        def with_new(x, x_new, op):
            return jnp.concatenate([op(x[:, :LANES], x_new), x[:, LANES:]], axis=1)

        m = lax.fori_loop(0, n_pages, lambda i, m: jnp.maximum(m, s_scr[i]),
                          jnp.full((2 * n_heads, rows), NEG_INF, F32), unroll=SOFTMAX_UNROLL)
        m = _class_reduce(with_new(m, s_new, jnp.maximum), jnp.maximum, n_heads)

        def sum_body(i, l):
            p = jnp.exp(s_scr[i] - m)
            s_scr[i] = p
            return l + p
        l = lax.fori_loop(0, n_pages, sum_body, jnp.zeros((2 * n_heads, rows), F32), unroll=SOFTMAX_UNROLL)
        p_new = jnp.exp(s_new - m[:, :LANES])
        l = _class_reduce(with_new(l, p_new, jnp.add), jnp.add, n_heads)

        lam = _lambda_value(lq1_ref, lk1_ref, lq2_ref, lk2_ref, lam_init)
        row = lax.broadcasted_iota(jnp.int32, (n_heads, rows), 0)
        lane = lax.broadcasted_iota(jnp.int32, (n_heads, rows), 1)
        own = row == lane % n_heads

        def a_body(i, c):
            pn = s_scr[i] / l
            a_scr[i] = jnp.where(own, pn[:n_heads] - lam * pn[n_heads:], 0.0)
            return c
        lax.fori_loop(0, n_pages, a_body, 0, unroll=SOFTMAX_UNROLL)
        pn_new = p_new / l[:, :LANES]
        anew_scr[...] = jnp.where(own[:, :LANES], pn_new[:n_heads] - lam * pn_new[n_heads:], 0.0)
        acc_scr[...] = jnp.zeros(acc_scr.shape, F32)

    @pl.when(g >= ng)
    def _():
        acc = acc_scr[...]
        for p in range(pp):
            vb = page_buf[slot0 + p].reshape(rows, vd).astype(BF16)
            acc = acc + jnp.dot(a_scr[(g - ng) * pp + p].astype(BF16), vb, preferred_element_type=F32)
        acc_scr[...] = acc

    @pl.when(g == 2 * ng - 1)
    def _():
        o = acc_scr[...] + jnp.dot(anew_scr[...].astype(BF16), pad_rows(vn_ref[...]).astype(BF16),
                                   preferred_element_type=F32)
        o_ref[...] = _head_rmsnorm(o, g_ref[...], lam_init).astype(o_ref.dtype)


def _attention(page_table, q, k_new, v_new, cache_k, cache_v, qp, kp, vp, lam_vecs, subln_g,
               *, batch, seq, lam_init, pages_per_step, tq, row_chunks):
    nb = q.shape[0]
    n_pages = page_table.shape[1]
    pp = pages_per_step
    ng = n_pages // pp
    _, page, n_heads, vd = cache_k.shape
    rows = page * n_heads
    pt_flat = page_table.reshape(-1)
    nq = seq // tq
    n_units = batch * n_heads * nq
    n_steps = nb * 2 * ng
    steps_per_unit = n_steps // n_units
    assert steps_per_unit * n_units == n_steps

    def unit(b, g):
        u = (b * (2 * ng) + g) // steps_per_unit
        return u // (n_heads * nq), (u // nq) % n_heads, u % nq

    def qp_map(b, g, pt):
        pb, h, i = unit(b, g)
        return pb * nq + i, h

    def kvp_map(b, g, pt):
        pb, h, _ = unit(b, g)
        return pb, h

    head_spec = pl.BlockSpec((None, n_heads, vd), lambda b, g, pt: (b, 0, 0))
    vec = pl.BlockSpec((1, lam_vecs[0].shape[1]), lambda b, g, pt: (0, 0))

    grid_spec = pltpu.PrefetchScalarGridSpec(
        num_scalar_prefetch=1,
        grid=(nb, 2 * ng),
        in_specs=[head_spec, head_spec, head_spec, vec, vec, vec, vec,
                  pl.BlockSpec((1, vd), lambda b, g, pt: (0, 0)),
                  pl.BlockSpec((tq, vd), qp_map),
                  pl.BlockSpec((seq, vd), kvp_map),
                  pl.BlockSpec((seq, vd), kvp_map),
                  pl.BlockSpec(memory_space=pl.ANY),
                  pl.BlockSpec(memory_space=pl.ANY)],
        out_specs=(head_spec, pl.BlockSpec((tq, vd), qp_map)),
        scratch_shapes=[
            pltpu.VMEM((n_pages, 2 * n_heads, rows), F32),
            pltpu.VMEM((n_pages, n_heads, rows), F32),
            pltpu.VMEM((2 * n_heads, vd), BF16),
            pltpu.VMEM((n_heads, LANES), F32),
            pltpu.VMEM((n_heads, vd), F32),
            pltpu.VMEM((vd, seq), BF16),
            pltpu.VMEM((PAGE_GROUPS * pp, page, n_heads, vd), F32),
            pltpu.SemaphoreType.DMA((PAGE_GROUPS * pp,)),
        ],
    )
    return pl.pallas_call(
        functools.partial(_attn_kernel, lam_init=lam_init, pages_per_step=pp, n_pages=n_pages,
                          steps_per_unit=steps_per_unit, nq=nq, row_chunks=row_chunks),
        out_shape=(jax.ShapeDtypeStruct((nb, n_heads, vd), BF16),
                   jax.ShapeDtypeStruct(qp.shape, BF16)),
        grid_spec=grid_spec,
        compiler_params=pltpu.CompilerParams(
            dimension_semantics=("arbitrary", "arbitrary"), vmem_limit_bytes=VMEM_LIMIT),
        name="attention",
    )(pt_flat, q, k_new, v_new, *lam_vecs, subln_g, qp, kp, vp, cache_k, cache_v)


def _mixer_tail(o, cb, conv, ga, gc, x, wa_ref, wc_ref, wo_ref, g_ref, b_ref, alpha):
    ga = ga.astype(F32)
    gc = gc.astype(F32)
    y_attn = jnp.dot(o, wa_ref[...], preferred_element_type=F32)
    y_conv = jnp.dot((cb * conv).astype(BF16), wc_ref[...], preferred_element_type=F32)
    merged = jax.nn.sigmoid(ga) * y_attn + jax.nn.sigmoid(gc) * y_conv
    z = jnp.dot(merged.astype(BF16), wo_ref[...], preferred_element_type=F32)
    return _layer_norm(alpha * x + z, g_ref[...], b_ref[...])


def _merge_prompt_kernel(o_ref, cb_ref, cc_ref, ch_ref, pc_ref, ph_ref, ga_ref, gc_ref, x_ref,
                         cw_ref, wa_ref, wc_ref, wo_ref, g_ref, b_ref,
                         out_ref, tail_ref, u_scr, *, alpha, tiles_per_seq):
    i = pl.program_id(0)
    tm = cc_ref.shape[0]
    u = cc_ref[...] * ch_ref[...]
    prev = pc_ref[...] * ph_ref[...]
    prev = jnp.where(i % tiles_per_seq == 0, jnp.zeros_like(prev), prev)
    u_scr[0:SUBLANES, :] = prev
    u_scr[SUBLANES:SUBLANES + tm, :] = u
    u1 = u_scr[SUBLANES - 1:SUBLANES - 1 + tm, :]
    u2 = u_scr[SUBLANES - 2:SUBLANES - 2 + tm, :]
    conv = cw_ref[0:1, :] * u2 + cw_ref[1:2, :] * u1 + cw_ref[2:3, :] * u
    tail_ref[...] = u[tm - SUBLANES:tm, :]
    out_ref[...] = _mixer_tail(o_ref[...], cb_ref[...], conv, ga_ref[...], gc_ref[...],
                               x_ref[...], wa_ref, wc_ref, wo_ref, g_ref, b_ref, alpha)


def _merge_sample_kernel(o_ref, cb_ref, cc_ref, ch_ref, s0_ref, s1_ref, ga_ref, gc_ref, x_ref,
                         cw_ref, wa_ref, wc_ref, wo_ref, g_ref, b_ref,
                         out_ref, u_ref, *, alpha):
    u = cc_ref[...] * ch_ref[...]
    conv = cw_ref[0:1, :] * s0_ref[...] + cw_ref[1:2, :] * s1_ref[...] + cw_ref[2:3, :] * u
    u_ref[...] = u
    out_ref[...] = _mixer_tail(o_ref[...], cb_ref[...], conv, ga_ref[...], gc_ref[...],
                               x_ref[...], wa_ref, wc_ref, wo_ref, g_ref, b_ref, alpha)


def _const_spec(shape):
    return pl.BlockSpec(shape, lambda i: (0,) * len(shape), pipeline_mode=pl.Buffered(1))


def _merge_prompt(o, conv_in, gates, x, conv_w, wa, wc, wo, g, b, *, alpha, tm, seq):
    m, d = x.shape
    dc = o.shape[1]
    rows8 = tm // SUBLANES
    in_specs = [
        pl.BlockSpec((tm, dc), lambda i: (i, 0)),
        pl.BlockSpec((tm, dc), lambda i: (i, 0)),
        pl.BlockSpec((tm, dc), lambda i: (i, 1)),
        pl.BlockSpec((tm, dc), lambda i: (i, 2)),
        pl.BlockSpec((SUBLANES, dc), lambda i: (jnp.maximum(i * rows8 - 1, 0), 1)),
        pl.BlockSpec((SUBLANES, dc), lambda i: (jnp.maximum(i * rows8 - 1, 0), 2)),
        pl.BlockSpec((tm, d), lambda i: (i, 0)),
        pl.BlockSpec((tm, d), lambda i: (i, 1)),
        pl.BlockSpec((tm, d), lambda i: (i, 0)),
        _const_spec(conv_w.shape), _const_spec(wa.shape), _const_spec(wc.shape), _const_spec(wo.shape),
        _const_spec(g.shape), _const_spec(b.shape),
    ]
    return pl.pallas_call(
        functools.partial(_merge_prompt_kernel, alpha=alpha, tiles_per_seq=seq // tm),
        out_shape=(jax.ShapeDtypeStruct((m, d), F32),
                   jax.ShapeDtypeStruct((m // tm * SUBLANES, dc), F32)),
        grid=(m // tm,),
        in_specs=in_specs,
        out_specs=(pl.BlockSpec((tm, d), lambda i: (i, 0)),
                   pl.BlockSpec((SUBLANES, dc), lambda i: (i, 0))),
        scratch_shapes=[pltpu.VMEM((tm + SUBLANES, dc), F32)],
        compiler_params=pltpu.CompilerParams(
            dimension_semantics=("parallel",), vmem_limit_bytes=VMEM_LIMIT),
        name="merge_prompt",
    )(o, conv_in, conv_in, conv_in, conv_in, conv_in, gates, gates, x, conv_w, wa, wc, wo, g, b)


def _merge_sample(o, conv_in, gates, s0, s1, x, conv_w, wa, wc, wo, g, b, *, alpha):
    m, d = x.shape
    dc = o.shape[1]
    in_specs = [
        pl.BlockSpec((m, dc), lambda i: (0, 0)),
        pl.BlockSpec((m, dc), lambda i: (0, 0)),
        pl.BlockSpec((m, dc), lambda i: (0, 1)),
        pl.BlockSpec((m, dc), lambda i: (0, 2)),
        pl.BlockSpec((m, dc), lambda i: (0, 0)),
        pl.BlockSpec((m, dc), lambda i: (0, 0)),
        pl.BlockSpec((m, d), lambda i: (0, 0)),
        pl.BlockSpec((m, d), lambda i: (0, 1)),
        pl.BlockSpec((m, d), lambda i: (0, 0)),
        _const_spec(conv_w.shape), _const_spec(wa.shape), _const_spec(wc.shape), _const_spec(wo.shape),
        _const_spec(g.shape), _const_spec(b.shape),
    ]
    return pl.pallas_call(
        functools.partial(_merge_sample_kernel, alpha=alpha),
        out_shape=(jax.ShapeDtypeStruct((m, d), F32), jax.ShapeDtypeStruct((m, dc), F32)),
        grid=(1,),
        in_specs=in_specs,
        out_specs=(pl.BlockSpec((m, d), lambda i: (0, 0)), pl.BlockSpec((m, dc), lambda i: (0, 0))),
        compiler_params=pltpu.CompilerParams(
            dimension_semantics=("arbitrary",), vmem_limit_bytes=VMEM_LIMIT),
        name="merge_sample",
    )(o, conv_in, conv_in, conv_in, s0, s1, gates, gates, x, conv_w, wa, wc, wo, g, b)


def _rope_tables(pos, head_dim, rot_dim):
    half = rot_dim // 2
    inv_freq = jnp.power(ROPE_THETA, -jnp.arange(0, rot_dim, 2, dtype=F32) / rot_dim)
    ang = pos.astype(F32)[:, None] * inv_freq[None, :]
    cos, sin = jnp.cos(ang), jnp.sin(ang)
    n = pos.shape[0]
    ones = jnp.ones((n, head_dim - rot_dim), F32)
    zeros_h = jnp.zeros((n, half), F32)
    zeros_r = jnp.zeros((n, head_dim - rot_dim), F32)
    c = jnp.concatenate([cos, cos, ones], axis=1)
    lo = jnp.concatenate([-sin, zeros_h, zeros_r], axis=1)
    hi = jnp.concatenate([zeros_h, sin, zeros_r], axis=1)
    two = lambda t: jnp.concatenate([t, t], axis=1)
    return two(c), two(lo), two(hi)


def kernel(x_prompt, x_sample, cache_k, cache_v, state_conv, page_table, ln_g, ln_b, ffn1_w_gate, ffn1_w_up, ffn1_w_down, w_in, conv_w, lambda_q1, lambda_k1, lambda_q2, lambda_k2, subln_g, w_attn_out, w_conv_out, w_o, ffn2_w_gate, ffn2_w_up, ffn2_w_down):
    batch, seq, d_model = x_prompt.shape
    dec_batch, dec_seq, _ = x_sample.shape
    assert dec_seq == 1
    depth = ln_g.shape[0]
    page_size, n_heads, v_dim = cache_k.shape[2:]
    head_dim = v_dim // 2
    rot_dim = head_dim // 4
    width = n_heads * v_dim
    d_conv = conv_w.shape[2]
    assert width == d_conv and v_dim == LANES and n_heads == SUBLANES
    past_len = page_table.shape[1] * page_size
    alpha = (2.0 * depth) ** 0.25
    q_scale = head_dim ** -0.5

    tabs_p = _rope_tables(jnp.arange(seq, dtype=jnp.int32), head_dim, rot_dim)
    tabs_s = _rope_tables(jnp.full((dec_batch,), past_len, jnp.int32), head_dim, rot_dim)

    y_p = x_prompt.reshape(batch * seq, d_model)
    y_s = x_sample.reshape(dec_batch, d_model)
    outs = [[] for _ in range(6)]
    for layer in range(depth):
        lam_init = 0.8 - 0.6 * math.exp(-0.3 * layer)
        bf = lambda w: w[layer].astype(BF16)
        wg1, wu1, wd1 = ffn1_w_gate[layer], ffn1_w_up[layer], ffn1_w_down[layer]
        wg2, wu2, wd2 = ffn2_w_gate[layer], ffn2_w_up[layer], ffn2_w_down[layer]
        win, wa, wc, wo = bf(w_in), bf(w_attn_out), bf(w_conv_out), bf(w_o)
        g = [ln_g[layer, i][None, :] for i in range(3)]
        b = [ln_b[layer, i][None, :] for i in range(3)]
        lam_vecs = [v[layer][None, :] for v in (lambda_q1, lambda_k1, lambda_q2, lambda_k2)]
        sg = subln_g[layer][None, :]
        cw = conv_w[layer]

        x1, s1 = _ffn_ln(y_p, y_s, wg1, wu1, wd1, g[0], b[0], alpha=alpha, tm=1024, tf=256)
        q, k, v, kb, vb, conv_in, gates = _inproj(x1, win, *tabs_p, tm=512, tn=width, q_scale=q_scale,
                                                  rot_half=rot_dim // 2, n_heads=n_heads)
        qs, ks, vs, _, _, conv_in_s, gates_s = _inproj(s1, win, *tabs_s, tm=dec_batch, tn=width,
                                                       q_scale=q_scale, rot_half=rot_dim // 2, n_heads=n_heads)

        heads = lambda t: t.reshape(dec_batch, n_heads, v_dim)
        os_, o = _attention(page_table, heads(qs), heads(ks), heads(vs), cache_k[layer], cache_v[layer],
                            q, kb, vb, lam_vecs, sg, batch=batch, seq=seq, lam_init=lam_init,
                            pages_per_step=16, tq=512, row_chunks=2)

        x2, tails = _merge_prompt(o, conv_in, gates, x1, cw, wa, wc, wo, g[1], b[1],
                                  alpha=alpha, tm=256, seq=seq)
        st = state_conv[layer]
        s2, u_s = _merge_sample(os_.reshape(dec_batch, width), conv_in_s, gates_s, st[:, 0], st[:, 1], s1,
                                cw, wa, wc, wo, g[1], b[1], alpha=alpha)
        y_p, y_s = _ffn_ln(x2, s2, wg2, wu2, wd2, g[2], b[2], alpha=alpha, tm=1024, tf=256)
        tails = tails.reshape(batch, seq // 256, SUBLANES, d_conv)
        outs[0].append(k.reshape(batch, seq, n_heads, v_dim))
        outs[1].append(v.reshape(batch, seq, n_heads, v_dim))
        outs[2].append(tails[:, -1, SUBLANES - 2:, :])
        outs[3].append(ks.reshape(dec_batch, 1, n_heads, v_dim))
        outs[4].append(vs.reshape(dec_batch, 1, n_heads, v_dim))
        outs[5].append(jnp.stack([st[:, 1], u_s], axis=1))

    return (y_p.reshape(batch, seq, d_model), y_s.reshape(dec_batch, 1, d_model),
            jnp.stack(outs[0]), jnp.stack(outs[1]), jnp.stack(outs[2]),
            jnp.stack(outs[3]), jnp.stack(outs[4]), jnp.stack(outs[5]))
```

```python
import functools
import math

import jax
import jax.numpy as jnp
from jax import lax
from jax.experimental import pallas as pl
from jax.experimental.pallas import tpu as pltpu

ROPE_THETA = 500000.0
LN_EPS = 1e-5
NEG_INF = -1e30
LANES = 128
SUBLANES = 8
VMEM_LIMIT = 56 * 1024 * 1024
LN_ROWS = 256
SOFTMAX_UNROLL = 8
PAGE_GROUPS = 4
BF16 = jnp.bfloat16
F32 = jnp.float32


def _layer_norm(y, g, b):
    mu = jnp.mean(y, axis=-1, keepdims=True)
    d = y - mu
    var = jnp.mean(d * d, axis=-1, keepdims=True)
    return d * lax.rsqrt(var + LN_EPS) * g + b


def _ffn_ln_kernel(x_ref, xs_ref, wg_ref, wu_ref, wd_ref, g_ref, b_ref, o_ref, os_ref, xb_ref, *, alpha):
    i = pl.program_id(0)
    f = pl.program_id(1)
    nf = pl.num_programs(1)
    tm = x_ref.shape[0]
    first_tile = i == 0

    @pl.when(f == 0)
    def _():
        xb_ref[0:tm, :] = x_ref[...].astype(BF16)
        o_ref[...] = jnp.zeros(o_ref.shape, F32)

    @pl.when(jnp.logical_and(first_tile, f == 0))
    def _():
        xb_ref[tm:, :] = xs_ref[...].astype(BF16)
        os_ref[...] = jnp.zeros(os_ref.shape, F32)

    xb = xb_ref[...]
    hg = jnp.dot(xb, wg_ref[...].astype(BF16), preferred_element_type=F32)
    hu = jnp.dot(xb, wu_ref[...].astype(BF16), preferred_element_type=F32)
    h = (hg * jax.nn.sigmoid(hg) * hu).astype(BF16)
    part = jnp.dot(h, wd_ref[...].astype(BF16), preferred_element_type=F32)
    o_ref[...] += part[:tm]

    @pl.when(first_tile)
    def _():
        os_ref[...] += part[tm:]

    @pl.when(f == nf - 1)
    def _():
        rows = min(tm, LN_ROWS)
        for r in range(tm // rows):
            sl = slice(r * rows, (r + 1) * rows)
            y = alpha * x_ref[sl, :] + 0.5 * o_ref[sl, :]
            o_ref[sl, :] = _layer_norm(y, g_ref[...], b_ref[...])

    @pl.when(jnp.logical_and(first_tile, f == nf - 1))
    def _():
        y = alpha * xs_ref[...] + 0.5 * os_ref[...]
        os_ref[...] = _layer_norm(y, g_ref[...], b_ref[...])


def _ffn_ln(x, xs, wg, wu, wd, g, b, *, alpha, tm, tf):
    m, d = x.shape
    ms = xs.shape[0]
    dff = wg.shape[1]
    return pl.pallas_call(
        functools.partial(_ffn_ln_kernel, alpha=alpha),
        out_shape=(jax.ShapeDtypeStruct((m, d), F32), jax.ShapeDtypeStruct((ms, d), F32)),
        grid=(m // tm, dff // tf),
        in_specs=[
            pl.BlockSpec((tm, d), lambda i, f: (i, 0)),
            pl.BlockSpec((ms, d), lambda i, f: (0, 0)),
            pl.BlockSpec((d, tf), lambda i, f: (0, f)),
            pl.BlockSpec((d, tf), lambda i, f: (0, f)),
            pl.BlockSpec((tf, d), lambda i, f: (f, 0)),
            pl.BlockSpec((1, d), lambda i, f: (0, 0)),
            pl.BlockSpec((1, d), lambda i, f: (0, 0)),
        ],
        out_specs=(pl.BlockSpec((tm, d), lambda i, f: (i, 0)),
                   pl.BlockSpec((ms, d), lambda i, f: (0, 0))),
        scratch_shapes=[pltpu.VMEM((tm + ms, d), BF16)],
        compiler_params=pltpu.CompilerParams(
            dimension_semantics=("arbitrary", "arbitrary"), vmem_limit_bytes=VMEM_LIMIT),
        name="ffn_ln",
    )(x, xs, wg, wu, wd, g, b)


N_QKV_TILES = 3
N_CONV_TILES = 3
def _rope_tile(z, cos, sin_lo, sin_hi, rot_half):
    up = pltpu.roll(z, LANES - rot_half, 1)
    down = pltpu.roll(z, rot_half, 1)
    return z * cos + up * sin_lo + down * sin_hi


def _inproj_kernel(x_ref, w_ref, cos_ref, slo_ref, shi_ref,
                   q_ref, k_ref, v_ref, kb_ref, vb_ref, r_ref, gate_ref, xb_ref,
                   *, q_scale, rot_half, n_heads):
    j = pl.program_id(1)

    @pl.when(j == 0)
    def _():
        xb_ref[...] = x_ref[...].astype(BF16)

    @pl.when(j < N_QKV_TILES + N_CONV_TILES)
    def _():
        r_ref[...] = jnp.dot(xb_ref[...], w_ref[...], preferred_element_type=F32)

    @pl.when(j >= N_QKV_TILES + N_CONV_TILES)
    def _():
        gate_ref[...] = jnp.dot(xb_ref[...], w_ref[...], preferred_element_type=F32).astype(BF16)

    @pl.when(j == 0)
    def _():
        for h in range(n_heads):
            sl = slice(h * LANES, (h + 1) * LANES)
            r = _rope_tile(r_ref[:, sl], cos_ref[...], slo_ref[...], shi_ref[...], rot_half)
            q_ref[:, sl] = (r * q_scale).astype(BF16)

    @pl.when(j == 1)
    def _():
        for h in range(n_heads):
            sl = slice(h * LANES, (h + 1) * LANES)
            r = _rope_tile(r_ref[:, sl], cos_ref[...], slo_ref[...], shi_ref[...], rot_half)
            k_ref[:, sl] = r
            kb_ref[:, sl] = r.astype(BF16)

    @pl.when(j == 2)
    def _():
        v_ref[...] = r_ref[...]
        vb_ref[...] = r_ref[...].astype(BF16)


def _inproj(x, w, cos, slo, shi, *, tm, tn, q_scale, rot_half, n_heads):
    m, d = x.shape
    n = w.shape[1]
    nj = n // tn
    n_first = N_QKV_TILES + N_CONV_TILES
    n_gate_tiles = nj - n_first
    tab_blocks = cos.shape[0] // tm
    tab_spec = pl.BlockSpec((tm, LANES), lambda i, j: (i % tab_blocks, 0))
    row_spec = pl.BlockSpec((tm, tn), lambda i, j: (i, 0))
    return pl.pallas_call(
        functools.partial(_inproj_kernel, q_scale=q_scale, rot_half=rot_half, n_heads=n_heads),
        out_shape=(
            jax.ShapeDtypeStruct((m, tn), BF16),
            jax.ShapeDtypeStruct((m, tn), F32),
            jax.ShapeDtypeStruct((m, tn), F32),
            jax.ShapeDtypeStruct((m, tn), BF16),
            jax.ShapeDtypeStruct((m, tn), BF16),
            jax.ShapeDtypeStruct((m, N_CONV_TILES * tn), F32),
            jax.ShapeDtypeStruct((m, n_gate_tiles * tn), BF16),
        ),
        grid=(m // tm, nj),
        in_specs=[
            pl.BlockSpec((tm, d), lambda i, j: (i, 0)),
            pl.BlockSpec((d, tn), lambda i, j: (0, j)),
            tab_spec, tab_spec, tab_spec,
        ],
        out_specs=(row_spec, row_spec, row_spec, row_spec, row_spec,
                   pl.BlockSpec((tm, tn), lambda i, j: (i, jnp.clip(j - N_QKV_TILES, 0, N_CONV_TILES - 1))),
                   pl.BlockSpec((tm, tn), lambda i, j: (i, jnp.maximum(j - n_first, 0)))),
        scratch_shapes=[pltpu.VMEM((tm, d), BF16)],
        compiler_params=pltpu.CompilerParams(
            dimension_semantics=("arbitrary", "arbitrary"), vmem_limit_bytes=VMEM_LIMIT),
        name="inproj",
    )(x, w, cos, slo, shi)


def _lambda_value(lq1_ref, lk1_ref, lq2_ref, lk2_ref, lam_init):
    s1 = jnp.sum(lq1_ref[...] * lk1_ref[...], axis=-1, keepdims=True)
    s2 = jnp.sum(lq2_ref[...] * lk2_ref[...], axis=-1, keepdims=True)
    return jnp.exp(s1) - jnp.exp(s2) + lam_init


def _head_rmsnorm(o, g, lam_init):
    ms = jnp.mean(o * o, axis=-1, keepdims=True)
    return o * lax.rsqrt(ms + LN_EPS) * g * (1.0 - lam_init)


def _reduce_keys(x, reduce_fn, combine_fn, groups=8):
    n, w = x.shape
    x3 = x.reshape(n // SUBLANES, SUBLANES, w)
    per = x3.shape[0] // groups
    parts = [reduce_fn(x3[g * per:(g + 1) * per], axis=0) for g in range(groups)]
    while len(parts) > 1:
        parts = [combine_fn(parts[i], parts[i + 1]) for i in range(0, len(parts), 2)]
    return reduce_fn(parts[0], axis=0, keepdims=True)


def _prompt_attn_unit(qi, q_ref, k_ref, v_ref, lam, g_ref, o_ref, vt_ref,
                      *, lam_init, nq, row_chunks):
    tq, vd = q_ref.shape
    head_dim = vd // 2
    rc = tq // row_chunks
    seq = k_ref.shape[0]

    @pl.when(qi == 0)
    def _():
        for c in range(seq // tq):
            cols = slice(c * tq, (c + 1) * tq)
            vt_ref[:, cols] = v_ref[cols, :].astype(F32).T.astype(BF16)

    q = q_ref[...]
    lane = lax.broadcasted_iota(jnp.int32, q.shape, 1)
    zero = jnp.zeros_like(q)
    q_sub = (jnp.where(lane < head_dim, q, zero), jnp.where(lane >= head_dim, q, zero))
    key = lax.broadcasted_iota(jnp.int32, (rc, rc), 0)
    qry = lax.broadcasted_iota(jnp.int32, (rc, rc), 1)
    keep = jnp.concatenate([key <= qry, key <= qry], axis=1)

    def softmax_av(qc, ext):
        s = lax.dot_general(k_ref[0:ext, :], qc, (((1,), (1,)), ((), ())), preferred_element_type=F32)
        diag = jnp.where(keep, s[ext - rc:, :], NEG_INF)
        s = diag if ext == rc else jnp.concatenate([s[:ext - rc, :], diag], axis=0)
        m = _reduce_keys(s, jnp.max, jnp.maximum)
        p = jnp.exp(s - m)
        l = _reduce_keys(p, jnp.sum, jnp.add)
        return jnp.dot(vt_ref[:, 0:ext], p.astype(BF16), preferred_element_type=F32) / l

    for i in range(nq):
        @pl.when(qi == i)
        def _(i=i):
            for r in range(row_chunks):
                rows = slice(r * rc, (r + 1) * rc)
                ext = i * tq + (r + 1) * rc
                o = softmax_av(jnp.concatenate([q_sub[0][rows], q_sub[1][rows]], axis=0), ext)
                o = o[:, :rc] - lam * o[:, rc:]
                o_ref[rows, :] = _head_rmsnorm(o.T, g_ref[...], lam_init).astype(BF16)


def _class_reduce(x, op, n_classes):
    shift = n_classes
    while shift < x.shape[-1]:
        x = op(x, pltpu.roll(x, shift, 1))
        shift *= 2
    return x


def _attn_kernel(pt_ref, q_ref, kn_ref, vn_ref, lq1_ref, lk1_ref, lq2_ref, lk2_ref, g_ref,
                 qp_ref, kp_ref, vp_ref, ck_hbm, cv_hbm, o_ref, op_ref,
                 s_scr, a_scr, qt_scr, anew_scr, acc_scr, vt_scr, page_buf, page_sem,
                 *, lam_init, pages_per_step, n_pages, steps_per_unit, nq, row_chunks):
    pp = pages_per_step
    g = pl.program_id(1)
    ng = n_pages // pp
    steps_per_seq = 2 * ng
    n_steps = pl.num_programs(0) * steps_per_seq
    step = pl.program_id(0) * steps_per_seq + g

    def page_copy(src_hbm, page_id, slot):
        return pltpu.make_async_copy(src_hbm.at[page_id], page_buf.at[slot], page_sem.at[slot])

    def fetch_group(t):
        seq_id = t // steps_per_seq
        j = t % steps_per_seq
        is_k = j < ng
        first = seq_id * n_pages + jnp.where(is_k, j, j - ng) * pp
        slot0 = (t % PAGE_GROUPS) * pp

        @pl.when(is_k)
        def _():
            for p in range(pp):
                page_copy(ck_hbm, pt_ref[first + p], slot0 + p).start()

        @pl.when(jnp.logical_not(is_k))
        def _():
            for p in range(pp):
                page_copy(cv_hbm, pt_ref[first + p], slot0 + p).start()

    @pl.when(step == 0)
    def _():
        for t in range(PAGE_GROUPS - 1):
            fetch_group(jnp.int32(t))

    @pl.when(step + PAGE_GROUPS - 1 < n_steps)
    def _():
        fetch_group(step + PAGE_GROUPS - 1)

    @pl.when(step % steps_per_unit == 0)
    def _():
        lam = _lambda_value(lq1_ref, lk1_ref, lq2_ref, lk2_ref, lam_init)
        _prompt_attn_unit((step // steps_per_unit) % nq, qp_ref, kp_ref, vp_ref, lam, g_ref, op_ref, vt_scr,
                          lam_init=lam_init, nq=nq, row_chunks=row_chunks)

    slot0 = (step % PAGE_GROUPS) * pp
    for p in range(pp):
        page_copy(ck_hbm, 0, slot0 + p).wait()

    _, page, n_heads, vd = page_buf.shape
    rows = page * n_heads
    hd = vd // 2
    nt = (((1,), (1,)), ((), ()))

    def pad_rows(x):
        return jnp.concatenate([x, jnp.zeros((LANES - n_heads, vd), x.dtype)], axis=0)

    @pl.when(g == 0)
    def _():
        q = q_ref[...]
        qq = jnp.concatenate([q, q], axis=0)
        row = lax.broadcasted_iota(jnp.int32, qq.shape, 0)
        lane = lax.broadcasted_iota(jnp.int32, qq.shape, 1)
        qt_scr[...] = jnp.where(row // n_heads == lane // hd, qq, jnp.zeros_like(qq))

    @pl.when(g < ng)
    def _():
        for p in range(pp):
            kb = page_buf[slot0 + p].reshape(rows, vd).astype(BF16)
            s_scr[g * pp + p] = lax.dot_general(qt_scr[...], kb, nt, preferred_element_type=F32)

    @pl.when(g == ng)
    def _():
        lane1 = lax.broadcasted_iota(jnp.int32, (2 * n_heads, LANES), 1)
        s_new = lax.dot_general(qt_scr[...], pad_rows(kn_ref[...]).astype(BF16), nt,
                                preferred_element_type=F32)
        s_new = jnp.where(lane1 < n_heads, s_new, NEG_INF)

        def with_new(x, x_new, op):
            return jnp.concatenate([op(x[:, :LANES], x_new), x[:, LANES:]], axis=1)

        m = lax.fori_loop(0, n_pages, lambda i, m: jnp.maximum(m, s_scr[i]),
                          jnp.full((2 * n_heads, rows), NEG_INF, F32), unroll=SOFTMAX_UNROLL)
        m = _class_reduce(with_new(m, s_new, jnp.maximum), jnp.maximum, n_heads)

        def sum_body(i, l):
            p = jnp.exp(s_scr[i] - m)
            s_scr[i] = p
            return l + p
        l = lax.fori_loop(0, n_pages, sum_body, jnp.zeros((2 * n_heads, rows), F32), unroll=SOFTMAX_UNROLL)
        p_new = jnp.exp(s_new - m[:, :LANES])
        l = _class_reduce(with_new(l, p_new, jnp.add), jnp.add, n_heads)

        lam = _lambda_value(lq1_ref, lk1_ref, lq2_ref, lk2_ref, lam_init)
        row = lax.broadcasted_iota(jnp.int32, (n_heads, rows), 0)
        lane = lax.broadcasted_iota(jnp.int32, (n_heads, rows), 1)
        own = row == lane % n_heads

        def a_body(i, c):
            pn = s_scr[i] / l
            a_scr[i] = jnp.where(own, pn[:n_heads] - lam * pn[n_heads:], 0.0)
            return c
        lax.fori_loop(0, n_pages, a_body, 0, unroll=SOFTMAX_UNROLL)
        pn_new = p_new / l[:, :LANES]
        anew_scr[...] = jnp.where(own[:, :LANES], pn_new[:n_heads] - lam * pn_new[n_heads:], 0.0)
        acc_scr[...] = jnp.zeros(acc_scr.shape, F32)

    @pl.when(g >= ng)
    def _():
        acc = acc_scr[...]
        for p in range(pp):
            vb = page_buf[slot0 + p].reshape(rows, vd).astype(BF16)
            acc = acc + jnp.dot(a_scr[(g - ng) * pp + p].astype(BF16), vb, preferred_element_type=F32)
        acc_scr[...] = acc

    @pl.when(g == 2 * ng - 1)
    def _():
        o = acc_scr[...] + jnp.dot(anew_scr[...].astype(BF16), pad_rows(vn_ref[...]).astype(BF16),
                                   preferred_element_type=F32)
        o_ref[...] = _head_rmsnorm(o, g_ref[...], lam_init).astype(o_ref.dtype)


def _attention(page_table, q, k_new, v_new, cache_k, cache_v, qp, kp, vp, lam_vecs, subln_g,
               *, batch, seq, lam_init, pages_per_step, tq, row_chunks):
    nb = q.shape[0]
    n_pages = page_table.shape[1]
    pp = pages_per_step
    ng = n_pages // pp
    _, page, n_heads, vd = cache_k.shape
    rows = page * n_heads
    pt_flat = page_table.reshape(-1)
    nq = seq // tq
    n_units = batch * n_heads * nq
    n_steps = nb * 2 * ng
    steps_per_unit = n_steps // n_units
    assert steps_per_unit * n_units == n_steps

    def unit(b, g):
        u = (b * (2 * ng) + g) // steps_per_unit
        return u // (n_heads * nq), (u // nq) % n_heads, u % nq

    def qp_map(b, g, pt):
        pb, h, i = unit(b, g)
        return pb * nq + i, h

    def kvp_map(b, g, pt):
        pb, h, _ = unit(b, g)
        return pb, h

    head_spec = pl.BlockSpec((None, n_heads, vd), lambda b, g, pt: (b, 0, 0))
    vec = pl.BlockSpec((1, lam_vecs[0].shape[1]), lambda b, g, pt: (0, 0))

    grid_spec = pltpu.PrefetchScalarGridSpec(
        num_scalar_prefetch=1,
        grid=(nb, 2 * ng),
        in_specs=[head_spec, head_spec, head_spec, vec, vec, vec, vec,
                  pl.BlockSpec((1, vd), lambda b, g, pt: (0, 0)),
                  pl.BlockSpec((tq, vd), qp_map),
                  pl.BlockSpec((seq, vd), kvp_map),
                  pl.BlockSpec((seq, vd), kvp_map),
                  pl.BlockSpec(memory_space=pl.ANY),
                  pl.BlockSpec(memory_space=pl.ANY)],
        out_specs=(head_spec, pl.BlockSpec((tq, vd), qp_map)),
        scratch_shapes=[
            pltpu.VMEM((n_pages, 2 * n_heads, rows), F32),
            pltpu.VMEM((n_pages, n_heads, rows), F32),
            pltpu.VMEM((2 * n_heads, vd), BF16),
            pltpu.VMEM((n_heads, LANES), F32),
            pltpu.VMEM((n_heads, vd), F32),
            pltpu.VMEM((vd, seq), BF16),
            pltpu.VMEM((PAGE_GROUPS * pp, page, n_heads, vd), F32),
            pltpu.SemaphoreType.DMA((PAGE_GROUPS * pp,)),
        ],
    )
    return pl.pallas_call(
        functools.partial(_attn_kernel, lam_init=lam_init, pages_per_step=pp, n_pages=n_pages,
                          steps_per_unit=steps_per_unit, nq=nq, row_chunks=row_chunks),
        out_shape=(jax.ShapeDtypeStruct((nb, n_heads, vd), BF16),
                   jax.ShapeDtypeStruct(qp.shape, BF16)),
        grid_spec=grid_spec,
        compiler_params=pltpu.CompilerParams(
            dimension_semantics=("arbitrary", "arbitrary"), vmem_limit_bytes=VMEM_LIMIT),
        name="attention",
    )(pt_flat, q, k_new, v_new, *lam_vecs, subln_g, qp, kp, vp, cache_k, cache_v)


def _mixer_tail(o, cb, conv, ga, gc, x, wa_ref, wc_ref, wo_ref, g_ref, b_ref, alpha):
    ga = ga.astype(F32)
    gc = gc.astype(F32)
    y_attn = jnp.dot(o, wa_ref[...], preferred_element_type=F32)
    y_conv = jnp.dot((cb * conv).astype(BF16), wc_ref[...], preferred_element_type=F32)
    merged = jax.nn.sigmoid(ga) * y_attn + jax.nn.sigmoid(gc) * y_conv
    z = jnp.dot(merged.astype(BF16), wo_ref[...], preferred_element_type=F32)
    return _layer_norm(alpha * x + z, g_ref[...], b_ref[...])


def _merge_prompt_kernel(o_ref, cb_ref, cc_ref, ch_ref, pc_ref, ph_ref, ga_ref, gc_ref, x_ref,
                         cw_ref, wa_ref, wc_ref, wo_ref, g_ref, b_ref,
                         out_ref, tail_ref, u_scr, *, alpha, tiles_per_seq):
    i = pl.program_id(0)
    tm = cc_ref.shape[0]
    u = cc_ref[...] * ch_ref[...]
    prev = pc_ref[...] * ph_ref[...]
    prev = jnp.where(i % tiles_per_seq == 0, jnp.zeros_like(prev), prev)
    u_scr[0:SUBLANES, :] = prev
    u_scr[SUBLANES:SUBLANES + tm, :] = u
    u1 = u_scr[SUBLANES - 1:SUBLANES - 1 + tm, :]
    u2 = u_scr[SUBLANES - 2:SUBLANES - 2 + tm, :]
    conv = cw_ref[0:1, :] * u2 + cw_ref[1:2, :] * u1 + cw_ref[2:3, :] * u
    tail_ref[...] = u[tm - SUBLANES:tm, :]
    out_ref[...] = _mixer_tail(o_ref[...], cb_ref[...], conv, ga_ref[...], gc_ref[...],
                               x_ref[...], wa_ref, wc_ref, wo_ref, g_ref, b_ref, alpha)


def _merge_sample_kernel(o_ref, cb_ref, cc_ref, ch_ref, s0_ref, s1_ref, ga_ref, gc_ref, x_ref,
                         cw_ref, wa_ref, wc_ref, wo_ref, g_ref, b_ref,
                         out_ref, u_ref, *, alpha):
    u = cc_ref[...] * ch_ref[...]
    conv = cw_ref[0:1, :] * s0_ref[...] + cw_ref[1:2, :] * s1_ref[...] + cw_ref[2:3, :] * u
    u_ref[...] = u
    out_ref[...] = _mixer_tail(o_ref[...], cb_ref[...], conv, ga_ref[...], gc_ref[...],
                               x_ref[...], wa_ref, wc_ref, wo_ref, g_ref, b_ref, alpha)


def _const_spec(shape):
    return pl.BlockSpec(shape, lambda i: (0,) * len(shape), pipeline_mode=pl.Buffered(1))


def _merge_prompt(o, conv_in, gates, x, conv_w, wa, wc, wo, g, b, *, alpha, tm, seq):
    m, d = x.shape
    dc = o.shape[1]
    rows8 = tm // SUBLANES
    in_specs = [
        pl.BlockSpec((tm, dc), lambda i: (i, 0)),
        pl.BlockSpec((tm, dc), lambda i: (i, 0)),
        pl.BlockSpec((tm, dc), lambda i: (i, 1)),
        pl.BlockSpec((tm, dc), lambda i: (i, 2)),
        pl.BlockSpec((SUBLANES, dc), lambda i: (jnp.maximum(i * rows8 - 1, 0), 1)),
        pl.BlockSpec((SUBLANES, dc), lambda i: (jnp.maximum(i * rows8 - 1, 0), 2)),
        pl.BlockSpec((tm, d), lambda i: (i, 0)),
        pl.BlockSpec((tm, d), lambda i: (i, 1)),
        pl.BlockSpec((tm, d), lambda i: (i, 0)),
        _const_spec(conv_w.shape), _const_spec(wa.shape), _const_spec(wc.shape), _const_spec(wo.shape),
        _const_spec(g.shape), _const_spec(b.shape),
    ]
    return pl.pallas_call(
        functools.partial(_merge_prompt_kernel, alpha=alpha, tiles_per_seq=seq // tm),
        out_shape=(jax.ShapeDtypeStruct((m, d), F32),
                   jax.ShapeDtypeStruct((m // tm * SUBLANES, dc), F32)),
        grid=(m // tm,),
        in_specs=in_specs,
        out_specs=(pl.BlockSpec((tm, d), lambda i: (i, 0)),
                   pl.BlockSpec((SUBLANES, dc), lambda i: (i, 0))),
        scratch_shapes=[pltpu.VMEM((tm + SUBLANES, dc), F32)],
        compiler_params=pltpu.CompilerParams(
            dimension_semantics=("parallel",), vmem_limit_bytes=VMEM_LIMIT),
        name="merge_prompt",
    )(o, conv_in, conv_in, conv_in, conv_in, conv_in, gates, gates, x, conv_w, wa, wc, wo, g, b)


def _merge_sample(o, conv_in, gates, s0, s1, x, conv_w, wa, wc, wo, g, b, *, alpha):
    m, d = x.shape
    dc = o.shape[1]
    in_specs = [
        pl.BlockSpec((m, dc), lambda i: (0, 0)),
        pl.BlockSpec((m, dc), lambda i: (0, 0)),
        pl.BlockSpec((m, dc), lambda i: (0, 1)),
        pl.BlockSpec((m, dc), lambda i: (0, 2)),
        pl.BlockSpec((m, dc), lambda i: (0, 0)),
        pl.BlockSpec((m, dc), lambda i: (0, 0)),
        pl.BlockSpec((m, d), lambda i: (0, 0)),
        pl.BlockSpec((m, d), lambda i: (0, 1)),
        pl.BlockSpec((m, d), lambda i: (0, 0)),
        _const_spec(conv_w.shape), _const_spec(wa.shape), _const_spec(wc.shape), _const_spec(wo.shape),
        _const_spec(g.shape), _const_spec(b.shape),
    ]
    return pl.pallas_call(
        functools.partial(_merge_sample_kernel, alpha=alpha),
        out_shape=(jax.ShapeDtypeStruct((m, d), F32), jax.ShapeDtypeStruct((m, dc), F32)),
        grid=(1,),
        in_specs=in_specs,
        out_specs=(pl.BlockSpec((m, d), lambda i: (0, 0)), pl.BlockSpec((m, dc), lambda i: (0, 0))),
        compiler_params=pltpu.CompilerParams(
            dimension_semantics=("arbitrary",), vmem_limit_bytes=VMEM_LIMIT),
        name="merge_sample",
    )(o, conv_in, conv_in, conv_in, s0, s1, gates, gates, x, conv_w, wa, wc, wo, g, b)


def _rope_tables(pos, head_dim, rot_dim):
    half = rot_dim // 2
    inv_freq = jnp.power(ROPE_THETA, -jnp.arange(0, rot_dim, 2, dtype=F32) / rot_dim)
    ang = pos.astype(F32)[:, None] * inv_freq[None, :]
    cos, sin = jnp.cos(ang), jnp.sin(ang)
    n = pos.shape[0]
    ones = jnp.ones((n, head_dim - rot_dim), F32)
    zeros_h = jnp.zeros((n, half), F32)
    zeros_r = jnp.zeros((n, head_dim - rot_dim), F32)
    c = jnp.concatenate([cos, cos, ones], axis=1)
    lo = jnp.concatenate([-sin, zeros_h, zeros_r], axis=1)
    hi = jnp.concatenate([zeros_h, sin, zeros_r], axis=1)
    two = lambda t: jnp.concatenate([t, t], axis=1)
    return two(c), two(lo), two(hi)


def kernel(x_prompt, x_sample, cache_k, cache_v, state_conv, page_table, ln_g, ln_b, ffn1_w_gate, ffn1_w_up, ffn1_w_down, w_in, conv_w, lambda_q1, lambda_k1, lambda_q2, lambda_k2, subln_g, w_attn_out, w_conv_out, w_o, ffn2_w_gate, ffn2_w_up, ffn2_w_down):
    batch, seq, d_model = x_prompt.shape
    dec_batch, dec_seq, _ = x_sample.shape
    assert dec_seq == 1
    depth = ln_g.shape[0]
    page_size, n_heads, v_dim = cache_k.shape[2:]
    head_dim = v_dim // 2
    rot_dim = head_dim // 4
    width = n_heads * v_dim
    d_conv = conv_w.shape[2]
    assert width == d_conv and v_dim == LANES and n_heads == SUBLANES
    past_len = page_table.shape[1] * page_size
    alpha = (2.0 * depth) ** 0.25
    q_scale = head_dim ** -0.5

    tabs_p = _rope_tables(jnp.arange(seq, dtype=jnp.int32), head_dim, rot_dim)
    tabs_s = _rope_tables(jnp.full((dec_batch,), past_len, jnp.int32), head_dim, rot_dim)

    y_p = x_prompt.reshape(batch * seq, d_model)
    y_s = x_sample.reshape(dec_batch, d_model)
    outs = [[] for _ in range(6)]
    for layer in range(depth):
        lam_init = 0.8 - 0.6 * math.exp(-0.3 * layer)
        bf = lambda w: w[layer].astype(BF16)
        wg1, wu1, wd1 = ffn1_w_gate[layer], ffn1_w_up[layer], ffn1_w_down[layer]
        wg2, wu2, wd2 = ffn2_w_gate[layer], ffn2_w_up[layer], ffn2_w_down[layer]
        win, wa, wc, wo = bf(w_in), bf(w_attn_out), bf(w_conv_out), bf(w_o)
        g = [ln_g[layer, i][None, :] for i in range(3)]
        b = [ln_b[layer, i][None, :] for i in range(3)]
        lam_vecs = [v[layer][None, :] for v in (lambda_q1, lambda_k1, lambda_q2, lambda_k2)]
        sg = subln_g[layer][None, :]
        cw = conv_w[layer]

        x1, s1 = _ffn_ln(y_p, y_s, wg1, wu1, wd1, g[0], b[0], alpha=alpha, tm=1024, tf=256)
        q, k, v, kb, vb, conv_in, gates = _inproj(x1, win, *tabs_p, tm=512, tn=width, q_scale=q_scale,
                                                  rot_half=rot_dim // 2, n_heads=n_heads)
        qs, ks, vs, _, _, conv_in_s, gates_s = _inproj(s1, win, *tabs_s, tm=dec_batch, tn=width,
                                                       q_scale=q_scale, rot_half=rot_dim // 2, n_heads=n_heads)

        heads = lambda t: t.reshape(dec_batch, n_heads, v_dim)
        os_, o = _attention(page_table, heads(qs), heads(ks), heads(vs), cache_k[layer], cache_v[layer],
                            q, kb, vb, lam_vecs, sg, batch=batch, seq=seq, lam_init=lam_init,
                            pages_per_step=16, tq=512, row_chunks=2)

        x2, tails = _merge_prompt(o, conv_in, gates, x1, cw, wa, wc, wo, g[1], b[1],
                                  alpha=alpha, tm=256, seq=seq)
        st = state_conv[layer]
        s2, u_s = _merge_sample(os_.reshape(dec_batch, width), conv_in_s, gates_s, st[:, 0], st[:, 1], s1,
                                cw, wa, wc, wo, g[1], b[1], alpha=alpha)
        y_p, y_s = _ffn_ln(x2, s2, wg2, wu2, wd2, g[2], b[2], alpha=alpha, tm=1024, tf=256)
        tails = tails.reshape(batch, seq // 256, SUBLANES, d_conv)
        outs[0].append(k.reshape(batch, seq, n_heads, v_dim))
        outs[1].append(v.reshape(batch, seq, n_heads, v_dim))
        outs[2].append(tails[:, -1, SUBLANES - 2:, :])
        outs[3].append(ks.reshape(dec_batch, 1, n_heads, v_dim))
        outs[4].append(vs.reshape(dec_batch, 1, n_heads, v_dim))
        outs[5].append(jnp.stack([st[:, 1], u_s], axis=1))

    return (y_p.reshape(batch, seq, d_model), y_s.reshape(dec_batch, 1, d_model),
            jnp.stack(outs[0]), jnp.stack(outs[1]), jnp.stack(outs[2]),
            jnp.stack(outs[3]), jnp.stack(outs[4]), jnp.stack(outs[5]))
```

```python
import functools
import math

import jax
import jax.numpy as jnp
from jax import lax
from jax.experimental import pallas as pl
from jax.experimental.pallas import tpu as pltpu

ROPE_THETA = 500000.0
LN_EPS = 1e-5
NEG_INF = -1e30
LANES = 128
SUBLANES = 8
VMEM_LIMIT = 56 * 1024 * 1024
LN_ROWS = 256
SOFTMAX_UNROLL = 8
PAGE_GROUPS = 4
BF16 = jnp.bfloat16
F32 = jnp.float32


def _layer_norm(y, g, b):
    mu = jnp.mean(y, axis=-1, keepdims=True)
    d = y - mu
    var = jnp.mean(d * d, axis=-1, keepdims=True)
    return d * lax.rsqrt(var + LN_EPS) * g + b


def _ffn_ln_kernel(x_ref, xs_ref, wg_ref, wu_ref, wd_ref, g_ref, b_ref, o_ref, os_ref, xb_ref, *, alpha):
    i = pl.program_id(0)
    f = pl.program_id(1)
    nf = pl.num_programs(1)
    tm = x_ref.shape[0]
    first_tile = i == 0

    @pl.when(f == 0)
    def _():
        xb_ref[0:tm, :] = x_ref[...].astype(BF16)
        o_ref[...] = jnp.zeros(o_ref.shape, F32)

    @pl.when(jnp.logical_and(first_tile, f == 0))
    def _():
        xb_ref[tm:, :] = xs_ref[...].astype(BF16)
        os_ref[...] = jnp.zeros(os_ref.shape, F32)

    xb = xb_ref[...]
    hg = jnp.dot(xb, wg_ref[...].astype(BF16), preferred_element_type=F32)
    hu = jnp.dot(xb, wu_ref[...].astype(BF16), preferred_element_type=F32)
    h = (hg * jax.nn.sigmoid(hg) * hu).astype(BF16)
    part = jnp.dot(h, wd_ref[...].astype(BF16), preferred_element_type=F32)
    o_ref[...] += part[:tm]

    @pl.when(first_tile)
    def _():
        os_ref[...] += part[tm:]

    @pl.when(f == nf - 1)
    def _():
        rows = min(tm, LN_ROWS)
        for r in range(tm // rows):
            sl = slice(r * rows, (r + 1) * rows)
            y = alpha * x_ref[sl, :] + 0.5 * o_ref[sl, :]
            o_ref[sl, :] = _layer_norm(y, g_ref[...], b_ref[...])

    @pl.when(jnp.logical_and(first_tile, f == nf - 1))
    def _():
        y = alpha * xs_ref[...] + 0.5 * os_ref[...]
        os_ref[...] = _layer_norm(y, g_ref[...], b_ref[...])


def _ffn_ln(x, xs, wg, wu, wd, g, b, *, alpha, tm, tf):
    m, d = x.shape
    ms = xs.shape[0]
    dff = wg.shape[1]
    return pl.pallas_call(
        functools.partial(_ffn_ln_kernel, alpha=alpha),
        out_shape=(jax.ShapeDtypeStruct((m, d), F32), jax.ShapeDtypeStruct((ms, d), F32)),
        grid=(m // tm, dff // tf),
        in_specs=[
            pl.BlockSpec((tm, d), lambda i, f: (i, 0)),
            pl.BlockSpec((ms, d), lambda i, f: (0, 0)),
            pl.BlockSpec((d, tf), lambda i, f: (0, f)),
            pl.BlockSpec((d, tf), lambda i, f: (0, f)),
            pl.BlockSpec((tf, d), lambda i, f: (f, 0)),
            pl.BlockSpec((1, d), lambda i, f: (0, 0)),
            pl.BlockSpec((1, d), lambda i, f: (0, 0)),
        ],
        out_specs=(pl.BlockSpec((tm, d), lambda i, f: (i, 0)),
                   pl.BlockSpec((ms, d), lambda i, f: (0, 0))),
        scratch_shapes=[pltpu.VMEM((tm + ms, d), BF16)],
        compiler_params=pltpu.CompilerParams(
            dimension_semantics=("arbitrary", "arbitrary"), vmem_limit_bytes=VMEM_LIMIT),
        name="ffn_ln",
    )(x, xs, wg, wu, wd, g, b)


N_QKV_TILES = 3
N_CONV_TILES = 3
def _rope_tile(z, cos, sin_lo, sin_hi, rot_half):
    up = pltpu.roll(z, LANES - rot_half, 1)
    down = pltpu.roll(z, rot_half, 1)
    return z * cos + up * sin_lo + down * sin_hi


def _inproj_kernel(x_ref, w_ref, cos_ref, slo_ref, shi_ref,
                   q_ref, k_ref, v_ref, kb_ref, vb_ref, r_ref, gate_ref, xb_ref,
                   *, q_scale, rot_half, n_heads):
    j = pl.program_id(1)

    @pl.when(j == 0)
    def _():
        xb_ref[...] = x_ref[...].astype(BF16)

    @pl.when(j < N_QKV_TILES + N_CONV_TILES)
    def _():
        r_ref[...] = jnp.dot(xb_ref[...], w_ref[...], preferred_element_type=F32)

    @pl.when(j >= N_QKV_TILES + N_CONV_TILES)
    def _():
        gate_ref[...] = jnp.dot(xb_ref[...], w_ref[...], preferred_element_type=F32).astype(BF16)

    @pl.when(j == 0)
    def _():
        for h in range(n_heads):
            sl = slice(h * LANES, (h + 1) * LANES)
            r = _rope_tile(r_ref[:, sl], cos_ref[...], slo_ref[...], shi_ref[...], rot_half)
            q_ref[:, sl] = (r * q_scale).astype(BF16)

    @pl.when(j == 1)
    def _():
        for h in range(n_heads):
            sl = slice(h * LANES, (h + 1) * LANES)
            r = _rope_tile(r_ref[:, sl], cos_ref[...], slo_ref[...], shi_ref[...], rot_half)
            k_ref[:, sl] = r
            kb_ref[:, sl] = r.astype(BF16)

    @pl.when(j == 2)
    def _():
        v_ref[...] = r_ref[...]
        vb_ref[...] = r_ref[...].astype(BF16)


def _inproj(x, w, cos, slo, shi, *, tm, tn, q_scale, rot_half, n_heads):
    m, d = x.shape
    n = w.shape[1]
    nj = n // tn
    n_first = N_QKV_TILES + N_CONV_TILES
    n_gate_tiles = nj - n_first
    tab_blocks = cos.shape[0] // tm
    tab_spec = pl.BlockSpec((tm, LANES), lambda i, j: (i % tab_blocks, 0))
    row_spec = pl.BlockSpec((tm, tn), lambda i, j: (i, 0))
    return pl.pallas_call(
        functools.partial(_inproj_kernel, q_scale=q_scale, rot_half=rot_half, n_heads=n_heads),
        out_shape=(
            jax.ShapeDtypeStruct((m, tn), BF16),
            jax.ShapeDtypeStruct((m, tn), F32),
            jax.ShapeDtypeStruct((m, tn), F32),
            jax.ShapeDtypeStruct((m, tn), BF16),
            jax.ShapeDtypeStruct((m, tn), BF16),
            jax.ShapeDtypeStruct((N_CONV_TILES, m, tn), F32),
            jax.ShapeDtypeStruct((n_gate_tiles, m, tn), BF16),
        ),
        grid=(m // tm, nj),
        in_specs=[
            pl.BlockSpec((tm, d), lambda i, j: (i, 0)),
            pl.BlockSpec((d, tn), lambda i, j: (0, j)),
            tab_spec, tab_spec, tab_spec,
        ],
        out_specs=(row_spec, row_spec, row_spec, row_spec, row_spec,
                   pl.BlockSpec((None, tm, tn),
                                lambda i, j: (jnp.clip(j - N_QKV_TILES, 0, N_CONV_TILES - 1), i, 0)),
                   pl.BlockSpec((None, tm, tn), lambda i, j: (jnp.maximum(j - n_first, 0), i, 0))),
        scratch_shapes=[pltpu.VMEM((tm, d), BF16)],
        compiler_params=pltpu.CompilerParams(
            dimension_semantics=("arbitrary", "arbitrary"), vmem_limit_bytes=VMEM_LIMIT),
        name="inproj",
    )(x, w, cos, slo, shi)


def _lambda_value(lq1_ref, lk1_ref, lq2_ref, lk2_ref, lam_init):
    s1 = jnp.sum(lq1_ref[...] * lk1_ref[...], axis=-1, keepdims=True)
    s2 = jnp.sum(lq2_ref[...] * lk2_ref[...], axis=-1, keepdims=True)
    return jnp.exp(s1) - jnp.exp(s2) + lam_init


def _head_rmsnorm(o, g, lam_init):
    ms = jnp.mean(o * o, axis=-1, keepdims=True)
    return o * lax.rsqrt(ms + LN_EPS) * g * (1.0 - lam_init)


def _reduce_keys(x, reduce_fn, combine_fn, groups=8):
    n, w = x.shape
    x3 = x.reshape(n // SUBLANES, SUBLANES, w)
    per = x3.shape[0] // groups
    parts = [reduce_fn(x3[g * per:(g + 1) * per], axis=0) for g in range(groups)]
    while len(parts) > 1:
        parts = [combine_fn(parts[i], parts[i + 1]) for i in range(0, len(parts), 2)]
    return reduce_fn(parts[0], axis=0, keepdims=True)


def _prompt_attn_unit(qi, q_ref, k_ref, v_ref, lam, g_ref, o_ref, vt_ref,
                      *, lam_init, nq, row_chunks):
    tq, vd = q_ref.shape
    head_dim = vd // 2
    rc = tq // row_chunks
    seq = k_ref.shape[0]

    @pl.when(qi == 0)
    def _():
        for c in range(seq // tq):
            cols = slice(c * tq, (c + 1) * tq)
            vt_ref[:, cols] = v_ref[cols, :].astype(F32).T.astype(BF16)

    q = q_ref[...]
    lane = lax.broadcasted_iota(jnp.int32, q.shape, 1)
    zero = jnp.zeros_like(q)
    q_sub = (jnp.where(lane < head_dim, q, zero), jnp.where(lane >= head_dim, q, zero))
    key = lax.broadcasted_iota(jnp.int32, (rc, rc), 0)
    qry = lax.broadcasted_iota(jnp.int32, (rc, rc), 1)
    keep = jnp.concatenate([key <= qry, key <= qry], axis=1)

    def softmax_av(qc, ext):
        s = lax.dot_general(k_ref[0:ext, :], qc, (((1,), (1,)), ((), ())), preferred_element_type=F32)
        diag = jnp.where(keep, s[ext - rc:, :], NEG_INF)
        s = diag if ext == rc else jnp.concatenate([s[:ext - rc, :], diag], axis=0)
        m = _reduce_keys(s, jnp.max, jnp.maximum)
        p = jnp.exp(s - m)
        l = _reduce_keys(p, jnp.sum, jnp.add)
        return jnp.dot(vt_ref[:, 0:ext], p.astype(BF16), preferred_element_type=F32) / l

    for i in range(nq):
        @pl.when(qi == i)
        def _(i=i):
            for r in range(row_chunks):
                rows = slice(r * rc, (r + 1) * rc)
                ext = i * tq + (r + 1) * rc
                o = softmax_av(jnp.concatenate([q_sub[0][rows], q_sub[1][rows]], axis=0), ext)
                o = o[:, :rc] - lam * o[:, rc:]
                o_ref[rows, :] = _head_rmsnorm(o.T, g_ref[...], lam_init).astype(BF16)


def _class_reduce(x, op, n_classes):
    shift = n_classes
    while shift < x.shape[-1]:
        x = op(x, pltpu.roll(x, shift, 1))
        shift *= 2
    return x


def _attn_kernel(pt_ref, q_ref, kn_ref, vn_ref, lq1_ref, lk1_ref, lq2_ref, lk2_ref, g_ref,
                 qp_ref, kp_ref, vp_ref, ck_hbm, cv_hbm, o_ref, op_ref,
                 s_scr, a_scr, qt_scr, anew_scr, acc_scr, vt_scr, page_buf, page_sem,
                 *, lam_init, pages_per_step, n_pages, steps_per_unit, nq, row_chunks):
    pp = pages_per_step
    g = pl.program_id(1)
    ng = n_pages // pp
    steps_per_seq = 2 * ng
    n_steps = pl.num_programs(0) * steps_per_seq
    step = pl.program_id(0) * steps_per_seq + g

    def page_copy(src_hbm, page_id, slot):
        return pltpu.make_async_copy(src_hbm.at[page_id], page_buf.at[slot], page_sem.at[slot])

    def fetch_group(t):
        seq_id = t // steps_per_seq
        j = t % steps_per_seq
        is_k = j < ng
        first = seq_id * n_pages + jnp.where(is_k, j, j - ng) * pp
        slot0 = (t % PAGE_GROUPS) * pp

        @pl.when(is_k)
        def _():
            for p in range(pp):
                page_copy(ck_hbm, pt_ref[first + p], slot0 + p).start()

        @pl.when(jnp.logical_not(is_k))
        def _():
            for p in range(pp):
                page_copy(cv_hbm, pt_ref[first + p], slot0 + p).start()

    @pl.when(step == 0)
    def _():
        for t in range(PAGE_GROUPS - 1):
            fetch_group(jnp.int32(t))

    @pl.when(step + PAGE_GROUPS - 1 < n_steps)
    def _():
        fetch_group(step + PAGE_GROUPS - 1)

    @pl.when(step % steps_per_unit == 0)
    def _():
        lam = _lambda_value(lq1_ref, lk1_ref, lq2_ref, lk2_ref, lam_init)
        _prompt_attn_unit((step // steps_per_unit) % nq, qp_ref, kp_ref, vp_ref, lam, g_ref, op_ref, vt_scr,
                          lam_init=lam_init, nq=nq, row_chunks=row_chunks)

    slot0 = (step % PAGE_GROUPS) * pp
    for p in range(pp):
        page_copy(ck_hbm, 0, slot0 + p).wait()

    _, page, n_heads, vd = page_buf.shape
    rows = page * n_heads
    hd = vd // 2
    nt = (((1,), (1,)), ((), ()))

    def pad_rows(x):
        return jnp.concatenate([x, jnp.zeros((LANES - n_heads, vd), x.dtype)], axis=0)

    @pl.when(g == 0)
    def _():
        q = q_ref[...]
        qq = jnp.concatenate([q, q], axis=0)
        row = lax.broadcasted_iota(jnp.int32, qq.shape, 0)
        lane = lax.broadcasted_iota(jnp.int32, qq.shape, 1)
        qt_scr[...] = jnp.where(row // n_heads == lane // hd, qq, jnp.zeros_like(qq))

    @pl.when(g < ng)
    def _():
        for p in range(pp):
            kb = page_buf[slot0 + p].reshape(rows, vd).astype(BF16)
            s_scr[g * pp + p] = lax.dot_general(qt_scr[...], kb, nt, preferred_element_type=F32)

    @pl.when(g == ng)
    def _():
        lane1 = lax.broadcasted_iota(jnp.int32, (2 * n_heads, LANES), 1)
        s_new = lax.dot_general(qt_scr[...], pad_rows(kn_ref[...]).astype(BF16), nt,
                                preferred_element_type=F32)
        s_new = jnp.where(lane1 < n_heads, s_new, NEG_INF)

        def with_new(x, x_new, op):
            return jnp.concatenate([op(x[:, :LANES], x_new), x[:, LANES:]], axis=1)

        m = lax.fori_loop(0, n_pages, lambda i, m: jnp.maximum(m, s_scr[i]),
                          jnp.full((2 * n_heads, rows), NEG_INF, F32), unroll=SOFTMAX_UNROLL)
        m = _class_reduce(with_new(m, s_new, jnp.maximum), jnp.maximum, n_heads)

        def sum_body(i, l):
            p = jnp.exp(s_scr[i] - m)
            s_scr[i] = p
            return l + p
        l = lax.fori_loop(0, n_pages, sum_body, jnp.zeros((2 * n_heads, rows), F32), unroll=SOFTMAX_UNROLL)
        p_new = jnp.exp(s_new - m[:, :LANES])
        l = _class_reduce(with_new(l, p_new, jnp.add), jnp.add, n_heads)

        lam = _lambda_value(lq1_ref, lk1_ref, lq2_ref, lk2_ref, lam_init)
        row = lax.broadcasted_iota(jnp.int32, (n_heads, rows), 0)
        lane = lax.broadcasted_iota(jnp.int32, (n_heads, rows), 1)
        own = row == lane % n_heads

        def a_body(i, c):
            pn = s_scr[i] / l
            a_scr[i] = jnp.where(own, pn[:n_heads] - lam * pn[n_heads:], 0.0)
            return c
        lax.fori_loop(0, n_pages, a_body, 0, unroll=SOFTMAX_UNROLL)
        pn_new = p_new / l[:, :LANES]
        anew_scr[...] = jnp.where(own[:, :LANES], pn_new[:n_heads] - lam * pn_new[n_heads:], 0.0)
        acc_scr[...] = jnp.zeros(acc_scr.shape, F32)

    @pl.when(g >= ng)
    def _():
        acc = acc_scr[...]
        for p in range(pp):
            vb = page_buf[slot0 + p].reshape(rows, vd).astype(BF16)
            acc = acc + jnp.dot(a_scr[(g - ng) * pp + p].astype(BF16), vb, preferred_element_type=F32)
        acc_scr[...] = acc

    @pl.when(g == 2 * ng - 1)
    def _():
        o = acc_scr[...] + jnp.dot(anew_scr[...].astype(BF16), pad_rows(vn_ref[...]).astype(BF16),
                                   preferred_element_type=F32)
        o_ref[...] = _head_rmsnorm(o, g_ref[...], lam_init).astype(o_ref.dtype)


def _attention(page_table, q, k_new, v_new, cache_k, cache_v, qp, kp, vp, lam_vecs, subln_g,
               *, batch, seq, lam_init, pages_per_step, tq, row_chunks):
    nb = q.shape[0]
    n_pages = page_table.shape[1]
    pp = pages_per_step
    ng = n_pages // pp
    _, page, n_heads, vd = cache_k.shape
    rows = page * n_heads
    pt_flat = page_table.reshape(-1)
    nq = seq // tq
    n_units = batch * n_heads * nq
    n_steps = nb * 2 * ng
    steps_per_unit = n_steps // n_units
    assert steps_per_unit * n_units == n_steps

    def unit(b, g):
        u = (b * (2 * ng) + g) // steps_per_unit
        return u // (n_heads * nq), (u // nq) % n_heads, u % nq

    def qp_map(b, g, pt):
        pb, h, i = unit(b, g)
        return pb * nq + i, h

    def kvp_map(b, g, pt):
        pb, h, _ = unit(b, g)
        return pb, h

    head_spec = pl.BlockSpec((None, n_heads, vd), lambda b, g, pt: (b, 0, 0))
    vec = pl.BlockSpec((1, lam_vecs[0].shape[1]), lambda b, g, pt: (0, 0))

    grid_spec = pltpu.PrefetchScalarGridSpec(
        num_scalar_prefetch=1,
        grid=(nb, 2 * ng),
        in_specs=[head_spec, head_spec, head_spec, vec, vec, vec, vec,
                  pl.BlockSpec((1, vd), lambda b, g, pt: (0, 0)),
                  pl.BlockSpec((tq, vd), qp_map),
                  pl.BlockSpec((seq, vd), kvp_map),
                  pl.BlockSpec((seq, vd), kvp_map),
                  pl.BlockSpec(memory_space=pl.ANY),
                  pl.BlockSpec(memory_space=pl.ANY)],
        out_specs=(head_spec, pl.BlockSpec((tq, vd), qp_map)),
        scratch_shapes=[
            pltpu.VMEM((n_pages, 2 * n_heads, rows), F32),
            pltpu.VMEM((n_pages, n_heads, rows), F32),
            pltpu.VMEM((2 * n_heads, vd), BF16),
            pltpu.VMEM((n_heads, LANES), F32),
            pltpu.VMEM((n_heads, vd), F32),
            pltpu.VMEM((vd, seq), BF16),
            pltpu.VMEM((PAGE_GROUPS * pp, page, n_heads, vd), F32),
            pltpu.SemaphoreType.DMA((PAGE_GROUPS * pp,)),
        ],
    )
    return pl.pallas_call(
        functools.partial(_attn_kernel, lam_init=lam_init, pages_per_step=pp, n_pages=n_pages,
                          steps_per_unit=steps_per_unit, nq=nq, row_chunks=row_chunks),
        out_shape=(jax.ShapeDtypeStruct((nb, n_heads, vd), BF16),
                   jax.ShapeDtypeStruct(qp.shape, BF16)),
        grid_spec=grid_spec,
        compiler_params=pltpu.CompilerParams(
            dimension_semantics=("arbitrary", "arbitrary"), vmem_limit_bytes=VMEM_LIMIT),
        name="attention",
    )(pt_flat, q, k_new, v_new, *lam_vecs, subln_g, qp, kp, vp, cache_k, cache_v)


def _mixer_tail(o, cb, conv, gate_refs, x, wa_ref, wc_ref, wo_ref, g_ref, b_ref, alpha):
    ga0_ref, ga1_ref, gc0_ref, gc1_ref = gate_refs
    ga = jnp.concatenate([ga0_ref[...], ga1_ref[...]], axis=1).astype(F32)
    gc = jnp.concatenate([gc0_ref[...], gc1_ref[...]], axis=1).astype(F32)
    y_attn = jnp.dot(o, wa_ref[...], preferred_element_type=F32)
    y_conv = jnp.dot((cb * conv).astype(BF16), wc_ref[...], preferred_element_type=F32)
    merged = jax.nn.sigmoid(ga) * y_attn + jax.nn.sigmoid(gc) * y_conv
    z = jnp.dot(merged.astype(BF16), wo_ref[...], preferred_element_type=F32)
    return _layer_norm(alpha * x + z, g_ref[...], b_ref[...])


def _merge_prompt_kernel(o_ref, cb_ref, cc_ref, ch_ref, pc_ref, ph_ref,
                         ga0_ref, ga1_ref, gc0_ref, gc1_ref, x_ref,
                         cw_ref, wa_ref, wc_ref, wo_ref, g_ref, b_ref,
                         out_ref, tail_ref, u_scr, *, alpha, tiles_per_seq):
    i = pl.program_id(0)
    tm = cc_ref.shape[0]
    u = cc_ref[...] * ch_ref[...]
    prev = pc_ref[...] * ph_ref[...]
    prev = jnp.where(i % tiles_per_seq == 0, jnp.zeros_like(prev), prev)
    u_scr[0:SUBLANES, :] = prev
    u_scr[SUBLANES:SUBLANES + tm, :] = u
    u1 = u_scr[SUBLANES - 1:SUBLANES - 1 + tm, :]
    u2 = u_scr[SUBLANES - 2:SUBLANES - 2 + tm, :]
    conv = cw_ref[0:1, :] * u2 + cw_ref[1:2, :] * u1 + cw_ref[2:3, :] * u
    tail_ref[...] = u[tm - SUBLANES:tm, :]
    out_ref[...] = _mixer_tail(o_ref[...], cb_ref[...], conv, (ga0_ref, ga1_ref, gc0_ref, gc1_ref),
                               x_ref[...], wa_ref, wc_ref, wo_ref, g_ref, b_ref, alpha)


def _merge_sample_kernel(o_ref, cb_ref, cc_ref, ch_ref, s0_ref, s1_ref,
                         ga0_ref, ga1_ref, gc0_ref, gc1_ref, x_ref,
                         cw_ref, wa_ref, wc_ref, wo_ref, g_ref, b_ref,
                         out_ref, u_ref, *, alpha):
    u = cc_ref[...] * ch_ref[...]
    conv = cw_ref[0:1, :] * s0_ref[...] + cw_ref[1:2, :] * s1_ref[...] + cw_ref[2:3, :] * u
    u_ref[...] = u
    out_ref[...] = _mixer_tail(o_ref[...], cb_ref[...], conv, (ga0_ref, ga1_ref, gc0_ref, gc1_ref),
                               x_ref[...], wa_ref, wc_ref, wo_ref, g_ref, b_ref, alpha)


def _const_spec(shape):
    return pl.BlockSpec(shape, lambda i: (0,) * len(shape), pipeline_mode=pl.Buffered(1))


def _merge_prompt(o, conv_in, gates, x, conv_w, wa, wc, wo, g, b, *, alpha, tm, seq):
    m, d = x.shape
    dc = o.shape[1]
    rows8 = tm // SUBLANES
    in_specs = [
        pl.BlockSpec((tm, dc), lambda i: (i, 0)),
        pl.BlockSpec((None, tm, dc), lambda i: (0, i, 0)),
        pl.BlockSpec((None, tm, dc), lambda i: (1, i, 0)),
        pl.BlockSpec((None, tm, dc), lambda i: (2, i, 0)),
        pl.BlockSpec((None, SUBLANES, dc), lambda i: (1, jnp.maximum(i * rows8 - 1, 0), 0)),
        pl.BlockSpec((None, SUBLANES, dc), lambda i: (2, jnp.maximum(i * rows8 - 1, 0), 0)),
        pl.BlockSpec((None, tm, dc), lambda i: (0, i, 0)),
        pl.BlockSpec((None, tm, dc), lambda i: (1, i, 0)),
        pl.BlockSpec((None, tm, dc), lambda i: (2, i, 0)),
        pl.BlockSpec((None, tm, dc), lambda i: (3, i, 0)),
        pl.BlockSpec((tm, d), lambda i: (i, 0)),
        _const_spec(conv_w.shape), _const_spec(wa.shape), _const_spec(wc.shape), _const_spec(wo.shape),
        _const_spec(g.shape), _const_spec(b.shape),
    ]
    return pl.pallas_call(
        functools.partial(_merge_prompt_kernel, alpha=alpha, tiles_per_seq=seq // tm),
        out_shape=(jax.ShapeDtypeStruct((m, d), F32),
                   jax.ShapeDtypeStruct((m // tm * SUBLANES, dc), F32)),
        grid=(m // tm,),
        in_specs=in_specs,
        out_specs=(pl.BlockSpec((tm, d), lambda i: (i, 0)),
                   pl.BlockSpec((SUBLANES, dc), lambda i: (i, 0))),
        scratch_shapes=[pltpu.VMEM((tm + SUBLANES, dc), F32)],
        compiler_params=pltpu.CompilerParams(
            dimension_semantics=("parallel",), vmem_limit_bytes=VMEM_LIMIT),
        name="merge_prompt",
    )(o, conv_in, conv_in, conv_in, conv_in, conv_in, gates, gates, gates, gates, x, conv_w, wa, wc, wo, g, b)


def _merge_sample(o, conv_in, gates, s0, s1, x, conv_w, wa, wc, wo, g, b, *, alpha):
    m, d = x.shape
    dc = o.shape[1]
    in_specs = [
        pl.BlockSpec((m, dc), lambda i: (0, 0)),
        pl.BlockSpec((None, m, dc), lambda i: (0, 0, 0)),
        pl.BlockSpec((None, m, dc), lambda i: (1, 0, 0)),
        pl.BlockSpec((None, m, dc), lambda i: (2, 0, 0)),
        pl.BlockSpec((m, dc), lambda i: (0, 0)),
        pl.BlockSpec((m, dc), lambda i: (0, 0)),
        pl.BlockSpec((None, m, dc), lambda i: (0, 0, 0)),
        pl.BlockSpec((None, m, dc), lambda i: (1, 0, 0)),
        pl.BlockSpec((None, m, dc), lambda i: (2, 0, 0)),
        pl.BlockSpec((None, m, dc), lambda i: (3, 0, 0)),
        pl.BlockSpec((m, d), lambda i: (0, 0)),
        _const_spec(conv_w.shape), _const_spec(wa.shape), _const_spec(wc.shape), _const_spec(wo.shape),
        _const_spec(g.shape), _const_spec(b.shape),
    ]
    return pl.pallas_call(
        functools.partial(_merge_sample_kernel, alpha=alpha),
        out_shape=(jax.ShapeDtypeStruct((m, d), F32), jax.ShapeDtypeStruct((m, dc), F32)),
        grid=(1,),
        in_specs=in_specs,
        out_specs=(pl.BlockSpec((m, d), lambda i: (0, 0)), pl.BlockSpec((m, dc), lambda i: (0, 0))),
        compiler_params=pltpu.CompilerParams(
            dimension_semantics=("arbitrary",), vmem_limit_bytes=VMEM_LIMIT),
        name="merge_sample",
    )(o, conv_in, conv_in, conv_in, s0, s1, gates, gates, gates, gates, x, conv_w, wa, wc, wo, g, b)


def _rope_tables(pos, head_dim, rot_dim):
    half = rot_dim // 2
    inv_freq = jnp.power(ROPE_THETA, -jnp.arange(0, rot_dim, 2, dtype=F32) / rot_dim)
    ang = pos.astype(F32)[:, None] * inv_freq[None, :]
    cos, sin = jnp.cos(ang), jnp.sin(ang)
    n = pos.shape[0]
    ones = jnp.ones((n, head_dim - rot_dim), F32)
    zeros_h = jnp.zeros((n, half), F32)
    zeros_r = jnp.zeros((n, head_dim - rot_dim), F32)
    c = jnp.concatenate([cos, cos, ones], axis=1)
    lo = jnp.concatenate([-sin, zeros_h, zeros_r], axis=1)
    hi = jnp.concatenate([zeros_h, sin, zeros_r], axis=1)
    two = lambda t: jnp.concatenate([t, t], axis=1)
    return two(c), two(lo), two(hi)


def kernel(x_prompt, x_sample, cache_k, cache_v, state_conv, page_table, ln_g, ln_b, ffn1_w_gate, ffn1_w_up, ffn1_w_down, w_in, conv_w, lambda_q1, lambda_k1, lambda_q2, lambda_k2, subln_g, w_attn_out, w_conv_out, w_o, ffn2_w_gate, ffn2_w_up, ffn2_w_down):
    batch, seq, d_model = x_prompt.shape
    dec_batch, dec_seq, _ = x_sample.shape
    assert dec_seq == 1
    depth = ln_g.shape[0]
    page_size, n_heads, v_dim = cache_k.shape[2:]
    head_dim = v_dim // 2
    rot_dim = head_dim // 4
    width = n_heads * v_dim
    d_conv = conv_w.shape[2]
    assert width == d_conv and v_dim == LANES and n_heads == SUBLANES
    past_len = page_table.shape[1] * page_size
    alpha = (2.0 * depth) ** 0.25
    q_scale = head_dim ** -0.5

    tabs_p = _rope_tables(jnp.arange(seq, dtype=jnp.int32), head_dim, rot_dim)
    tabs_s = _rope_tables(jnp.full((dec_batch,), past_len, jnp.int32), head_dim, rot_dim)

    y_p = x_prompt.reshape(batch * seq, d_model)
    y_s = x_sample.reshape(dec_batch, d_model)
    outs = [[] for _ in range(6)]
    for layer in range(depth):
        lam_init = 0.8 - 0.6 * math.exp(-0.3 * layer)
        bf = lambda w: w[layer].astype(BF16)
        wg1, wu1, wd1 = ffn1_w_gate[layer], ffn1_w_up[layer], ffn1_w_down[layer]
        wg2, wu2, wd2 = ffn2_w_gate[layer], ffn2_w_up[layer], ffn2_w_down[layer]
        win, wa, wc, wo = bf(w_in), bf(w_attn_out), bf(w_conv_out), bf(w_o)
        g = [ln_g[layer, i][None, :] for i in range(3)]
        b = [ln_b[layer, i][None, :] for i in range(3)]
        lam_vecs = [v[layer][None, :] for v in (lambda_q1, lambda_k1, lambda_q2, lambda_k2)]
        sg = subln_g[layer][None, :]
        cw = conv_w[layer]

        x1, s1 = _ffn_ln(y_p, y_s, wg1, wu1, wd1, g[0], b[0], alpha=alpha, tm=1024, tf=256)
        q, k, v, kb, vb, conv_in, gates = _inproj(x1, win, *tabs_p, tm=512, tn=width, q_scale=q_scale,
                                                  rot_half=rot_dim // 2, n_heads=n_heads)
        qs, ks, vs, _, _, conv_in_s, gates_s = _inproj(s1, win, *tabs_s, tm=dec_batch, tn=width,
                                                       q_scale=q_scale, rot_half=rot_dim // 2, n_heads=n_heads)

        heads = lambda t: t.reshape(dec_batch, n_heads, v_dim)
        os_, o = _attention(page_table, heads(qs), heads(ks), heads(vs), cache_k[layer], cache_v[layer],
                            q, kb, vb, lam_vecs, sg, batch=batch, seq=seq, lam_init=lam_init,
                            pages_per_step=16, tq=512, row_chunks=2)

        x2, tails = _merge_prompt(o, conv_in, gates, x1, cw, wa, wc, wo, g[1], b[1],
                                  alpha=alpha, tm=256, seq=seq)
        st = state_conv[layer]
        s2, u_s = _merge_sample(os_.reshape(dec_batch, width), conv_in_s, gates_s, st[:, 0], st[:, 1], s1,
                                cw, wa, wc, wo, g[1], b[1], alpha=alpha)
        y_p, y_s = _ffn_ln(x2, s2, wg2, wu2, wd2, g[2], b[2], alpha=alpha, tm=1024, tf=256)
        tails = tails.reshape(batch, seq // 256, SUBLANES, d_conv)
        outs[0].append(k.reshape(batch, seq, n_heads, v_dim))
        outs[1].append(v.reshape(batch, seq, n_heads, v_dim))
        outs[2].append(tails[:, -1, SUBLANES - 2:, :])
        outs[3].append(ks.reshape(dec_batch, 1, n_heads, v_dim))
        outs[4].append(vs.reshape(dec_batch, 1, n_heads, v_dim))
        outs[5].append(jnp.stack([st[:, 1], u_s], axis=1))

    return (y_p.reshape(batch, seq, d_model), y_s.reshape(dec_batch, 1, d_model),
            jnp.stack(outs[0]), jnp.stack(outs[1]), jnp.stack(outs[2]),
            jnp.stack(outs[3]), jnp.stack(outs[4]), jnp.stack(outs[5]))
```
